```python
import math
import jax, jax.numpy as jnp
from jax import lax
import numpy as np

D_MODEL = 2048
BATCH = 4
SEQ = 2048
DEPTH = 4
DEC_BATCH = 2
DEC_SEQ = 16384
PAST_LEN = 128

N_MIXERS = 2
N_ATTN_LAYERS = (DEPTH + 1) // 2
N_HYENA_LAYERS = DEPTH // 2
HEAD_DIM = 128
HEADS_PER_GROUP = D_MODEL // HEAD_DIM
GROUP_WINDOWS = (128, 512, 2048)
GROUP_DILATIONS = (1, 4, 16)
N_GROUPS = 3
QKV_WIDTH = N_GROUPS * 3 * HEADS_PER_GROUP * HEAD_DIM
ATTN_WIDTH = HEADS_PER_GROUP * HEAD_DIM
ROPE_THETA = 10000.0
D_FF = 4 * D_MODEL
SHORT_CONV = 3
FILTER_EMB = 33
FILTER_BANDS = (FILTER_EMB - 1) // 2
FILTER_HIDDEN = 64
DECAY_TARGET = 1e-2
FAST_DECAY_PCT = 0.3
SLOW_DECAY_PCT = 1.5
EPS = 1e-6

kernel_name = "hybrid_dilated_attn_hyena_encoder"


def rms_norm(x, g):
    xf = x.astype(jnp.float32)
    y = xf * lax.rsqrt(jnp.mean(xf * xf, axis=-1, keepdims=True) + EPS)
    return (y * g.astype(jnp.float32)).astype(x.dtype)


def rope_tables(S):
    inv = ROPE_THETA ** (-jnp.arange(0, HEAD_DIM, 2, dtype=jnp.float32) / HEAD_DIM)
    ang = jnp.arange(S, dtype=jnp.float32)[:, None] * inv[None, :]
    return jnp.cos(ang), jnp.sin(ang)


def apply_rope(t, cos, sin):
    half = HEAD_DIM // 2
    t1, t2 = t[..., :half], t[..., half:]
    c = cos[None, :, None, :]
    s = sin[None, :, None, :]
    return jnp.concatenate([t1 * c - t2 * s, t2 * c + t1 * s], axis=-1)


def dilated_window_attention(q, k, v, dilation, half):
    B, S, H, Dh = q.shape
    n = S // dilation
    nb = -(-n // half)
    L = nb * half

    def by_stride(t):
        return t.reshape(B, n, dilation, H, Dh).transpose(0, 2, 3, 1, 4)

    qb = jnp.pad(by_stride(q), ((0, 0), (0, 0), (0, 0), (0, L - n), (0, 0))).reshape(B, dilation, H, nb, half, Dh)

    def neighbours(t):
        tp = jnp.pad(by_stride(t), ((0, 0), (0, 0), (0, 0), (half, L - n + half), (0, 0))).reshape(B, dilation, H, nb + 2, half, Dh)
        return jnp.concatenate([tp[:, :, :, :-2], tp[:, :, :, 1:-1], tp[:, :, :, 2:]], axis=4)

    kb = neighbours(k)
    vb = neighbours(v)
    s = jnp.einsum('brhnqd,brhnkd->brhnqk', qb, kb)
    jq = jnp.arange(nb)[:, None] * half + jnp.arange(half)[None, :]
    jk = (jnp.arange(nb)[:, None] - 1) * half + jnp.arange(3 * half)[None, :]
    rel = jk[:, None, :] - jq[:, :, None]
    valid = (jnp.abs(rel) <= half) & (jk[:, None, :] >= 0) & (jk[:, None, :] < n)
    s = jnp.where(valid, s, -jnp.inf)
    m = jnp.max(s, axis=-1, keepdims=True)
    p = jnp.exp(s - m)
    den = jnp.sum(p, axis=-1)
    o = jnp.einsum('brhnqk,brhnkd->brhnqd', p, vb) / den[..., None]
    lse = m[..., 0] + jnp.log(den)
    o = o.reshape(B, dilation, H, L, Dh)[:, :, :, :n].transpose(0, 3, 1, 2, 4).reshape(B, S, H, Dh)
    lse = lse.reshape(B, dilation, H, L)[..., :n].transpose(0, 3, 1, 2).reshape(B, S, H)
    return o, lse


def attention_mixer(xn, w_qkv, q_gain, k_gain, w_out, cos, sin):
    B, S, _ = xn.shape
    qkv = (xn @ w_qkv).reshape(B, S, N_GROUPS, 3, HEADS_PER_GROUP, HEAD_DIM).astype(jnp.float32)
    scale = HEAD_DIM ** -0.5
    outs = []
    lses = []
    for g in range(N_GROUPS):
        q = apply_rope(rms_norm(qkv[:, :, g, 0], q_gain[g]), cos, sin) * scale
        k = apply_rope(rms_norm(qkv[:, :, g, 1], k_gain[g]), cos, sin)
        dil = GROUP_DILATIONS[g]
        o, l = dilated_window_attention(q, k, qkv[:, :, g, 2], dil, GROUP_WINDOWS[g] // (2 * dil))
        outs.append(o)
        lses.append(l)
    w = jax.nn.softmax(jnp.stack(lses), axis=0)
    o = jnp.einsum('gbsh,gbshd->bshd', w, jnp.stack(outs))
    return o.reshape(B, S, ATTN_WIDTH).astype(xn.dtype) @ w_out


def implicit_filters(L, w1, b1, w2, b2, w3, b3, freq, w4):
    f32 = jnp.float32
    pos = jnp.arange(L, dtype=f32)
    t = pos / (L - 1)
    bands = jnp.linspace(1e-4, FILTER_BANDS - 1, FILTER_BANDS, dtype=f32)
    ang = (2.0 * math.pi / L) * pos[:, None] * bands[None, :]
    z = jnp.concatenate([t[:, None], jnp.cos(ang), -jnp.sin(ang)], axis=-1)
    fr = freq.astype(f32)
    h = jnp.sin(fr * (z @ w1.astype(f32) + b1.astype(f32)))
    h = jnp.sin(fr * (h @ w2.astype(f32) + b2.astype(f32)))
    h = jnp.sin(fr * (h @ w3.astype(f32) + b3.astype(f32)))
    h = h @ w4.astype(f32)
    deltas = jnp.abs(jnp.linspace(math.log(DECAY_TARGET) / SLOW_DECAY_PCT, math.log(DECAY_TARGET) / FAST_DECAY_PCT, D_MODEL, dtype=f32))
    decay = jnp.exp(-t[:, None] * deltas[None, :])
    return h[:, :D_MODEL] * decay, h[:, D_MODEL:] * decay


def bidirectional_long_conv(z, h_fwd, h_bwd):
    L = z.shape[1]
    kern = jnp.concatenate([h_fwd, jnp.zeros((1, D_MODEL), jnp.float32), h_bwd[1:][::-1]], axis=0)
    zf = jnp.fft.rfft(z, n=2 * L, axis=1)
    kf = jnp.fft.rfft(kern, axis=0)
    return jnp.fft.irfft(zf * kf[None], n=2 * L, axis=1)[:, :L]


def hyena_mixer(xn, w_in, b_in, conv_w, conv_b, fw1, fb1, fw2, fb2, fw3, fb3, ffreq, fw4, skip, w_out, b_out):
    B, L, _ = xn.shape
    u = xn @ w_in + b_in
    up = jnp.pad(u, ((0, 0), (1, 1), (0, 0)))
    u = up[:, :-2] * conv_w[0] + up[:, 1:-1] * conv_w[1] + up[:, 2:] * conv_w[2] + conv_b
    x0, x1, v = jnp.split(u.astype(jnp.float32), 3, axis=-1)
    h_fwd, h_bwd = implicit_filters(L, fw1, fb1, fw2, fb2, fw3, fb3, ffreq, fw4)
    z = v * x1
    z = bidirectional_long_conv(z, h_fwd, h_bwd) + z * skip.astype(jnp.float32)
    y = (z * x0).astype(xn.dtype)
    return y @ w_out + b_out


def squared_relu_mlp(xn, w1, w2):
    h = jax.nn.relu(xn @ w1)
    return (h * h) @ w2


def trunk(x, mix_norm, mlp_norm, attn_w_qkv, attn_q_gain, attn_k_gain, attn_w_out,
          hy_w_in, hy_b_in, hy_conv_w, hy_conv_b, hy_filt_w1, hy_filt_b1, hy_filt_w2, hy_filt_b2,
          hy_filt_w3, hy_filt_b3, hy_filt_freq, hy_filt_w4, hy_skip, hy_w_out, hy_b_out, mlp_w1, mlp_w2):
    cos, sin = rope_tables(x.shape[1])
    for i in range(DEPTH):
        xn = rms_norm(x, mix_norm[i])
        j = i // N_MIXERS
        if i % N_MIXERS == 0:
            x = x + attention_mixer(xn, attn_w_qkv[j], attn_q_gain[j], attn_k_gain[j], attn_w_out[j], cos, sin)
        else:
            x = x + hyena_mixer(xn, hy_w_in[j], hy_b_in[j], hy_conv_w[j], hy_conv_b[j],
                                hy_filt_w1[j], hy_filt_b1[j], hy_filt_w2[j], hy_filt_b2[j],
                                hy_filt_w3[j], hy_filt_b3[j], hy_filt_freq[j], hy_filt_w4[j],
                                hy_skip[j], hy_w_out[j], hy_b_out[j])
        x = x + squared_relu_mlp(rms_norm(x, mlp_norm[i]), mlp_w1[i], mlp_w2[i])
    return x


def setup_inputs(seed: int = 0) -> dict:
    key = jax.random.key(seed)
    ks = jax.random.split(key, 26)

    def nrm(k, shape, scale):
        return jax.random.normal(k, shape, jnp.float32) * scale

    D = D_MODEL
    NA = N_ATTN_LAYERS
    NH = N_HYENA_LAYERS
    FH = FILTER_HIDDEN
    return {
        "x_prompt": nrm(ks[0], (BATCH, SEQ, D), 1.0),
        "x_sample": nrm(ks[1], (DEC_BATCH, DEC_SEQ, D), 1.0),
        "mix_norm": 1.0 + nrm(ks[2], (DEPTH, D), 0.02),
        "mlp_norm": 1.0 + nrm(ks[3], (DEPTH, D), 0.02),
        "attn_w_qkv": nrm(ks[4], (NA, D, QKV_WIDTH), D ** -0.5),
        "attn_q_gain": 1.0 + nrm(ks[5], (NA, N_GROUPS, HEAD_DIM), 0.02),
        "attn_k_gain": 1.0 + nrm(ks[6], (NA, N_GROUPS, HEAD_DIM), 0.02),
        "attn_w_out": nrm(ks[7], (NA, ATTN_WIDTH, D), ATTN_WIDTH ** -0.5),
        "hy_w_in": nrm(ks[8], (NH, D, 3 * D), D ** -0.5),
        "hy_b_in": nrm(ks[9], (NH, 3 * D), 0.02),
        "hy_conv_w": nrm(ks[10], (NH, SHORT_CONV, 3 * D), 0.5),
        "hy_conv_b": nrm(ks[11], (NH, 3 * D), 0.02),
        "hy_filt_w1": nrm(ks[12], (NH, FILTER_EMB, FH), FILTER_EMB ** -0.5),
        "hy_filt_b1": nrm(ks[13], (NH, FH), 0.02),
        "hy_filt_w2": nrm(ks[14], (NH, FH, FH), FH ** -0.5),
        "hy_filt_b2": nrm(ks[15], (NH, FH), 0.02),
        "hy_filt_w3": nrm(ks[16], (NH, FH, FH), FH ** -0.5),
        "hy_filt_b3": nrm(ks[17], (NH, FH), 0.02),
        "hy_filt_freq": 1.0 + nrm(ks[18], (NH, FH), 0.02),
        "hy_filt_w4": nrm(ks[19], (NH, FH, 2 * D), 0.03 * FH ** -0.5),
        "hy_skip": nrm(ks[20], (NH, D), 0.5),
        "hy_w_out": nrm(ks[21], (NH, D, D), D ** -0.5),
        "hy_b_out": nrm(ks[22], (NH, D), 0.02),
        "mlp_w1": nrm(ks[23], (DEPTH, D, D_FF), D ** -0.5),
        "mlp_w2": nrm(ks[24], (DEPTH, D_FF, D), D_FF ** -0.5),
    }


def reference(x_prompt, x_sample, mix_norm, mlp_norm, attn_w_qkv, attn_q_gain, attn_k_gain, attn_w_out,
              hy_w_in, hy_b_in, hy_conv_w, hy_conv_b, hy_filt_w1, hy_filt_b1, hy_filt_w2, hy_filt_b2,
              hy_filt_w3, hy_filt_b3, hy_filt_freq, hy_filt_w4, hy_skip, hy_w_out, hy_b_out, mlp_w1, mlp_w2):
    weights = (mix_norm, mlp_norm, attn_w_qkv, attn_q_gain, attn_k_gain, attn_w_out,
               hy_w_in, hy_b_in, hy_conv_w, hy_conv_b, hy_filt_w1, hy_filt_b1, hy_filt_w2, hy_filt_b2,
               hy_filt_w3, hy_filt_b3, hy_filt_freq, hy_filt_w4, hy_skip, hy_w_out, hy_b_out, mlp_w1, mlp_w2)
    y_prompt = trunk(x_prompt, *weights)
    y_sample = trunk(x_sample, *weights)
    return (y_prompt, y_sample)
```

```python
import functools
import math

import numpy as np
import jax
import jax.numpy as jnp
from jax import lax
from jax.experimental import pallas as pl
from jax.experimental.pallas import tpu as pltpu

F32 = jnp.float32
BF16 = jnp.bfloat16

HEAD_DIM = 128
GROUP_WINDOWS = (128, 512, 2048)
GROUP_DILATIONS = (1, 4, 16)
ROPE_THETA = 10000.0
FILTER_EMB = 33
FILTER_BANDS = (FILTER_EMB - 1) // 2
DECAY_TARGET = 1e-2
FAST_DECAY_PCT = 0.3
SLOW_DECAY_PCT = 1.5
EPS = 1e-6

LANES = 128
SUBLANES = 8
ATTN_TILE = 1024
FFT_N2 = 128
MIB = 1024 * 1024


def _params(semantics, vmem_mib):
    return pltpu.CompilerParams(dimension_semantics=semantics, vmem_limit_bytes=vmem_mib * MIB)


def _tile(n, pref, quantum=LANES):
    if n <= pref:
        return n
    t = (pref // quantum) * quantum
    while t > quantum and n % t:
        t -= quantum
    assert n % t == 0, (n, pref)
    return t


def _split(x):
    hi = x.astype(BF16)
    lo = (x - hi.astype(F32)).astype(BF16)
    return hi, lo


def _dot(a, b):
    return jnp.dot(a, b, preferred_element_type=F32)


def _dot3(a, b):
    ah, al = _split(a)
    bh, bl = _split(b)
    return _dot(ah, bh) + (_dot(ah, bl) + _dot(al, bh))


def _dot3c(ch, cl, b):
    bh, bl = _split(b)
    return _dot(ch, bh) + (_dot(ch, bl) + _dot(cl, bh))


def _rms_rows_to(x_ref, g_ref, xn_ref, row_chunk):
    def body(c, carry):
        rows = pl.ds(pl.multiple_of(c * row_chunk, row_chunk), row_chunk)
        x = x_ref[rows, :]
        inv = lax.rsqrt(jnp.mean(x * x, axis=-1, keepdims=True) + EPS)
        xn_ref[rows, :] = (x * inv * g_ref[...]).astype(BF16)
        return carry

    lax.fori_loop(0, x_ref.shape[0] // row_chunk, body, 0)


def _norm_mm_kernel(x_ref, g_ref, w_ref, *rest, has_bias, relu2, row_chunk):
    if has_bias:
        b_ref, o_ref, xn_ref = rest
    else:
        o_ref, xn_ref = rest

    @pl.when(pl.program_id(1) == 0)
    def _():
        _rms_rows_to(x_ref, g_ref, xn_ref, row_chunk)

    acc = _dot(xn_ref[...], w_ref[...])
    if has_bias:
        acc = acc + b_ref[...]
    if relu2:
        acc = jnp.maximum(acc, 0.0)
        acc = acc * acc
    o_ref[...] = acc.astype(o_ref.dtype)


def norm_matmul(x, g, w, bias=None, *, relu2=False, out_dtype=F32, tm=1024, tn=1024):
    T, K = x.shape
    N = w.shape[1]
    tm = _tile(T, tm)
    tn = _tile(N, tn)
    in_specs = [
        pl.BlockSpec((tm, K), lambda i, j: (i, 0)),
        pl.BlockSpec((1, K), lambda i, j: (0, 0)),
        pl.BlockSpec((K, tn), lambda i, j: (0, j)),
    ]
    args = [x, g.reshape(1, K), w]
    if bias is not None:
        in_specs.append(pl.BlockSpec((1, tn), lambda i, j: (0, j)))
        args.append(bias.reshape(1, N))
    return pl.pallas_call(
        functools.partial(_norm_mm_kernel, has_bias=bias is not None, relu2=relu2, row_chunk=min(tm, 128)),
        grid=(T // tm, N // tn),
        in_specs=in_specs,
        out_specs=pl.BlockSpec((tm, tn), lambda i, j: (i, j)),
        out_shape=jax.ShapeDtypeStruct((T, N), out_dtype),
        scratch_shapes=[pltpu.VMEM((tm, K), BF16)],
        compiler_params=_params(("parallel", "arbitrary"), 48),
        name="norm_matmul",
    )(*args)


def _qkv_kernel(x_ref, g_ref, w_ref, gain_ref, cos_ref, sin_ref, o_ref, xn_ref, *, row_chunk, tiles_per_part):
    j = pl.program_id(1)

    @pl.when(j == 0)
    def _():
        _rms_rows_to(x_ref, g_ref, xn_ref, row_chunk)

    acc = _dot(xn_ref[...], w_ref[...])
    part = (j // tiles_per_part) % 3
    n_slab = acc.shape[1] // HEAD_DIM

    @pl.when(part == 2)
    def _():
        for s in range(n_slab):
            o_ref[s] = acc[:, s * HEAD_DIM:(s + 1) * HEAD_DIM]

    @pl.when(part != 2)
    def _():
        gain = gain_ref[...]
        cos = cos_ref[...]
        sin = sin_ref[...]
        for s in range(n_slab):
            a = acc[:, s * HEAD_DIM:(s + 1) * HEAD_DIM]
            inv = lax.rsqrt(jnp.mean(a * a, axis=-1, keepdims=True) + EPS)
            y = a * inv * gain
            o_ref[s] = y * cos + pltpu.roll(y, HEAD_DIM // 2, axis=1) * sin


def qkv_project(x, g, w, gains, cos, sin, seq_len, *, tm=1024, tn=1024):
    T, K = x.shape
    N = w.shape[1]
    part_width = N // (len(GROUP_DILATIONS) * 3)
    tm = _tile(seq_len, tm)
    tn = _tile(part_width, tn)
    pos_tiles = seq_len // tm
    tiles_per_part = part_width // tn
    return pl.pallas_call(
        functools.partial(_qkv_kernel, row_chunk=min(tm, 128), tiles_per_part=tiles_per_part),
        grid=(T // tm, N // tn),
        in_specs=[
            pl.BlockSpec((tm, K), lambda i, j: (i, 0)),
            pl.BlockSpec((1, K), lambda i, j: (0, 0)),
            pl.BlockSpec((K, tn), lambda i, j: (0, j)),
            pl.BlockSpec((None, 1, HEAD_DIM), lambda i, j: (j // tiles_per_part, 0, 0)),
            pl.BlockSpec((tm, HEAD_DIM), lambda i, j: (i % pos_tiles, 0)),
            pl.BlockSpec((tm, HEAD_DIM), lambda i, j: (i % pos_tiles, 0)),
        ],
        out_specs=pl.BlockSpec((tn // HEAD_DIM, tm, HEAD_DIM), lambda i, j: (j, i, 0)),
        out_shape=jax.ShapeDtypeStruct((N // HEAD_DIM, T, HEAD_DIM), F32),
        scratch_shapes=[pltpu.VMEM((tm, K), BF16)],
        compiler_params=_params(("parallel", "arbitrary"), 48),
        name="qkv_project",
    )(x, g.reshape(1, K), w, gains, cos, sin)


def _mm_resid_kernel(a_ref, w_ref, *rest, has_bias):
    if has_bias:
        b_ref, r_ref, o_ref = rest
    else:
        r_ref, o_ref = rest
    k = pl.program_id(2)
    part = _dot(a_ref[...].astype(BF16), w_ref[...])

    @pl.when(k == 0)
    def _():
        first = part + r_ref[...]
        if has_bias:
            first = first + b_ref[...]
        o_ref[...] = first

    @pl.when(k > 0)
    def _():
        o_ref[...] += part


def matmul_resid(a, w, resid, bias=None, *, tm=1024, tn=1024, tk=2048):
    T, K = a.shape
    N = w.shape[1]
    tm = _tile(T, tm)
    tn = _tile(N, tn)
    tk = _tile(K, tk)
    in_specs = [
        pl.BlockSpec((tm, tk), lambda i, j, k: (i, k)),
        pl.BlockSpec((tk, tn), lambda i, j, k: (k, j)),
    ]
    args = [a, w]
    if bias is not None:
        in_specs.append(pl.BlockSpec((1, tn), lambda i, j, k: (0, j)))
        args.append(bias.reshape(1, N))
    in_specs.append(pl.BlockSpec((tm, tn), lambda i, j, k: (i, j)))
    args.append(resid)
    return pl.pallas_call(
        functools.partial(_mm_resid_kernel, has_bias=bias is not None),
        grid=(T // tm, N // tn, K // tk),
        in_specs=in_specs,
        out_specs=pl.BlockSpec((tm, tn), lambda i, j, k: (i, j)),
        out_shape=jax.ShapeDtypeStruct((T, N), F32),
        compiler_params=_params(("parallel", "parallel", "arbitrary"), 48),
        name="matmul_resid",
    )(*args)


def _attn_kernel(*refs, tq, dils, half, n_tiles):
    G = len(dils)
    q_refs = refs[0:G]
    kp, kc, kn = refs[G:2 * G], refs[2 * G:3 * G], refs[3 * G:4 * G]
    vp, vc, vn = refs[4 * G:5 * G], refs[5 * G:6 * G], refs[6 * G:7 * G]
    o_ref = refs[7 * G]
    acc_ref, m_ref, l_ref, kbuf, vbuf = refs[7 * G + 1:]
    i = pl.program_id(2)
    has_prev = i > 0
    has_next = i < n_tiles - 1

    for g, d in enumerate(dils):
        n_r = tq // d
        qs = min(128, n_r)
        ks = qs + 2 * half
        n_sub = n_r // qs
        qq = lax.broadcasted_iota(jnp.int32, (qs, ks), 0)
        kk = lax.broadcasted_iota(jnp.int32, (qs, ks), 1)
        band = (kk >= qq) & (kk <= qq + 2 * half)

        for r in range(d):

            def rows(start, size, d=d, r=r):
                if d == 1:
                    return pl.ds(start, size)
                return pl.ds(start * d + r, size, stride=d)

            kbuf[0:half, :] = kp[g][rows(0, half), :].astype(BF16)
            kbuf[half:half + n_r, :] = kc[g][rows(0, n_r), :].astype(BF16)
            kbuf[half + n_r:2 * half + n_r, :] = kn[g][rows(0, half), :].astype(BF16)
            vbuf[0:half, :] = vp[g][rows(0, half), :].astype(BF16)
            vbuf[half:half + n_r, :] = vc[g][rows(0, n_r), :].astype(BF16)
            vbuf[half + n_r:2 * half + n_r, :] = vn[g][rows(0, half), :].astype(BF16)

            for u in range(n_sub):
                q = q_refs[g][rows(u * qs, qs), :].astype(BF16)
                kw = kbuf[u * qs:u * qs + ks, :]
                vw = vbuf[u * qs:u * qs + ks, :]
                s = lax.dot_general(q, kw, (((1,), (1,)), ((), ())), preferred_element_type=F32)
                valid = band
                if u == 0:
                    valid = valid & ((kk >= half) | has_prev)
                if u == n_sub - 1:
                    valid = valid & ((kk < ks - half) | has_next)
                s = jnp.where(valid, s, -jnp.inf)
                m_blk = jnp.max(s, axis=-1, keepdims=True)
                p = jnp.exp(s - m_blk)
                l_blk = jnp.sum(p, axis=-1, keepdims=True)
                pv = _dot(p.astype(BF16), vw)
                out_rows = rows(u * qs, qs)
                if g == 0:
                    m_ref[out_rows, :] = m_blk
                    l_ref[out_rows, :] = l_blk
                    acc_ref[out_rows, :] = pv
                else:
                    m_old = m_ref[out_rows, :]
                    m_new = jnp.maximum(m_old, m_blk)
                    a = jnp.exp(m_old - m_new)
                    b = jnp.exp(m_blk - m_new)
                    m_ref[out_rows, :] = m_new
                    l_ref[out_rows, :] = a * l_ref[out_rows, :] + b * l_blk
                    acc_ref[out_rows, :] = a * acc_ref[out_rows, :] + b * pv

    o_ref[...] = (acc_ref[...] / l_ref[...]).astype(o_ref.dtype)


def window_attention(qkv, batch, seq_len, n_heads):
    G = len(GROUP_DILATIONS)
    tq = ATTN_TILE
    assert seq_len % tq == 0
    n_tiles = seq_len // tq
    half = GROUP_WINDOWS[0] // (2 * GROUP_DILATIONS[0])
    for wdw, d in zip(GROUP_WINDOWS, GROUP_DILATIONS):
        assert wdw // (2 * d) == half and half * d <= tq and tq % (d * min(128, tq // d)) == 0
    H = n_heads

    def slab(g, c):
        return lambda b, h, i: ((g * 3 + c) * H + h, b * n_tiles + i, 0)

    def halo(g, c, d, side):
        hb = half * d
        per_tile = tq // hb
        last = batch * seq_len // hb - 1
        if side < 0:
            return lambda b, h, i: ((g * 3 + c) * H + h, jnp.maximum((b * n_tiles + i) * per_tile - 1, 0), 0)
        return lambda b, h, i: ((g * 3 + c) * H + h, jnp.minimum((b * n_tiles + i + 1) * per_tile, last), 0)

    dils = GROUP_DILATIONS
    main = lambda c: [pl.BlockSpec((None, tq, HEAD_DIM), slab(g, c)) for g in range(G)]
    side = lambda c, sd: [pl.BlockSpec((None, half * d, HEAD_DIM), halo(g, c, d, sd)) for g, d in enumerate(dils)]
    in_specs = main(0) + side(1, -1) + main(1) + side(1, +1) + side(2, -1) + main(2) + side(2, +1)
    return pl.pallas_call(
        functools.partial(_attn_kernel, tq=tq, dils=dils, half=half, n_tiles=n_tiles),
        grid=(batch, H, n_tiles),
        in_specs=in_specs,
        out_specs=pl.BlockSpec((tq, HEAD_DIM), lambda b, h, i: (b * n_tiles + i, h)),
        out_shape=jax.ShapeDtypeStruct((batch * seq_len, H * HEAD_DIM), BF16),
        scratch_shapes=[
            pltpu.VMEM((tq, HEAD_DIM), F32),
            pltpu.VMEM((tq, 1), F32),
            pltpu.VMEM((tq, 1), F32),
            pltpu.VMEM((tq + 2 * half, HEAD_DIM), BF16),
            pltpu.VMEM((tq + 2 * half, HEAD_DIM), BF16),
        ],
        compiler_params=_params(("parallel", "parallel", "parallel"), 40),
        name="window_attention",
    )(*([qkv] * (7 * G)))


def _sconv_kernel(*refs, n_tiles):
    (x0m, x0p, x0n, x1m, x1p, x1n, vm, vp, vn, w0, w1, w2, b0, b1, b2, z_ref, x0_ref) = refs
    i = pl.program_id(1)

    def conv(m_ref, p_ref, n_ref, w_ref, b_ref):
        x = m_ref[...]
        ts = x.shape[0]
        prev_row = jnp.where(i > 0, p_ref[7:8, :], 0.0)
        next_row = jnp.where(i < n_tiles - 1, n_ref[0:1, :], 0.0)
        row = lax.broadcasted_iota(jnp.int32, x.shape, 0)
        up = jnp.where(row == 0, prev_row, pltpu.roll(x, 1, axis=0))
        dn = jnp.where(row == ts - 1, next_row, pltpu.roll(x, ts - 1, axis=0))
        return up * w_ref[0:1, :] + x * w_ref[1:2, :] + dn * w_ref[2:3, :] + b_ref[...]

    x0_ref[...] = conv(x0m, x0p, x0n, w0, b0)
    z_ref[...] = conv(vm, vp, vn, w2, b2) * conv(x1m, x1p, x1n, w1, b1)


def short_conv_gate(u, conv_w, conv_b, *, ts=512, ct=512):
    B, L, D3 = u.shape
    D = D3 // 3
    ts = _tile(L, ts, 8)
    ct = _tile(D, ct)
    n_tiles = L // ts
    n_ct = D // ct
    rows8 = ts // 8
    last8 = L // 8 - 1
    specs, args = [], []
    for part in range(3):
        off = part * n_ct
        specs += [
            pl.BlockSpec((None, ts, ct), lambda b, i, j, off=off: (b, i, off + j)),
            pl.BlockSpec((None, 8, ct), lambda b, i, j, off=off: (b, jnp.maximum(i * rows8 - 1, 0), off + j)),
            pl.BlockSpec((None, 8, ct), lambda b, i, j, off=off: (b, jnp.minimum((i + 1) * rows8, last8), off + j)),
        ]
        args += [u, u, u]
    for part in range(3):
        specs.append(pl.BlockSpec((3, ct), lambda b, i, j, off=part * n_ct: (0, off + j)))
        args.append(conv_w)
    for part in range(3):
        specs.append(pl.BlockSpec((1, ct), lambda b, i, j, off=part * n_ct: (0, off + j)))
        args.append(conv_b.reshape(1, D3))
    out_spec = pl.BlockSpec((None, ts, ct), lambda b, i, j: (b, i, j))
    return pl.pallas_call(
        functools.partial(_sconv_kernel, n_tiles=n_tiles),
        grid=(B, n_tiles, n_ct),
        in_specs=specs,
        out_specs=[out_spec, out_spec],
        out_shape=[jax.ShapeDtypeStruct((B, L, D), F32)] * 2,
        compiler_params=_params(("parallel", "parallel", "parallel"), 32),
        name="short_conv_gate",
    )(*args)


def _filter_kernel(feat_ref, tv_ref, w1, b1, w2, b2, w3, b3, fr_ref, w4, delta_ref, o_ref):
    fr = fr_ref[...]
    h = jnp.sin(fr * (_dot3(feat_ref[...], w1[...]) + b1[...]))
    h = jnp.sin(fr * (_dot3(h, w2[...]) + b2[...]))
    h = jnp.sin(fr * (_dot3(h, w3[...]) + b3[...]))
    out = _dot3(h, w4[...])
    t = tv_ref[:, 0:1]
    valid = tv_ref[:, 1:2]
    o_ref[...] = out * jnp.exp(-t * delta_ref[...]) * valid


def conv_kernel_signal(L, D, fw1, fb1, fw2, fb2, fw3, fb3, ffreq, fw4, *, tl=256):
    FH = fw2.shape[0]
    FE = 64
    p = jnp.arange(2 * L, dtype=jnp.int32)
    pos = jnp.where(p < L, p, 2 * L - p).astype(F32)
    t = pos / (L - 1)
    bands = jnp.linspace(1e-4, FILTER_BANDS - 1, FILTER_BANDS, dtype=F32)
    ang = (2.0 * math.pi / L) * pos[:, None] * bands[None, :]
    feat = jnp.concatenate(
        [t[:, None], jnp.cos(ang), -jnp.sin(ang), jnp.zeros((2 * L, FE - FILTER_EMB), F32)], axis=-1)
    tv = jnp.stack([t, (p != L).astype(F32)], axis=-1)
    deltas = np.abs(np.linspace(math.log(DECAY_TARGET) / SLOW_DECAY_PCT, math.log(DECAY_TARGET) / FAST_DECAY_PCT, D,
                                dtype=np.float32)).reshape(1, D)
    w1p = jnp.pad(fw1.astype(F32), ((0, FE - FILTER_EMB), (0, 0)))
    tl = _tile(L, tl, 8)
    half_tiles = L // tl
    full = lambda shape: pl.BlockSpec(shape, lambda i: (0, 0))
    return pl.pallas_call(
        _filter_kernel,
        grid=(2 * L // tl,),
        in_specs=[
            pl.BlockSpec((tl, FE), lambda i: (i, 0)),
            pl.BlockSpec((tl, 2), lambda i: (i, 0)),
            full((FE, FH)), full((1, FH)), full((FH, FH)), full((1, FH)), full((FH, FH)), full((1, FH)),
            full((1, FH)),
            pl.BlockSpec((FH, D), lambda i: (0, i // half_tiles)),
            full((1, D)),
        ],
        out_specs=pl.BlockSpec((tl, D), lambda i: (i, 0)),
        out_shape=jax.ShapeDtypeStruct((2 * L, D), F32),
        compiler_params=_params(("parallel",), 32),
        name="hyena_filter",
    )(feat, tv, w1p, fb1.reshape(1, FH).astype(F32), fw2.astype(F32), fb2.reshape(1, FH).astype(F32),
      fw3.astype(F32), fb3.reshape(1, FH).astype(F32), ffreq.reshape(1, FH).astype(F32), fw4.astype(F32),
      jnp.asarray(deltas))


def _hilo(m):
    m = np.asarray(m, np.float64)
    hi = jnp.asarray(m, dtype=F32).astype(BF16)
    lo = (jnp.asarray(m, dtype=F32) - hi.astype(F32)).astype(BF16)
    return hi, lo


def _dft_tables(n1, n2, k1):
    n = n1 * n2
    a1 = 2.0 * np.pi * np.outer(np.arange(n1), np.arange(n1)) / n1
    f1 = np.concatenate([np.cos(a1)[:, :k1], -np.sin(a1)[:, :k1]], axis=0)
    a2 = 2.0 * np.pi * np.outer(np.arange(n2), np.arange(n2)) / n2
    c2, s2 = np.cos(a2), np.sin(a2)
    f2 = np.block([[c2, s2], [-s2, c2]])
    g2 = np.block([[c2, -s2], [s2, c2]])
    g1 = np.concatenate([np.cos(a1)[: n1 // 2], -np.sin(a1)[: n1 // 2]], axis=1)
    at = 2.0 * np.pi * np.outer(np.arange(n2), np.arange(n1)) / n
    tw1 = (np.cos(at)[:, :, None].astype(np.float32), (-np.sin(at))[:, :, None].astype(np.float32))
    tw2 = (np.cos(at.T)[:, :, None].astype(np.float32), np.sin(at.T)[:, :, None].astype(np.float32))
    return f1, f2, g2, g1, tw1, tw2


def _fft1_kernel(x_ref, fh_ref, fl_ref, twr_ref, twi_ref, ar_ref, ai_ref, *, n1):
    fh = fh_ref[...]
    fl = fl_ref[...]
    for i in range(SUBLANES):
        r = _dot3c(fh, fl, x_ref[:, i, :])
        re, im = r[:n1], r[n1:]
        twr = twr_ref[i]
        twi = twi_ref[i]
        ar_ref[:, i, :] = re * twr - im * twi
        ai_ref[:, i, :] = re * twi + im * twr


def _fft_stage1(x, k1, n1, n2, D, f1, tw1):
    B = x.shape[0]
    x = x.reshape(B, k1, n2 // SUBLANES, SUBLANES, D)
    fh, fl = _hilo(f1)
    dt = _tile(D, 512)
    const = lambda shape: pl.BlockSpec(shape, lambda b, s, c: (0,) * len(shape))
    out_spec = pl.BlockSpec((None, n1, SUBLANES, dt), lambda b, s, c: (b, 0, s, c))
    tw_spec = pl.BlockSpec((SUBLANES, n1, 1), lambda b, s, c: (s, 0, 0))
    return pl.pallas_call(
        functools.partial(_fft1_kernel, n1=n1),
        grid=(B, n2 // SUBLANES, D // dt),
        in_specs=[
            pl.BlockSpec((None, k1, None, SUBLANES, dt), lambda b, s, c: (b, 0, s, 0, c)),
            const((2 * n1, k1)), const((2 * n1, k1)), tw_spec, tw_spec,
        ],
        out_specs=[out_spec, out_spec],
        out_shape=[jax.ShapeDtypeStruct((B, n1, n2, D), F32)] * 2,
        compiler_params=_params(("parallel", "parallel", "parallel"), 40),
        name="fft_stage1",
    )(x, fh, fl, jnp.asarray(tw1[0]), jnp.asarray(tw1[1]))


def _fft2_kernel(*refs, n2, cc, scale, spectrum_only):
    if spectrum_only:
        ar_ref, ai_ref, fh_ref, fl_ref, yr_ref, yi_ref = refs
    else:
        ar_ref, ai_ref, kr_ref, ki_ref, fh_ref, fl_ref, gh_ref, gl_ref, twr_ref, twi_ref, yr_ref, yi_ref = refs
        twr = twr_ref[...]
        twi = twi_ref[...]
    fh = fh_ref[...]
    fl = fl_ref[...]
    for c in range(ar_ref.shape[1] // cc):
        cols = slice(c * cc, (c + 1) * cc)
        y = _dot3c(fh, fl, jnp.concatenate([ar_ref[:, cols], ai_ref[:, cols]], axis=0))
        yr, yi = y[:n2], y[n2:]
        if spectrum_only:
            yr_ref[:, cols] = yr * scale
            yi_ref[:, cols] = yi * scale
        else:
            kr = kr_ref[:, cols]
            ki = ki_ref[:, cols]
            pr = yr * kr - yi * ki
            pi = yr * ki + yi * kr
            q = _dot3c(gh_ref[...], gl_ref[...], jnp.concatenate([pr, pi], axis=0))
            qr, qi = q[:n2], q[n2:]
            yr_ref[:, cols] = qr * twr - qi * twi
            yi_ref[:, cols] = qr * twi + qi * twr


def _fft_stage2(ar, ai, n1, n2, D, f2, *, kf=None, g2=None, tw2=None, scale=1.0):
    B = ar.shape[0]
    fh, fl = _hilo(f2)
    cc = min(D, 512)
    blk = pl.BlockSpec((None, None, n2, D), lambda f, b: (b, f, 0, 0))
    const = lambda shape: pl.BlockSpec(shape, lambda f, b: (0,) * len(shape))
    specs = [blk, blk]
    args = [ar, ai]
    if kf is not None:
        gh, gl = _hilo(g2)
        kblk = pl.BlockSpec((None, n2, D), lambda f, b: (f, 0, 0))
        tblk = pl.BlockSpec((None, n2, 1), lambda f, b: (f, 0, 0))
        specs += [kblk, kblk, const((2 * n2, 2 * n2)), const((2 * n2, 2 * n2)), const((2 * n2, 2 * n2)),
                  const((2 * n2, 2 * n2)), tblk, tblk]
        args += [kf[0], kf[1], fh, fl, gh, gl, jnp.asarray(tw2[0]), jnp.asarray(tw2[1])]
    else:
        specs += [const((2 * n2, 2 * n2)), const((2 * n2, 2 * n2))]
        args += [fh, fl]
    return pl.pallas_call(
        functools.partial(_fft2_kernel, n2=n2, cc=cc, scale=scale, spectrum_only=kf is None),
        grid=(n1, B),
        in_specs=specs,
        out_specs=[blk, blk],
        out_shape=[jax.ShapeDtypeStruct((B, n1, n2, D), F32)] * 2,
        compiler_params=_params(("parallel", "parallel"), 40),
        name="fft_stage2",
    )(*args)


def _fft3_kernel(qr_ref, qi_ref, gh_ref, gl_ref, z_ref, x0_ref, skip_ref, o_ref):
    gh = gh_ref[...]
    gl = gl_ref[...]
    skip = skip_ref[...]
    for i in range(SUBLANES):
        y = _dot3c(gh, gl, jnp.concatenate([qr_ref[:, i, :], qi_ref[:, i, :]], axis=0))
        o_ref[:, i, :] = (y + z_ref[:, i, :] * skip) * x0_ref[:, i, :]


def _fft_stage3(qr, qi, z, x0, skip, n1, n2, D, g1):
    B, L, _ = z.shape
    split = lambda t: t.reshape(B, n1 // 2, n2 // SUBLANES, SUBLANES, D)
    gh, gl = _hilo(g1)
    dt = _tile(D, 512)
    qblk = pl.BlockSpec((None, n1, SUBLANES, dt), lambda b, s, c: (b, 0, s, c))
    zblk = pl.BlockSpec((None, n1 // 2, None, SUBLANES, dt), lambda b, s, c: (b, 0, s, 0, c))
    const = lambda shape: pl.BlockSpec(shape, lambda b, s, c: (0,) * len(shape))
    out = pl.pallas_call(
        _fft3_kernel,
        grid=(B, n2 // SUBLANES, D // dt),
        in_specs=[qblk, qblk, const((n1 // 2, 2 * n1)), const((n1 // 2, 2 * n1)), zblk, zblk,
                  pl.BlockSpec((1, dt), lambda b, s, c: (0, c))],
        out_specs=zblk,
        out_shape=jax.ShapeDtypeStruct((B, n1 // 2, n2 // SUBLANES, SUBLANES, D), F32),
        compiler_params=_params(("parallel", "parallel", "parallel"), 40),
        name="fft_stage3",
    )(qr, qi, gh, gl, split(z), split(x0), skip.reshape(1, D).astype(F32))
    return out.reshape(B, L, D)


def long_conv_gate(z, x0, skip, kern):
    B, L, D = z.shape
    n2 = FFT_N2
    n1 = 2 * L // n2
    assert n1 * n2 == 2 * L and n1 % 2 == 0
    f1_full, f2, g2, g1, tw1, tw2 = _dft_tables(n1, n2, n1)
    ka = _fft_stage1(kern.reshape(1, 2 * L, D), n1, n1, n2, D, f1_full, tw1)
    kf = _fft_stage2(ka[0], ka[1], n1, n2, D, f2, scale=1.0 / (n1 * n2))
    kf = (kf[0].reshape(n1, n2, D), kf[1].reshape(n1, n2, D))
    a = _fft_stage1(z, n1 // 2, n1, n2, D, f1_full[:, : n1 // 2], tw1)
    q = _fft_stage2(a[0], a[1], n1, n2, D, f2, kf=kf, g2=g2, tw2=tw2)
    return _fft_stage3(q[0], q[1], z, x0, skip, n1, n2, D, g1)


def _rope_tables(S):
    inv = ROPE_THETA ** (-jnp.arange(0, HEAD_DIM, 2, dtype=F32) / HEAD_DIM)
    ang = jnp.arange(S, dtype=F32)[:, None] * inv[None, :]
    cos, sin = jnp.cos(ang), jnp.sin(ang)
    return jnp.concatenate([cos, cos], axis=-1), jnp.concatenate([-sin, sin], axis=-1)


def _trunk(x, p):
    B, S, D = x.shape
    H = D // HEAD_DIM
    G = len(GROUP_DILATIONS)
    T = B * S
    cos, sin = _rope_tables(S)
    x = x.reshape(T, D)
    depth = p["mix_norm"].shape[0]
    for i in range(depth):
        j = i // 2
        if i % 2 == 0:
            scale = HEAD_DIM ** -0.5
            gains = jnp.stack([p["attn_q_gain"][j] * scale, p["attn_k_gain"][j], jnp.ones_like(p["attn_k_gain"][j])],
                              axis=1).reshape(G * 3, 1, HEAD_DIM).astype(F32)
            qkv = qkv_project(x, p["mix_norm"][i], p["attn_w_qkv"][j], gains, cos, sin, S)
            o = window_attention(qkv, B, S, H)
            x = matmul_resid(o, p["attn_w_out"][j], x)
        else:
            u = norm_matmul(x, p["mix_norm"][i], p["hy_w_in"][j], p["hy_b_in"][j])
            z, x0 = short_conv_gate(u.reshape(B, S, 3 * D), p["hy_conv_w"][j], p["hy_conv_b"][j])
            kern = conv_kernel_signal(S, D, p["hy_filt_w1"][j], p["hy_filt_b1"][j], p["hy_filt_w2"][j],
                                      p["hy_filt_b2"][j], p["hy_filt_w3"][j], p["hy_filt_b3"][j],
                                      p["hy_filt_freq"][j], p["hy_filt_w4"][j])
            y = long_conv_gate(z, x0, p["hy_skip"][j], kern)
            x = matmul_resid(y.reshape(T, D), p["hy_w_out"][j], x, p["hy_b_out"][j])
        h = norm_matmul(x, p["mlp_norm"][i], p["mlp_w1"][i], relu2=True, out_dtype=BF16)
        x = matmul_resid(h, p["mlp_w2"][i], x)
    return x.reshape(B, S, D)


def kernel(x_prompt, x_sample, mix_norm, mlp_norm, attn_w_qkv, attn_q_gain, attn_k_gain, attn_w_out, hy_w_in, hy_b_in, hy_conv_w, hy_conv_b, hy_filt_w1, hy_filt_b1, hy_filt_w2, hy_filt_b2, hy_filt_w3, hy_filt_b3, hy_filt_freq, hy_filt_w4, hy_skip, hy_w_out, hy_b_out, mlp_w1, mlp_w2):
    p = dict(
        mix_norm=mix_norm, mlp_norm=mlp_norm,
        attn_w_qkv=attn_w_qkv.astype(BF16), attn_q_gain=attn_q_gain, attn_k_gain=attn_k_gain,
        attn_w_out=attn_w_out.astype(BF16),
        hy_w_in=hy_w_in.astype(BF16), hy_b_in=hy_b_in, hy_conv_w=hy_conv_w, hy_conv_b=hy_conv_b,
        hy_filt_w1=hy_filt_w1, hy_filt_b1=hy_filt_b1, hy_filt_w2=hy_filt_w2, hy_filt_b2=hy_filt_b2,
        hy_filt_w3=hy_filt_w3, hy_filt_b3=hy_filt_b3, hy_filt_freq=hy_filt_freq, hy_filt_w4=hy_filt_w4,
        hy_skip=hy_skip, hy_w_out=hy_w_out.astype(BF16), hy_b_out=hy_b_out,
        mlp_w1=mlp_w1.astype(BF16), mlp_w2=mlp_w2.astype(BF16),
    )
    return (_trunk(x_prompt, p), _trunk(x_sample, p))
```

```python
import functools
import math

import numpy as np
import jax
import jax.numpy as jnp
from jax import lax
from jax.experimental import pallas as pl
from jax.experimental.pallas import tpu as pltpu

F32 = jnp.float32
BF16 = jnp.bfloat16

HEAD_DIM = 128
GROUP_WINDOWS = (128, 512, 2048)
GROUP_DILATIONS = (1, 4, 16)
ROPE_THETA = 10000.0
FILTER_EMB = 33
FILTER_BANDS = (FILTER_EMB - 1) // 2
DECAY_TARGET = 1e-2
FAST_DECAY_PCT = 0.3
SLOW_DECAY_PCT = 1.5
EPS = 1e-6

LANES = 128
SUBLANES = 8
ATTN_TILE = 1024
ATTN_BATCH = 8
FFT_N2 = 128
FFT_J = FFT_N2 // SUBLANES
MIB = 1024 * 1024


def _params(semantics, vmem_mib):
    return pltpu.CompilerParams(dimension_semantics=semantics, vmem_limit_bytes=vmem_mib * MIB)


def _tile(n, pref, quantum=LANES):
    if n <= pref:
        return n
    t = (pref // quantum) * quantum
    while t > quantum and n % t:
        t -= quantum
    assert n % t == 0, (n, pref)
    return t


def _split(x):
    hi = x.astype(BF16)
    lo = (x - hi.astype(F32)).astype(BF16)
    return hi, lo


def _dot(a, b):
    return jnp.dot(a, b, preferred_element_type=F32)


def _dot3(a, b):
    ah, al = _split(a)
    bh, bl = _split(b)
    return _dot(ah, bh) + (_dot(ah, bl) + _dot(al, bh))


def _dot3c(ch, cl, b):
    bh, bl = _split(b)
    return _dot(ch, bh) + (_dot(ch, bl) + _dot(cl, bh))


def _rms_rows_to(x_ref, g_ref, xn_ref, row_chunk):
    def body(c, carry):
        rows = pl.ds(pl.multiple_of(c * row_chunk, row_chunk), row_chunk)
        x = x_ref[rows, :]
        inv = lax.rsqrt(jnp.mean(x * x, axis=-1, keepdims=True) + EPS)
        xn_ref[rows, :] = (x * inv * g_ref[...]).astype(BF16)
        return carry

    lax.fori_loop(0, x_ref.shape[0] // row_chunk, body, 0)


def _store_grouped(o_ref, v):
    for a in range(o_ref.shape[1]):
        o_ref[:, a] = v[a * FFT_N2:(a + 1) * FFT_N2].reshape(FFT_J, SUBLANES, v.shape[1])


def _load_grouped(a_ref):
    c = a_ref.shape[-1]
    return jnp.concatenate([a_ref[:, s].reshape(FFT_N2, c) for s in range(a_ref.shape[1])], axis=0)


def _norm_mm_kernel(x_ref, g_ref, w_ref, *rest, has_bias, relu2, row_chunk):
    if has_bias:
        b_ref, o_ref, xn_ref = rest
    else:
        o_ref, xn_ref = rest

    @pl.when(pl.program_id(1) == 0)
    def _():
        _rms_rows_to(x_ref, g_ref, xn_ref, row_chunk)

    acc = _dot(xn_ref[...], w_ref[...])
    if has_bias:
        acc = acc + b_ref[...]
    if relu2:
        acc = jnp.maximum(acc, 0.0)
        acc = acc * acc
    o_ref[...] = acc.astype(o_ref.dtype)


def norm_matmul(x, g, w, bias=None, *, relu2=False, out_dtype=F32, tm=1024, tn=1024):
    T, K = x.shape
    N = w.shape[1]
    tm = _tile(T, tm)
    tn = _tile(N, tn)
    in_specs = [
        pl.BlockSpec((tm, K), lambda i, j: (i, 0)),
        pl.BlockSpec((1, K), lambda i, j: (0, 0)),
        pl.BlockSpec((K, tn), lambda i, j: (0, j)),
    ]
    args = [x, g.reshape(1, K), w]
    if bias is not None:
        in_specs.append(pl.BlockSpec((1, tn), lambda i, j: (0, j)))
        args.append(bias.reshape(1, N))
    return pl.pallas_call(
        functools.partial(_norm_mm_kernel, has_bias=bias is not None, relu2=relu2, row_chunk=min(tm, 128)),
        grid=(T // tm, N // tn),
        in_specs=in_specs,
        out_specs=pl.BlockSpec((tm, tn), lambda i, j: (i, j)),
        out_shape=jax.ShapeDtypeStruct((T, N), out_dtype),
        scratch_shapes=[pltpu.VMEM((tm, K), BF16)],
        compiler_params=_params(("parallel", "arbitrary"), 48),
        name="norm_matmul",
    )(*args)


def _qkv_kernel(x_ref, g_ref, w_ref, gain_ref, cos_ref, sin_ref, o_ref, xn_ref, stage_ref, *, row_chunk,
                tiles_per_part, dils):
    j = pl.program_id(1)

    @pl.when(j == 0)
    def _():
        _rms_rows_to(x_ref, g_ref, xn_ref, row_chunk)

    acc = _dot(xn_ref[...], w_ref[...])
    gc = j // tiles_per_part
    group = gc // 3
    part = gc % 3
    tm = acc.shape[0]
    n_slab = acc.shape[1] // HEAD_DIM

    def store(s, y, d):
        if d == 1:
            o_ref[s] = y.astype(o_ref.dtype)
            return
        stage_ref[s] = y
        n_r = tm // d
        for r in range(d):
            o_ref[s, r * n_r:(r + 1) * n_r, :] = stage_ref[s, pl.ds(r, n_r, stride=d), :].astype(o_ref.dtype)

    for g, d in enumerate(dils):

        @pl.when((group == g) & (part == 2))
        def _(d=d):
            for s in range(n_slab):
                store(s, acc[:, s * HEAD_DIM:(s + 1) * HEAD_DIM], d)

        @pl.when((group == g) & (part != 2))
        def _(d=d):
            gain = gain_ref[...]
            cos = cos_ref[...]
            sin = sin_ref[...]
            for s in range(n_slab):
                a = acc[:, s * HEAD_DIM:(s + 1) * HEAD_DIM]
                inv = lax.rsqrt(jnp.mean(a * a, axis=-1, keepdims=True) + EPS)
                y = a * inv * gain
                store(s, y * cos + pltpu.roll(y, HEAD_DIM // 2, axis=1) * sin, d)


def qkv_project(x, g, w, gains, cos, sin, seq_len, *, tn=1024):
    T, K = x.shape
    N = w.shape[1]
    part_width = N // (len(GROUP_DILATIONS) * 3)
    tm = ATTN_TILE
    assert seq_len % tm == 0
    tn = _tile(part_width, tn)
    pos_tiles = seq_len // tm
    tiles_per_part = part_width // tn

    def out_map(i, j):
        gc = j // tiles_per_part
        return ((gc // 3) * tiles_per_part + j % tiles_per_part, (gc % 3 + 2) % 3, i, 0)

    return pl.pallas_call(
        functools.partial(_qkv_kernel, row_chunk=min(tm, 128), tiles_per_part=tiles_per_part,
                          dils=GROUP_DILATIONS),
        grid=(T // tm, N // tn),
        in_specs=[
            pl.BlockSpec((tm, K), lambda i, j: (i, 0)),
            pl.BlockSpec((1, K), lambda i, j: (0, 0)),
            pl.BlockSpec((K, tn), lambda i, j: (0, j)),
            pl.BlockSpec((None, 1, HEAD_DIM), lambda i, j: (j // tiles_per_part, 0, 0)),
            pl.BlockSpec((tm, HEAD_DIM), lambda i, j: (i % pos_tiles, 0)),
            pl.BlockSpec((tm, HEAD_DIM), lambda i, j: (i % pos_tiles, 0)),
        ],
        out_specs=pl.BlockSpec((tn // HEAD_DIM, None, tm, HEAD_DIM), out_map),
        out_shape=jax.ShapeDtypeStruct((N // HEAD_DIM // 3, 3, T, HEAD_DIM), BF16),
        scratch_shapes=[pltpu.VMEM((tm, K), BF16), pltpu.VMEM((tn // HEAD_DIM, tm, HEAD_DIM), F32)],
        compiler_params=_params(("parallel", "arbitrary"), 48),
        name="qkv_project",
    )(x, g.reshape(1, K), w, gains, cos, sin)


def _mm_resid_kernel(a_ref, w_ref, *rest, has_bias, grouped):
    if has_bias:
        b_ref, r_ref, o_ref = rest
    else:
        r_ref, o_ref = rest
    k = pl.program_id(2)
    a = _load_grouped(a_ref) if grouped else a_ref[...]
    part = _dot(a.astype(BF16), w_ref[...])

    @pl.when(k == 0)
    def _():
        first = part + r_ref[...]
        if has_bias:
            first = first + b_ref[...]
        o_ref[...] = first

    @pl.when(k > 0)
    def _():
        o_ref[...] += part


def matmul_resid(a, w, resid, bias=None, *, tm=1024, tn=1024, tk=2048):
    grouped = a.ndim == 5
    T, N = resid.shape
    K = w.shape[0]
    tn = _tile(N, tn)
    tk = _tile(K, tk)
    if grouped:
        seq_len = a.shape[2] * FFT_N2
        tm = _tile(seq_len, tm)
        seq_tiles = seq_len // tm
        a_spec = pl.BlockSpec((None, FFT_J, tm // FFT_N2, SUBLANES, tk),
                              lambda i, j, k: (i // seq_tiles, 0, i % seq_tiles, 0, k))
    else:
        tm = _tile(T, tm)
        a_spec = pl.BlockSpec((tm, tk), lambda i, j, k: (i, k))
    in_specs = [
        a_spec,
        pl.BlockSpec((tk, tn), lambda i, j, k: (k, j)),
    ]
    args = [a, w]
    if bias is not None:
        in_specs.append(pl.BlockSpec((1, tn), lambda i, j, k: (0, j)))
        args.append(bias.reshape(1, N))
    in_specs.append(pl.BlockSpec((tm, tn), lambda i, j, k: (i, j)))
    args.append(resid)
    return pl.pallas_call(
        functools.partial(_mm_resid_kernel, has_bias=bias is not None, grouped=grouped),
        grid=(T // tm, N // tn, K // tk),
        in_specs=in_specs,
        out_specs=pl.BlockSpec((tm, tn), lambda i, j, k: (i, j)),
        out_shape=jax.ShapeDtypeStruct((T, N), F32),
        compiler_params=_params(("parallel", "parallel", "arbitrary"), 48),
        name="matmul_resid",
    )(*args)


def _attn_kernel(*refs, tq, dils, half, n_tiles):
    G = len(dils)
    cur_refs, prev_refs, next_refs = refs[0:G], refs[G:2 * G], refs[2 * G:3 * G]
    o_ref = refs[3 * G]
    acc_ref, m_ref, l_ref, out_ref, bias_ref = refs[3 * G + 1:]
    i = pl.program_id(2)
    has_prev = i > 0
    has_next = i < n_tiles - 1

    for g, d in enumerate(dils):
        n_r = tq // d
        qs = min(128, n_r)
        ks = qs + 2 * half
        n_sub = n_r // qs
        qq = lax.broadcasted_iota(jnp.int32, (qs, ks), 0)
        kk = lax.broadcasted_iota(jnp.int32, (qs, ks), 1)
        band = (kk >= qq) & (kk <= qq + 2 * half)
        prev_ok = (kk >= half) | has_prev
        next_ok = (kk < ks - half) | has_next
        for idx, valid in enumerate((band & prev_ok, band, band & next_ok, band & prev_ok & next_ok)):
            bias_ref[idx, 0:qs, 0:ks] = jnp.where(valid, 0.0, -jnp.inf)

        def window(c, r, u, g=g, n_r=n_r, qs=qs):
            lo, hi = u * qs - half, (u + 1) * qs + half
            pieces = [prev_refs[g][c, r]] if lo < 0 else []
            pieces.append(cur_refs[g][c, r, max(lo, 0):min(hi, n_r), :])
            if hi > n_r:
                pieces.append(next_refs[g][c, r])
            return pieces[0] if len(pieces) == 1 else jnp.concatenate(pieces, axis=0)

        blocks = [(r, u) for r in range(d) for u in range(n_sub)]
        for b0 in range(0, len(blocks), ATTN_BATCH):
            batch = blocks[b0:b0 + ATTN_BATCH]
            edge = lambda u: (3 if n_sub == 1 else 0) if u == 0 else (2 if u == n_sub - 1 else 1)
            q = jnp.stack([cur_refs[g][2, r, u * qs:(u + 1) * qs, :] for r, u in batch])
            kw = jnp.stack([window(0, r, u) for r, u in batch])
            vw = jnp.stack([window(1, r, u) for r, u in batch])
            bias = jnp.stack([bias_ref[edge(u), 0:qs, 0:ks] for _, u in batch])
            s = jnp.einsum("bqd,bkd->bqk", q, kw, preferred_element_type=F32) + bias
            m_blk = jnp.max(s, axis=-1, keepdims=True)
            p = jnp.exp(s - m_blk)
            l_blk = jnp.sum(p, axis=-1, keepdims=True)
            pv = jnp.einsum("bqk,bkd->bqd", p.astype(BF16), vw, preferred_element_type=F32)
            for b, (r, u) in enumerate(batch):
                rows = pl.ds(r * n_r + u * qs, qs)
                m_ref[g, rows, :] = m_blk[b]
                l_ref[g, rows, :] = l_blk[b]
                acc_ref[g, rows, :] = pv[b]

    dmax = max(dils)
    n_max = tq // dmax
    for r in range(dmax):
        sel = [pl.ds((r % d) * (tq // d) + r // d, n_max, stride=dmax // d) if d < dmax else pl.ds(r * n_max, n_max)
               for d in dils]
        ms = [m_ref[g, sel[g], :] for g in range(G)]
        m = functools.reduce(jnp.maximum, ms)
        ws = [jnp.exp(mg - m) for mg in ms]
        num = sum(ws[g] * acc_ref[g, sel[g], :] for g in range(G))
        den = sum(ws[g] * l_ref[g, sel[g], :] for g in range(G))
        out_ref[r * n_max:(r + 1) * n_max, :] = num / den
    o_ref[...] = jnp.concatenate([out_ref[pl.ds(j, dmax, stride=n_max), :] for j in range(n_max)],
                                 axis=0).astype(o_ref.dtype)


def window_attention(qkv, batch, seq_len, n_heads):
    G = len(GROUP_DILATIONS)
    tq = ATTN_TILE
    assert seq_len % tq == 0
    n_tiles = seq_len // tq
    total_tiles = batch * n_tiles
    half = GROUP_WINDOWS[0] // (2 * GROUP_DILATIONS[0])
    for wdw, d in zip(GROUP_WINDOWS, GROUP_DILATIONS):
        assert wdw // (2 * d) == half and half * d <= tq and tq % (d * min(128, tq // d)) == 0
    H = n_heads
    dils = GROUP_DILATIONS

    cur, prev, nxt = [], [], []
    for g, d in enumerate(dils):
        cur.append(pl.BlockSpec((None, 3, None, d, tq // d, HEAD_DIM),
                                lambda b, h, i, g=g: (g * H + h, 0, b * n_tiles + i, 0, 0, 0)))
        prev.append(pl.BlockSpec((None, 2, None, d, half, HEAD_DIM),
                                 lambda b, h, i, g=g, d=d: (g * H + h, 0, jnp.maximum(b * n_tiles + i - 1, 0), 0,
                                                            tq // d // half - 1, 0)))
        nxt.append(pl.BlockSpec((None, 2, None, d, half, HEAD_DIM),
                                lambda b, h, i, g=g: (g * H + h, 0, jnp.minimum(b * n_tiles + i + 1, total_tiles - 1),
                                                      0, 0, 0)))
    views = [qkv.reshape(G * H, 3, total_tiles, d, tq // d, HEAD_DIM) for d in dils]
    in_specs = cur + prev + nxt
    return pl.pallas_call(
        functools.partial(_attn_kernel, tq=tq, dils=dils, half=half, n_tiles=n_tiles),
        grid=(batch, H, n_tiles),
        in_specs=in_specs,
        out_specs=pl.BlockSpec((tq, HEAD_DIM), lambda b, h, i: (b * n_tiles + i, h)),
        out_shape=jax.ShapeDtypeStruct((batch * seq_len, H * HEAD_DIM), BF16),
        scratch_shapes=[
            pltpu.VMEM((G, tq, HEAD_DIM), F32),
            pltpu.VMEM((G, tq, 1), F32),
            pltpu.VMEM((G, tq, 1), F32),
            pltpu.VMEM((tq, HEAD_DIM), F32),
            pltpu.VMEM((4, 128, 128 + 2 * half), F32),
        ],
        compiler_params=_params(("parallel", "parallel", "parallel"), 40),
        name="window_attention",
    )(*(views * 3))


def _sconv_kernel(*refs, n_tiles):
    (x0m, x0p, x0n, x1m, x1p, x1n, vm, vp, vn, w0, w1, w2, b0, b1, b2, z_ref, x0_ref) = refs
    i = pl.program_id(1)

    def conv(m_ref, p_ref, n_ref, w_ref, b_ref):
        x = m_ref[...]
        ts = x.shape[0]
        prev_row = jnp.where(i > 0, p_ref[7:8, :], 0.0)
        next_row = jnp.where(i < n_tiles - 1, n_ref[0:1, :], 0.0)
        row = lax.broadcasted_iota(jnp.int32, x.shape, 0)
        up = jnp.where(row == 0, prev_row, pltpu.roll(x, 1, axis=0))
        dn = jnp.where(row == ts - 1, next_row, pltpu.roll(x, ts - 1, axis=0))
        return up * w_ref[0:1, :] + x * w_ref[1:2, :] + dn * w_ref[2:3, :] + b_ref[...]

    _store_grouped(x0_ref, conv(x0m, x0p, x0n, w0, b0))
    _store_grouped(z_ref, conv(vm, vp, vn, w2, b2) * conv(x1m, x1p, x1n, w1, b1))


def short_conv_gate(u, conv_w, conv_b, *, ts=512, ct=512):
    B, L, D3 = u.shape
    D = D3 // 3
    ts = _tile(L, ts, FFT_N2)
    ct = _tile(D, ct)
    n_tiles = L // ts
    n_ct = D // ct
    rows8 = ts // 8
    last8 = L // 8 - 1
    specs, args = [], []
    for part in range(3):
        off = part * n_ct
        specs += [
            pl.BlockSpec((None, ts, ct), lambda b, i, j, off=off: (b, i, off + j)),
            pl.BlockSpec((None, 8, ct), lambda b, i, j, off=off: (b, jnp.maximum(i * rows8 - 1, 0), off + j)),
            pl.BlockSpec((None, 8, ct), lambda b, i, j, off=off: (b, jnp.minimum((i + 1) * rows8, last8), off + j)),
        ]
        args += [u, u, u]
    for part in range(3):
        specs.append(pl.BlockSpec((3, ct), lambda b, i, j, off=part * n_ct: (0, off + j)))
        args.append(conv_w)
    for part in range(3):
        specs.append(pl.BlockSpec((1, ct), lambda b, i, j, off=part * n_ct: (0, off + j)))
        args.append(conv_b.reshape(1, D3))
    out_spec = pl.BlockSpec((None, FFT_J, ts // FFT_N2, SUBLANES, ct), lambda b, i, j: (b, 0, i, 0, j))
    return pl.pallas_call(
        functools.partial(_sconv_kernel, n_tiles=n_tiles),
        grid=(B, n_tiles, n_ct),
        in_specs=specs,
        out_specs=[out_spec, out_spec],
        out_shape=[jax.ShapeDtypeStruct((B, FFT_J, L // FFT_N2, SUBLANES, D), F32)] * 2,
        compiler_params=_params(("parallel", "parallel", "parallel"), 32),
        name="short_conv_gate",
    )(*args)


def _filter_kernel(feat_ref, tv_ref, w1, b1, w2, b2, w3, b3, fr_ref, w4, delta_ref, o_ref):
    fr = fr_ref[...]
    h = jnp.sin(fr * (_dot3(feat_ref[...], w1[...]) + b1[...]))
    h = jnp.sin(fr * (_dot3(h, w2[...]) + b2[...]))
    h = jnp.sin(fr * (_dot3(h, w3[...]) + b3[...]))
    out = _dot3(h, w4[...])
    t = tv_ref[:, 0:1]
    valid = tv_ref[:, 1:2]
    _store_grouped(o_ref, out * jnp.exp(-t * delta_ref[...]) * valid)


def conv_kernel_signal(L, D, fw1, fb1, fw2, fb2, fw3, fb3, ffreq, fw4, *, tl=256):
    FH = fw2.shape[0]
    FE = 64
    p = jnp.arange(2 * L, dtype=jnp.int32)
    pos = jnp.where(p < L, p, 2 * L - p).astype(F32)
    t = pos / (L - 1)
    bands = jnp.linspace(1e-4, FILTER_BANDS - 1, FILTER_BANDS, dtype=F32)
    ang = (2.0 * math.pi / L) * pos[:, None] * bands[None, :]
    feat = jnp.concatenate(
        [t[:, None], jnp.cos(ang), -jnp.sin(ang), jnp.zeros((2 * L, FE - FILTER_EMB), F32)], axis=-1)
    tv = jnp.stack([t, (p != L).astype(F32)], axis=-1)
    deltas = np.abs(np.linspace(math.log(DECAY_TARGET) / SLOW_DECAY_PCT, math.log(DECAY_TARGET) / FAST_DECAY_PCT, D,
                                dtype=np.float32)).reshape(1, D)
    w1p = jnp.pad(fw1.astype(F32), ((0, FE - FILTER_EMB), (0, 0)))
    tl = _tile(L, tl, FFT_N2)
    half_tiles = L // tl
    full = lambda shape: pl.BlockSpec(shape, lambda i: (0, 0))
    return pl.pallas_call(
        _filter_kernel,
        grid=(2 * L // tl,),
        in_specs=[
            pl.BlockSpec((tl, FE), lambda i: (i, 0)),
            pl.BlockSpec((tl, 2), lambda i: (i, 0)),
            full((FE, FH)), full((1, FH)), full((FH, FH)), full((1, FH)), full((FH, FH)), full((1, FH)),
            full((1, FH)),
            pl.BlockSpec((FH, D), lambda i: (0, i // half_tiles)),
            full((1, D)),
        ],
        out_specs=pl.BlockSpec((FFT_J, tl // FFT_N2, SUBLANES, D), lambda i: (0, i, 0, 0)),
        out_shape=jax.ShapeDtypeStruct((FFT_J, 2 * L // FFT_N2, SUBLANES, D), F32),
        compiler_params=_params(("parallel",), 32),
        name="hyena_filter",
    )(feat, tv, w1p, fb1.reshape(1, FH).astype(F32), fw2.astype(F32), fb2.reshape(1, FH).astype(F32),
      fw3.astype(F32), fb3.reshape(1, FH).astype(F32), ffreq.reshape(1, FH).astype(F32), fw4.astype(F32),
      jnp.asarray(deltas))


def _hilo(m):
    m = jnp.asarray(np.asarray(m, np.float64), dtype=F32)
    return _split(m)


def _stack(re, im):
    return np.block([[re, -im], [im, re]])


def _dft_tables(n1, n2):
    n, k1 = n1 * n2, n1 // 2
    a1 = 2.0 * np.pi * np.outer(np.arange(n1), np.arange(n1)) / n1
    c1, s1 = np.cos(a1), np.sin(a1)
    a2 = 2.0 * np.pi * np.outer(np.arange(n2), np.arange(n2)) / n2
    c2, s2 = np.cos(a2), np.sin(a2)
    at = 2.0 * np.pi * np.outer(np.arange(n2), np.arange(n1)) / n
    bcast = lambda t: jnp.broadcast_to(jnp.asarray(t, F32)[:, :, None], t.shape + (LANES,))
    return dict(
        f1_real=np.concatenate([c1, -s1], axis=0),
        f1_cplx=_stack(c1[:, :k1], -s1[:, :k1]),
        f2=_stack(c2, -s2),
        g2=_stack(c2, s2),
        g1_cplx=_stack(c1[:k1], s1[:k1]),
        tw1=(bcast(np.cos(at)), bcast(-np.sin(at))),
        tw2=(bcast(np.cos(at.T)), bcast(np.sin(at.T))),
    )


def _fft1_kernel(x_ref, fh_ref, fl_ref, twr_ref, twi_ref, ar_ref, ai_ref, *, n1):
    fh = fh_ref[...]
    fl = fl_ref[...]
    parts, rows = x_ref.shape[0], x_ref.shape[1] // SUBLANES
    for i in range(SUBLANES):
        x = jnp.concatenate([x_ref[p, pl.ds(i, rows, stride=SUBLANES), :] for p in range(parts)], axis=0)
        r = _dot3c(fh, fl, x)
        re, im = r[:n1], r[n1:]
        twr = twr_ref[i]
        twi = twi_ref[i]
        ar_ref[pl.ds(i, n1, stride=SUBLANES), :] = re * twr - im * twi
        ai_ref[pl.ds(i, n1, stride=SUBLANES), :] = re * twi + im * twr


def _fft_stage1(x, n1, D, f1, tw1):
    P, parts, _, k8, _ = x.shape
    fh, fl = _hilo(f1)
    const = lambda shape: pl.BlockSpec(shape, lambda j, p, c: (0,) * len(shape))
    out_spec = pl.BlockSpec((None, None, n1 * SUBLANES, LANES), lambda j, p, c: (p, j, 0, c))
    tw_spec = pl.BlockSpec((SUBLANES, n1, LANES), lambda j, p, c: (j, 0, 0))
    return pl.pallas_call(
        functools.partial(_fft1_kernel, n1=n1),
        grid=(FFT_J, P, D // LANES),
        in_specs=[
            pl.BlockSpec((None, parts, None, k8, LANES), lambda j, p, c: (p, 0, j, 0, c)),
            const(f1.shape), const(f1.shape), tw_spec, tw_spec,
        ],
        out_specs=[out_spec, out_spec],
        out_shape=[jax.ShapeDtypeStruct((P, FFT_J, n1 * SUBLANES, D), F32)] * 2,
        compiler_params=_params(("parallel", "parallel", "parallel"), 32),
        name="fft_stage1",
    )(x, fh, fl, tw1[0], tw1[1])


def _fft2_kernel(*refs, n2, cc, scale, spectrum_only):
    if spectrum_only:
        ar_ref, ai_ref, fh_ref, fl_ref, yr_ref, yi_ref = refs
    else:
        ar_ref, ai_ref, kr_ref, ki_ref, fh_ref, fl_ref, gh_ref, gl_ref, twr_ref, twi_ref, yr_ref, yi_ref = refs
        twr = jnp.concatenate([twr_ref[...]] * (cc // LANES), axis=1)
        twi = jnp.concatenate([twi_ref[...]] * (cc // LANES), axis=1)
    fh = fh_ref[...]
    fl = fl_ref[...]
    for c in range(ar_ref.shape[-1] // cc):
        cols = slice(c * cc, (c + 1) * cc)
        a = jnp.concatenate([ar_ref[:, :, cols].reshape(n2, cc), ai_ref[:, :, cols].reshape(n2, cc)], axis=0)
        y = _dot3c(fh, fl, a)
        yr, yi = y[:n2], y[n2:]
        if spectrum_only:
            yr_ref[:, cols] = yr * scale
            yi_ref[:, cols] = yi * scale
        else:
            kr = kr_ref[:, cols]
            ki = ki_ref[:, cols]
            pr = yr * kr - yi * ki
            pi = yr * ki + yi * kr
            q = _dot3c(gh_ref[...], gl_ref[...], jnp.concatenate([pr, pi], axis=0))
            qr, qi = q[:n2], q[n2:]
            yr_ref[:, :, cols] = (qr * twr - qi * twi).reshape(FFT_J, SUBLANES, cc)
            yi_ref[:, :, cols] = (qr * twi + qi * twr).reshape(FFT_J, SUBLANES, cc)


def _fft_stage2(ar, ai, n1, n2, D, f2, *, kf=None, g2=None, tw2=None, scale=1.0):
    B = ar.shape[0]
    ar = ar.reshape(B, FFT_J, n1, SUBLANES, D)
    ai = ai.reshape(B, FFT_J, n1, SUBLANES, D)
    fh, fl = _hilo(f2)
    cc = _tile(D, 512)
    blk = pl.BlockSpec((None, FFT_J, None, SUBLANES, D), lambda f, b: (b, 0, f, 0, 0))
    const = lambda shape: pl.BlockSpec(shape, lambda f, b: (0,) * len(shape))
    specs = [blk, blk]
    args = [ar, ai]
    if kf is not None:
        gh, gl = _hilo(g2)
        kblk = pl.BlockSpec((None, n2, D), lambda f, b: (f, 0, 0))
        tblk = pl.BlockSpec((None, n2, LANES), lambda f, b: (f, 0, 0))
        specs += [kblk, kblk, const((2 * n2, 2 * n2)), const((2 * n2, 2 * n2)), const((2 * n2, 2 * n2)),
                  const((2 * n2, 2 * n2)), tblk, tblk]
        args += [kf[0], kf[1], fh, fl, gh, gl, tw2[0], tw2[1]]
        out_spec = blk
        out_shape = jax.ShapeDtypeStruct((B, FFT_J, n1, SUBLANES, D), F32)
    else:
        specs += [const((2 * n2, 2 * n2)), const((2 * n2, 2 * n2))]
        args += [fh, fl]
        out_spec = pl.BlockSpec((None, None, n2, D), lambda f, b: (b, f, 0, 0))
        out_shape = jax.ShapeDtypeStruct((B, n1, n2, D), F32)
    out = pl.pallas_call(
        functools.partial(_fft2_kernel, n2=n2, cc=cc, scale=scale, spectrum_only=kf is None),
        grid=(n1, B),
        in_specs=specs,
        out_specs=[out_spec, out_spec],
        out_shape=[out_shape] * 2,
        compiler_params=_params(("parallel", "parallel"), 40),
        name="fft_stage2",
    )(*args)
    if kf is not None:
        out = [o.reshape(B, FFT_J, n1 * SUBLANES, D) for o in out]
    return out


def _fft3_kernel(qr_ref, qi_ref, gh_ref, gl_ref, z_ref, x0_ref, skip_ref, o_ref, *, n1):
    gh = gh_ref[...]
    gl = gl_ref[...]
    skip = skip_ref[...]
    rows = z_ref.shape[1] // SUBLANES
    for i in range(SUBLANES):
        qsel = pl.ds(i, n1, stride=SUBLANES)
        sel = pl.ds(i, rows, stride=SUBLANES)
        y = _dot3c(gh, gl, jnp.concatenate([qr_ref[qsel, :], qi_ref[qsel, :]], axis=0))
        for p in range(2):
            conv = y[p * rows:(p + 1) * rows]
            o_ref[p, sel, :] = (conv + z_ref[p, sel, :] * skip) * x0_ref[p, sel, :]


def _fft_stage3(qr, qi, z, x0, skip, n1, D, g1):
    k8 = z.shape[3]
    gh, gl = _hilo(g1)
    qblk = pl.BlockSpec((None, None, n1 * SUBLANES, LANES), lambda p, j, c: (p, j, 0, c))
    zblk = pl.BlockSpec((None, 2, None, k8, LANES), lambda p, j, c: (p, 0, j, 0, c))
    const = lambda shape: pl.BlockSpec(shape, lambda p, j, c: (0,) * len(shape))
    return pl.pallas_call(
        functools.partial(_fft3_kernel, n1=n1),
        grid=(z.shape[0], FFT_J, D // LANES),
        in_specs=[qblk, qblk, const(g1.shape), const(g1.shape), zblk, zblk,
                  pl.BlockSpec((1, LANES), lambda p, j, c: (0, c))],
        out_specs=zblk,
        out_shape=jax.ShapeDtypeStruct(z.shape, F32),
        compiler_params=_params(("parallel", "parallel", "parallel"), 32),
        name="fft_stage3",
    )(qr, qi, gh, gl, z, x0, skip.reshape(1, D).astype(F32))


def long_conv_gate(z, x0, skip, kern):
    B, _, k1, _, D = z.shape
    n2 = FFT_N2
    n1 = 2 * k1
    assert B % 2 == 0
    t = _dft_tables(n1, n2)
    ka = _fft_stage1(kern.reshape(1, 1, FFT_J, n1 * SUBLANES, D), n1, D, t["f1_real"], t["tw1"])
    kf = _fft_stage2(ka[0], ka[1], n1, n2, D, t["f2"], scale=1.0 / (n1 * n2))
    kf = (kf[0].reshape(n1, n2, D), kf[1].reshape(n1, n2, D))
    pairs = lambda v: v.reshape(B // 2, 2, FFT_J, k1 * SUBLANES, D)
    a = _fft_stage1(pairs(z), n1, D, t["f1_cplx"], t["tw1"])
    q = _fft_stage2(a[0], a[1], n1, n2, D, t["f2"], kf=kf, g2=t["g2"], tw2=t["tw2"])
    return _fft_stage3(q[0], q[1], pairs(z), pairs(x0), skip, n1, D, t["g1_cplx"]).reshape(z.shape)


def _rope_tables(S):
    inv = ROPE_THETA ** (-jnp.arange(0, HEAD_DIM, 2, dtype=F32) / HEAD_DIM)
    ang = jnp.arange(S, dtype=F32)[:, None] * inv[None, :]
    cos, sin = jnp.cos(ang), jnp.sin(ang)
    return jnp.concatenate([cos, cos], axis=-1), jnp.concatenate([-sin, sin], axis=-1)


def _trunk(x, p):
    B, S, D = x.shape
    H = D // HEAD_DIM
    G = len(GROUP_DILATIONS)
    T = B * S
    cos, sin = _rope_tables(S)
    x = x.reshape(T, D)
    depth = p["mix_norm"].shape[0]
    for i in range(depth):
        j = i // 2
        if i % 2 == 0:
            scale = HEAD_DIM ** -0.5
            gains = jnp.stack([p["attn_q_gain"][j] * scale, p["attn_k_gain"][j], jnp.ones_like(p["attn_k_gain"][j])],
                              axis=1).reshape(G * 3, 1, HEAD_DIM).astype(F32)
            qkv = qkv_project(x, p["mix_norm"][i], p["attn_w_qkv"][j], gains, cos, sin, S)
            o = window_attention(qkv, B, S, H)
            x = matmul_resid(o, p["attn_w_out"][j], x)
        else:
            u = norm_matmul(x, p["mix_norm"][i], p["hy_w_in"][j], p["hy_b_in"][j])
            z, x0 = short_conv_gate(u.reshape(B, S, 3 * D), p["hy_conv_w"][j], p["hy_conv_b"][j])
            kern = conv_kernel_signal(S, D, p["hy_filt_w1"][j], p["hy_filt_b1"][j], p["hy_filt_w2"][j],
                                      p["hy_filt_b2"][j], p["hy_filt_w3"][j], p["hy_filt_b3"][j],
                                      p["hy_filt_freq"][j], p["hy_filt_w4"][j])
            y = long_conv_gate(z, x0, p["hy_skip"][j], kern)
            x = matmul_resid(y, p["hy_w_out"][j], x, p["hy_b_out"][j])
        h = norm_matmul(x, p["mlp_norm"][i], p["mlp_w1"][i], relu2=True, out_dtype=BF16)
        x = matmul_resid(h, p["mlp_w2"][i], x)
    return x.reshape(B, S, D)


def kernel(x_prompt, x_sample, mix_norm, mlp_norm, attn_w_qkv, attn_q_gain, attn_k_gain, attn_w_out, hy_w_in, hy_b_in, hy_conv_w, hy_conv_b, hy_filt_w1, hy_filt_b1, hy_filt_w2, hy_filt_b2, hy_filt_w3, hy_filt_b3, hy_filt_freq, hy_filt_w4, hy_skip, hy_w_out, hy_b_out, mlp_w1, mlp_w2):
    p = dict(
        mix_norm=mix_norm, mlp_norm=mlp_norm,
        attn_w_qkv=attn_w_qkv.astype(BF16), attn_q_gain=attn_q_gain, attn_k_gain=attn_k_gain,
        attn_w_out=attn_w_out.astype(BF16),
        hy_w_in=hy_w_in.astype(BF16), hy_b_in=hy_b_in, hy_conv_w=hy_conv_w, hy_conv_b=hy_conv_b,
        hy_filt_w1=hy_filt_w1, hy_filt_b1=hy_filt_b1, hy_filt_w2=hy_filt_w2, hy_filt_b2=hy_filt_b2,
        hy_filt_w3=hy_filt_w3, hy_filt_b3=hy_filt_b3, hy_filt_freq=hy_filt_freq, hy_filt_w4=hy_filt_w4,
        hy_skip=hy_skip, hy_w_out=hy_w_out.astype(BF16), hy_b_out=hy_b_out,
        mlp_w1=mlp_w1.astype(BF16), mlp_w2=mlp_w2.astype(BF16),
    )
    return (_trunk(x_prompt, p), _trunk(x_sample, p))
```

```python
import functools
import math

import numpy as np
import jax
import jax.numpy as jnp
from jax import lax
from jax.experimental import pallas as pl
from jax.experimental.pallas import tpu as pltpu

F32 = jnp.float32
BF16 = jnp.bfloat16

HEAD_DIM = 128
GROUP_WINDOWS = (128, 512, 2048)
GROUP_DILATIONS = (1, 4, 16)
ROPE_THETA = 10000.0
FILTER_EMB = 33
FILTER_BANDS = (FILTER_EMB - 1) // 2
DECAY_TARGET = 1e-2
FAST_DECAY_PCT = 0.3
SLOW_DECAY_PCT = 1.5
EPS = 1e-6

LANES = 128
SUBLANES = 8
ATTN_TILE = 1024
ATTN_BATCH = 8
FFT_N2 = 128
FFT_J = FFT_N2 // SUBLANES
FFT_LANE_TILES = 4
MIB = 1024 * 1024


def _params(semantics, vmem_mib):
    return pltpu.CompilerParams(dimension_semantics=semantics, vmem_limit_bytes=vmem_mib * MIB)


def _tile(n, pref, quantum=LANES):
    if n <= pref:
        return n
    t = (pref // quantum) * quantum
    while t > quantum and n % t:
        t -= quantum
    assert n % t == 0, (n, pref)
    return t


def _split(x):
    hi = x.astype(BF16)
    lo = (x - hi.astype(F32)).astype(BF16)
    return hi, lo


def _dot(a, b):
    return jnp.dot(a, b, preferred_element_type=F32)


def _dot3(a, b):
    ah, al = _split(a)
    bh, bl = _split(b)
    return _dot(ah, bh) + (_dot(ah, bl) + _dot(al, bh))


def _dot3c(ch, cl, b):
    bh, bl = _split(b)
    return _dot(ch, bh) + (_dot(ch, bl) + _dot(cl, bh))


def _rms_rows_to(x_ref, g_ref, xn_ref, row_chunk):
    def body(c, carry):
        rows = pl.ds(pl.multiple_of(c * row_chunk, row_chunk), row_chunk)
        x = x_ref[rows, :]
        inv = lax.rsqrt(jnp.mean(x * x, axis=-1, keepdims=True) + EPS)
        xn_ref[rows, :] = (x * inv * g_ref[...]).astype(BF16)
        return carry

    lax.fori_loop(0, x_ref.shape[0] // row_chunk, body, 0)


def _store_grouped(o_ref, v):
    for c in range(o_ref.shape[1]):
        for a in range(o_ref.shape[2]):
            o_ref[:, c, a] = v[a * FFT_N2:(a + 1) * FFT_N2, c * LANES:(c + 1) * LANES].reshape(FFT_J, SUBLANES, LANES)


def _load_grouped(a_ref):
    rows = [jnp.concatenate([a_ref[:, c, s].reshape(FFT_N2, LANES) for c in range(a_ref.shape[1])], axis=1)
            for s in range(a_ref.shape[2])]
    return rows[0] if len(rows) == 1 else jnp.concatenate(rows, axis=0)


def _norm_mm_kernel(x_ref, g_ref, w_ref, *rest, has_bias, relu2, row_chunk):
    if has_bias:
        b_ref, o_ref, xn_ref = rest
    else:
        o_ref, xn_ref = rest

    @pl.when(pl.program_id(1) == 0)
    def _():
        _rms_rows_to(x_ref, g_ref, xn_ref, row_chunk)

    acc = _dot(xn_ref[...], w_ref[...])
    if has_bias:
        acc = acc + b_ref[...]
    if relu2:
        acc = jnp.maximum(acc, 0.0)
        acc = acc * acc
    o_ref[...] = acc.astype(o_ref.dtype)


def norm_matmul(x, g, w, bias=None, *, relu2=False, out_dtype=F32, tm=1024, tn=1024):
    T, K = x.shape
    N = w.shape[1]
    tm = _tile(T, tm)
    tn = _tile(N, tn)
    in_specs = [
        pl.BlockSpec((tm, K), lambda i, j: (i, 0)),
        pl.BlockSpec((1, K), lambda i, j: (0, 0)),
        pl.BlockSpec((K, tn), lambda i, j: (0, j)),
    ]
    args = [x, g.reshape(1, K), w]
    if bias is not None:
        in_specs.append(pl.BlockSpec((1, tn), lambda i, j: (0, j)))
        args.append(bias.reshape(1, N))
    return pl.pallas_call(
        functools.partial(_norm_mm_kernel, has_bias=bias is not None, relu2=relu2, row_chunk=min(tm, 128)),
        grid=(T // tm, N // tn),
        in_specs=in_specs,
        out_specs=pl.BlockSpec((tm, tn), lambda i, j: (i, j)),
        out_shape=jax.ShapeDtypeStruct((T, N), out_dtype),
        scratch_shapes=[pltpu.VMEM((tm, K), BF16)],
        compiler_params=_params(("parallel", "arbitrary"), 48),
        name="norm_matmul",
    )(*args)


def _qkv_kernel(x_ref, g_ref, w_ref, gain_ref, cos_ref, sin_ref, o_ref, xn_ref, stage_ref, *, row_chunk,
                tiles_per_part, d, n_split):
    j = pl.program_id(1)

    @pl.when(j == 0)
    def _():
        _rms_rows_to(x_ref, g_ref, xn_ref, row_chunk)

    is_v = j // tiles_per_part == 2
    tm, tn = xn_ref.shape[0], w_ref.shape[1]
    gain = gain_ref[...]
    cos = cos_ref[...]
    sin = sin_ref[...]
    for h in range(n_split):
        acc = _dot(xn_ref[...], w_ref[:, h * (tn // n_split):(h + 1) * (tn // n_split)])
        for t in range(tn // n_split // HEAD_DIM):
            s = h * (tn // n_split // HEAD_DIM) + t
            a = acc[:, t * HEAD_DIM:(t + 1) * HEAD_DIM]
            inv = jnp.where(is_v, 1.0, lax.rsqrt(jnp.mean(a * a, axis=-1, keepdims=True) + EPS))
            y = a * inv * gain
            y = y * cos + pltpu.roll(y, HEAD_DIM // 2, axis=1) * sin
            if d == 1:
                o_ref[s] = y.astype(o_ref.dtype)
            else:
                stage_ref[s] = y
                n_r = tm // d
                for r in range(d):
                    o_ref[s, r * n_r:(r + 1) * n_r, :] = stage_ref[s, pl.ds(r, n_r, stride=d), :].astype(o_ref.dtype)


def qkv_project(x, g, w, gains, cos, sin, seq_len, group, *, tn=1024, n_split=4):
    T, K = x.shape
    part_width = w.shape[1] // (len(GROUP_DILATIONS) * 3)
    tm = ATTN_TILE
    assert seq_len % tm == 0
    tn = _tile(part_width, tn)
    pos_tiles = seq_len // tm
    tiles_per_part = part_width // tn
    table = lambda i, j: (j // tiles_per_part // 2, i % pos_tiles, 0)
    return pl.pallas_call(
        functools.partial(_qkv_kernel, row_chunk=min(tm, 128), tiles_per_part=tiles_per_part,
                          d=GROUP_DILATIONS[group], n_split=n_split),
        grid=(T // tm, 3 * tiles_per_part),
        in_specs=[
            pl.BlockSpec((tm, K), lambda i, j: (i, 0)),
            pl.BlockSpec((1, K), lambda i, j: (0, 0)),
            pl.BlockSpec((K, tn), lambda i, j: (0, group * 3 * tiles_per_part + j)),
            pl.BlockSpec((None, 1, HEAD_DIM), lambda i, j: (group * 3 + j // tiles_per_part, 0, 0)),
            pl.BlockSpec((None, tm, HEAD_DIM), table),
            pl.BlockSpec((None, tm, HEAD_DIM), table),
        ],
        out_specs=pl.BlockSpec((tn // HEAD_DIM, None, tm, HEAD_DIM),
                               lambda i, j: (j % tiles_per_part, (j // tiles_per_part + 2) % 3, i, 0)),
        out_shape=jax.ShapeDtypeStruct((part_width // HEAD_DIM, 3, T, HEAD_DIM), BF16),
        scratch_shapes=[pltpu.VMEM((tm, K), BF16), pltpu.VMEM((tn // HEAD_DIM, tm, HEAD_DIM), F32)],
        compiler_params=_params(("parallel", "arbitrary"), 48),
        name="qkv_project",
    )(x, g.reshape(1, K), w, gains, cos, sin)


def _mm_resid_kernel(a_ref, w_ref, *rest, has_bias, grouped):
    if has_bias:
        b_ref, r_ref, o_ref = rest
    else:
        r_ref, o_ref = rest
    k = pl.program_id(2)
    a = _load_grouped(a_ref) if grouped else a_ref[...]
    part = _dot(a.astype(BF16), w_ref[...])

    @pl.when(k == 0)
    def _():
        first = part + r_ref[...]
        if has_bias:
            first = first + b_ref[...]
        o_ref[...] = first

    @pl.when(k > 0)
    def _():
        o_ref[...] += part


def matmul_resid(a, w, resid, bias=None, *, tm=1024, tn=1024, tk=2048):
    grouped = a.ndim == 6
    T, N = resid.shape
    K = w.shape[0]
    tn = _tile(N, tn)
    tk = _tile(K, tk)
    if grouped:
        seq_len = a.shape[3] * FFT_N2
        tm = _tile(seq_len, tm)
        seq_tiles = seq_len // tm
        a_spec = pl.BlockSpec((None, FFT_J, tk // LANES, tm // FFT_N2, SUBLANES, LANES),
                              lambda i, j, k: (i // seq_tiles, 0, k, i % seq_tiles, 0, 0))
    else:
        tm = _tile(T, tm)
        a_spec = pl.BlockSpec((tm, tk), lambda i, j, k: (i, k))
    in_specs = [
        a_spec,
        pl.BlockSpec((tk, tn), lambda i, j, k: (k, j)),
    ]
    args = [a, w]
    if bias is not None:
        in_specs.append(pl.BlockSpec((1, tn), lambda i, j, k: (0, j)))
        args.append(bias.reshape(1, N))
    in_specs.append(pl.BlockSpec((tm, tn), lambda i, j, k: (i, j)))
    args.append(resid)
    return pl.pallas_call(
        functools.partial(_mm_resid_kernel, has_bias=bias is not None, grouped=grouped),
        grid=(T // tm, N // tn, K // tk),
        in_specs=in_specs,
        out_specs=pl.BlockSpec((tm, tn), lambda i, j, k: (i, j)),
        out_shape=jax.ShapeDtypeStruct((T, N), F32),
        compiler_params=_params(("parallel", "parallel", "arbitrary"), 48),
        name="matmul_resid",
    )(*args)


def _attn_kernel(*refs, tq, dils, half, n_tiles):
    G = len(dils)
    cur_refs, prev_refs, next_refs = refs[0:G], refs[G:2 * G], refs[2 * G:3 * G]
    o_ref = refs[3 * G]
    acc_ref, m_ref, l_ref, out_ref, bias_ref = refs[3 * G + 1:]
    i = pl.program_id(2)
    has_prev = i > 0
    has_next = i < n_tiles - 1

    for g, d in enumerate(dils):
        n_r = tq // d
        qs = min(128, n_r)
        ks = qs + 2 * half
        n_sub = n_r // qs
        qq = lax.broadcasted_iota(jnp.int32, (qs, ks), 0)
        kk = lax.broadcasted_iota(jnp.int32, (qs, ks), 1)
        band = (kk >= qq) & (kk <= qq + 2 * half)
        prev_ok = (kk >= half) | has_prev
        next_ok = (kk < ks - half) | has_next
        for idx, valid in enumerate((band & prev_ok, band, band & next_ok, band & prev_ok & next_ok)):
            bias_ref[idx, 0:qs, 0:ks] = jnp.where(valid, 0.0, -jnp.inf)

        def window(c, r, u, g=g, n_r=n_r, qs=qs):
            lo, hi = u * qs - half, (u + 1) * qs + half
            pieces = [prev_refs[g][c, r]] if lo < 0 else []
            pieces.append(cur_refs[g][c, r, max(lo, 0):min(hi, n_r), :])
            if hi > n_r:
                pieces.append(next_refs[g][c, r])
            return pieces[0] if len(pieces) == 1 else jnp.concatenate(pieces, axis=0)

        blocks = [(r, u) for r in range(d) for u in range(n_sub)]
        for b0 in range(0, len(blocks), ATTN_BATCH):
            batch = blocks[b0:b0 + ATTN_BATCH]
            edge = lambda u: (3 if n_sub == 1 else 0) if u == 0 else (2 if u == n_sub - 1 else 1)
            q = jnp.stack([cur_refs[g][2, r, u * qs:(u + 1) * qs, :] for r, u in batch])
            kw = jnp.stack([window(0, r, u) for r, u in batch])
            vw = jnp.stack([window(1, r, u) for r, u in batch])
            bias = jnp.stack([bias_ref[edge(u), 0:qs, 0:ks] for _, u in batch])
            s = jnp.einsum("bqd,bkd->bqk", q, kw, preferred_element_type=F32) + bias
            m_blk = jnp.max(s, axis=-1, keepdims=True)
            p = jnp.exp(s - m_blk)
            l_blk = jnp.sum(p, axis=-1, keepdims=True)
            pv = jnp.einsum("bqk,bkd->bqd", p.astype(BF16), vw, preferred_element_type=F32)
            for b, (r, u) in enumerate(batch):
                rows = pl.ds(r * n_r + u * qs, qs)
                m_ref[g, rows, :] = m_blk[b]
                l_ref[g, rows, :] = l_blk[b]
                acc_ref[g, rows, :] = pv[b]

    dmax = max(dils)
    n_max = tq // dmax
    for r in range(dmax):
        sel = [pl.ds((r % d) * (tq // d) + r // d, n_max, stride=dmax // d) if d < dmax else pl.ds(r * n_max, n_max)
               for d in dils]
        ms = [m_ref[g, sel[g], :] for g in range(G)]
        m = functools.reduce(jnp.maximum, ms)
        ws = [jnp.exp(mg - m) for mg in ms]
        num = sum(ws[g] * acc_ref[g, sel[g], :] for g in range(G))
        den = sum(ws[g] * l_ref[g, sel[g], :] for g in range(G))
        out_ref[r * n_max:(r + 1) * n_max, :] = num / den
    o_ref[...] = jnp.concatenate([out_ref[pl.ds(j, dmax, stride=n_max), :] for j in range(n_max)],
                                 axis=0).astype(o_ref.dtype)


def window_attention(qkv, batch, seq_len, n_heads):
    G = len(GROUP_DILATIONS)
    tq = ATTN_TILE
    assert seq_len % tq == 0
    n_tiles = seq_len // tq
    total_tiles = batch * n_tiles
    half = GROUP_WINDOWS[0] // (2 * GROUP_DILATIONS[0])
    for wdw, d in zip(GROUP_WINDOWS, GROUP_DILATIONS):
        assert wdw // (2 * d) == half and half * d <= tq and tq % (d * min(128, tq // d)) == 0
    H = n_heads
    dils = GROUP_DILATIONS

    cur, prev, nxt = [], [], []
    for d in dils:
        cur.append(pl.BlockSpec((None, 3, None, d, tq // d, HEAD_DIM),
                                lambda b, h, i: (h, 0, b * n_tiles + i, 0, 0, 0)))
        prev.append(pl.BlockSpec((None, 2, None, d, half, HEAD_DIM),
                                 lambda b, h, i, d=d: (h, 0, jnp.maximum(b * n_tiles + i - 1, 0), 0,
                                                       tq // d // half - 1, 0)))
        nxt.append(pl.BlockSpec((None, 2, None, d, half, HEAD_DIM),
                                lambda b, h, i: (h, 0, jnp.minimum(b * n_tiles + i + 1, total_tiles - 1), 0, 0, 0)))
    views = [t.reshape(H, 3, total_tiles, d, tq // d, HEAD_DIM) for t, d in zip(qkv, dils)]
    in_specs = cur + prev + nxt
    return pl.pallas_call(
        functools.partial(_attn_kernel, tq=tq, dils=dils, half=half, n_tiles=n_tiles),
        grid=(batch, H, n_tiles),
        in_specs=in_specs,
        out_specs=pl.BlockSpec((tq, HEAD_DIM), lambda b, h, i: (b * n_tiles + i, h)),
        out_shape=jax.ShapeDtypeStruct((batch * seq_len, H * HEAD_DIM), BF16),
        scratch_shapes=[
            pltpu.VMEM((G, tq, HEAD_DIM), F32),
            pltpu.VMEM((G, tq, 1), F32),
            pltpu.VMEM((G, tq, 1), F32),
            pltpu.VMEM((tq, HEAD_DIM), F32),
            pltpu.VMEM((4, 128, 128 + 2 * half), F32),
        ],
        compiler_params=_params(("parallel", "parallel", "parallel"), 40),
        name="window_attention",
    )(*(views * 3))


def _sconv_kernel(*refs, n_tiles):
    (x0m, x0p, x0n, x1m, x1p, x1n, vm, vp, vn, w0, w1, w2, b0, b1, b2, z_ref, x0_ref) = refs
    i = pl.program_id(1)

    def conv(m_ref, p_ref, n_ref, w_ref, b_ref):
        x = m_ref[...]
        ts = x.shape[0]
        prev_row = jnp.where(i > 0, p_ref[7:8, :], 0.0)
        next_row = jnp.where(i < n_tiles - 1, n_ref[0:1, :], 0.0)
        row = lax.broadcasted_iota(jnp.int32, x.shape, 0)
        up = jnp.where(row == 0, prev_row, pltpu.roll(x, 1, axis=0))
        dn = jnp.where(row == ts - 1, next_row, pltpu.roll(x, ts - 1, axis=0))
        return up * w_ref[0:1, :] + x * w_ref[1:2, :] + dn * w_ref[2:3, :] + b_ref[...]

    _store_grouped(x0_ref, conv(x0m, x0p, x0n, w0, b0))
    _store_grouped(z_ref, conv(vm, vp, vn, w2, b2) * conv(x1m, x1p, x1n, w1, b1))


def short_conv_gate(u, conv_w, conv_b, *, ts=512, ct=512):
    B, L, D3 = u.shape
    D = D3 // 3
    ts = _tile(L, ts, FFT_N2)
    ct = _tile(D, ct)
    n_tiles = L // ts
    n_ct = D // ct
    rows8 = ts // 8
    last8 = L // 8 - 1
    specs, args = [], []
    for part in range(3):
        off = part * n_ct
        specs += [
            pl.BlockSpec((None, ts, ct), lambda b, i, j, off=off: (b, i, off + j)),
            pl.BlockSpec((None, 8, ct), lambda b, i, j, off=off: (b, jnp.maximum(i * rows8 - 1, 0), off + j)),
            pl.BlockSpec((None, 8, ct), lambda b, i, j, off=off: (b, jnp.minimum((i + 1) * rows8, last8), off + j)),
        ]
        args += [u, u, u]
    for part in range(3):
        specs.append(pl.BlockSpec((3, ct), lambda b, i, j, off=part * n_ct: (0, off + j)))
        args.append(conv_w)
    for part in range(3):
        specs.append(pl.BlockSpec((1, ct), lambda b, i, j, off=part * n_ct: (0, off + j)))
        args.append(conv_b.reshape(1, D3))
    out_spec = pl.BlockSpec((None, FFT_J, ct // LANES, ts // FFT_N2, SUBLANES, LANES),
                            lambda b, i, j: (b, 0, j, i, 0, 0))
    return pl.pallas_call(
        functools.partial(_sconv_kernel, n_tiles=n_tiles),
        grid=(B, n_tiles, n_ct),
        in_specs=specs,
        out_specs=[out_spec, out_spec],
        out_shape=[jax.ShapeDtypeStruct((B, FFT_J, D // LANES, L // FFT_N2, SUBLANES, LANES), F32)] * 2,
        compiler_params=_params(("parallel", "parallel", "parallel"), 32),
        name="short_conv_gate",
    )(*args)


def _filter_kernel(feat_ref, tv_ref, w1, b1, w2, b2, w3, b3, fr_ref, w4, delta_ref, o_ref):
    fr = fr_ref[...]
    h = jnp.sin(fr * (_dot3(feat_ref[...], w1[...]) + b1[...]))
    h = jnp.sin(fr * (_dot3(h, w2[...]) + b2[...]))
    h = jnp.sin(fr * (_dot3(h, w3[...]) + b3[...]))
    out = _dot3(h, w4[...])
    t = tv_ref[:, 0:1]
    valid = tv_ref[:, 1:2]
    _store_grouped(o_ref, out * jnp.exp(-t * delta_ref[...]) * valid)


def conv_kernel_signal(L, D, fw1, fb1, fw2, fb2, fw3, fb3, ffreq, fw4, *, tl=256):
    FH = fw2.shape[0]
    FE = 64
    p = jnp.arange(2 * L, dtype=jnp.int32)
    pos = jnp.where(p < L, p, 2 * L - p).astype(F32)
    t = pos / (L - 1)
    bands = jnp.linspace(1e-4, FILTER_BANDS - 1, FILTER_BANDS, dtype=F32)
    ang = (2.0 * math.pi / L) * pos[:, None] * bands[None, :]
    feat = jnp.concatenate(
        [t[:, None], jnp.cos(ang), -jnp.sin(ang), jnp.zeros((2 * L, FE - FILTER_EMB), F32)], axis=-1)
    tv = jnp.stack([t, (p != L).astype(F32)], axis=-1)
    deltas = np.abs(np.linspace(math.log(DECAY_TARGET) / SLOW_DECAY_PCT, math.log(DECAY_TARGET) / FAST_DECAY_PCT, D,
                                dtype=np.float32)).reshape(1, D)
    w1p = jnp.pad(fw1.astype(F32), ((0, FE - FILTER_EMB), (0, 0)))
    tl = _tile(L, tl, FFT_N2)
    half_tiles = L // tl
    full = lambda shape: pl.BlockSpec(shape, lambda i: (0, 0))
    return pl.pallas_call(
        _filter_kernel,
        grid=(2 * L // tl,),
        in_specs=[
            pl.BlockSpec((tl, FE), lambda i: (i, 0)),
            pl.BlockSpec((tl, 2), lambda i: (i, 0)),
            full((FE, FH)), full((1, FH)), full((FH, FH)), full((1, FH)), full((FH, FH)), full((1, FH)),
            full((1, FH)),
            pl.BlockSpec((FH, D), lambda i: (0, i // half_tiles)),
            full((1, D)),
        ],
        out_specs=pl.BlockSpec((FFT_J, D // LANES, tl // FFT_N2, SUBLANES, LANES), lambda i: (0, 0, i, 0, 0)),
        out_shape=jax.ShapeDtypeStruct((FFT_J, D // LANES, 2 * L // FFT_N2, SUBLANES, LANES), F32),
        compiler_params=_params(("parallel",), 32),
        name="hyena_filter",
    )(feat, tv, w1p, fb1.reshape(1, FH).astype(F32), fw2.astype(F32), fb2.reshape(1, FH).astype(F32),
      fw3.astype(F32), fb3.reshape(1, FH).astype(F32), ffreq.reshape(1, FH).astype(F32), fw4.astype(F32),
      jnp.asarray(deltas))


def _hilo(m):
    m = jnp.asarray(np.asarray(m, np.float64), dtype=F32)
    return _split(m)


def _stack(re, im):
    return np.block([[re, -im], [im, re]])


def _dft_tables(n1, n2):
    n, k1 = n1 * n2, n1 // 2
    a1 = 2.0 * np.pi * np.outer(np.arange(n1), np.arange(n1)) / n1
    c1, s1 = np.cos(a1), np.sin(a1)
    a2 = 2.0 * np.pi * np.outer(np.arange(n2), np.arange(n2)) / n2
    c2, s2 = np.cos(a2), np.sin(a2)
    at = 2.0 * np.pi * np.outer(np.arange(n2), np.arange(n1)) / n
    bcast = lambda t: jnp.broadcast_to(jnp.asarray(t, F32)[:, :, None], t.shape + (LANES,))
    return dict(
        f1_real=np.concatenate([c1, -s1], axis=0),
        f1_cplx=_stack(c1[:, :k1], -s1[:, :k1]),
        f2=_stack(c2, -s2),
        g2=_stack(c2, s2),
        g1_cplx=_stack(c1[:k1], s1[:k1]),
        tw1=(bcast(np.cos(at)), bcast(-np.sin(at))),
        tw2=(bcast(np.cos(at.T)), bcast(np.sin(at.T))),
    )


def _fft1_kernel(x_ref, fh_ref, fl_ref, twr_ref, twi_ref, ar_ref, ai_ref, *, n1):
    fh = fh_ref[...]
    fl = fl_ref[...]
    parts, cb, rows = x_ref.shape[0], x_ref.shape[1], x_ref.shape[2] // SUBLANES
    for i in range(SUBLANES):
        sel = pl.ds(i, rows, stride=SUBLANES)
        x = jnp.concatenate([jnp.concatenate([x_ref[p, c, sel, :] for c in range(cb)], axis=1)
                             for p in range(parts)], axis=0)
        r = _dot3c(fh, fl, x)
        re, im = r[:n1], r[n1:]
        twr = jnp.concatenate([twr_ref[i]] * cb, axis=1)
        twi = jnp.concatenate([twi_ref[i]] * cb, axis=1)
        ar = re * twr - im * twi
        ai = re * twi + im * twr
        for c in range(cb):
            ar_ref[c, pl.ds(i, n1, stride=SUBLANES), :] = ar[:, c * LANES:(c + 1) * LANES]
            ai_ref[c, pl.ds(i, n1, stride=SUBLANES), :] = ai[:, c * LANES:(c + 1) * LANES]


def _fft_stage1(x, n1, f1, tw1):
    P, parts, _, C, k8, _ = x.shape
    fh, fl = _hilo(f1)
    cb = min(C, FFT_LANE_TILES)
    const = lambda shape: pl.BlockSpec(shape, lambda j, p, c: (0,) * len(shape))
    out_spec = pl.BlockSpec((None, None, cb, n1 * SUBLANES, LANES), lambda j, p, c: (p, j, c, 0, 0))
    tw_spec = pl.BlockSpec((SUBLANES, n1, LANES), lambda j, p, c: (j, 0, 0))
    return pl.pallas_call(
        functools.partial(_fft1_kernel, n1=n1),
        grid=(FFT_J, P, C // cb),
        in_specs=[
            pl.BlockSpec((None, parts, None, cb, k8, LANES), lambda j, p, c: (p, 0, j, c, 0, 0)),
            const(f1.shape), const(f1.shape), tw_spec, tw_spec,
        ],
        out_specs=[out_spec, out_spec],
        out_shape=[jax.ShapeDtypeStruct((P, FFT_J, C, n1 * SUBLANES, LANES), F32)] * 2,
        compiler_params=_params(("parallel", "parallel", "parallel"), 40),
        name="fft_stage1",
    )(x, fh, fl, tw1[0], tw1[1])


def _fft2_kernel(*refs, n2, cc, scale, spectrum_only):
    if spectrum_only:
        ar_ref, ai_ref, fh_ref, fl_ref, yr_ref, yi_ref = refs
    else:
        ar_ref, ai_ref, kr_ref, ki_ref, fh_ref, fl_ref, gh_ref, gl_ref, twr_ref, twi_ref, yr_ref, yi_ref = refs
        twr = jnp.concatenate([twr_ref[...]] * (cc // LANES), axis=1)
        twi = jnp.concatenate([twi_ref[...]] * (cc // LANES), axis=1)
    fh = fh_ref[...]
    fl = fl_ref[...]
    lt = cc // LANES
    rows = lambda ref, c: jnp.concatenate([ref[:, c * lt + t].reshape(n2, LANES) for t in range(lt)], axis=1)
    for c in range(ar_ref.shape[1] // lt):
        cols = slice(c * cc, (c + 1) * cc)
        y = _dot3c(fh, fl, jnp.concatenate([rows(ar_ref, c), rows(ai_ref, c)], axis=0))
        yr, yi = y[:n2], y[n2:]
        if spectrum_only:
            yr_ref[:, cols] = yr * scale
            yi_ref[:, cols] = yi * scale
        else:
            kr = kr_ref[:, cols]
            ki = ki_ref[:, cols]
            pr = yr * kr - yi * ki
            pi = yr * ki + yi * kr
            q = _dot3c(gh_ref[...], gl_ref[...], jnp.concatenate([pr, pi], axis=0))
            qr, qi = q[:n2], q[n2:]
            outr = qr * twr - qi * twi
            outi = qr * twi + qi * twr
            for t in range(lt):
                yr_ref[:, c * lt + t] = outr[:, t * LANES:(t + 1) * LANES].reshape(FFT_J, SUBLANES, LANES)
                yi_ref[:, c * lt + t] = outi[:, t * LANES:(t + 1) * LANES].reshape(FFT_J, SUBLANES, LANES)


def _fft_stage2(ar, ai, n1, n2, D, f2, *, kf=None, g2=None, tw2=None, scale=1.0):
    B, _, C = ar.shape[:3]
    ar = ar.reshape(B, FFT_J, C, n1, SUBLANES, LANES)
    ai = ai.reshape(B, FFT_J, C, n1, SUBLANES, LANES)
    fh, fl = _hilo(f2)
    cc = _tile(D, 512)
    blk = pl.BlockSpec((None, FFT_J, C, None, SUBLANES, LANES), lambda f, b: (b, 0, 0, f, 0, 0))
    const = lambda shape: pl.BlockSpec(shape, lambda f, b: (0,) * len(shape))
    specs = [blk, blk]
    args = [ar, ai]
    if kf is not None:
        gh, gl = _hilo(g2)
        kblk = pl.BlockSpec((None, n2, D), lambda f, b: (f, 0, 0))
        tblk = pl.BlockSpec((None, n2, LANES), lambda f, b: (f, 0, 0))
        specs += [kblk, kblk, const((2 * n2, 2 * n2)), const((2 * n2, 2 * n2)), const((2 * n2, 2 * n2)),
                  const((2 * n2, 2 * n2)), tblk, tblk]
        args += [kf[0], kf[1], fh, fl, gh, gl, tw2[0], tw2[1]]
        out_spec = blk
        out_shape = jax.ShapeDtypeStruct((B, FFT_J, C, n1, SUBLANES, LANES), F32)
    else:
        specs += [const((2 * n2, 2 * n2)), const((2 * n2, 2 * n2))]
        args += [fh, fl]
        out_spec = pl.BlockSpec((None, None, n2, D), lambda f, b: (b, f, 0, 0))
        out_shape = jax.ShapeDtypeStruct((B, n1, n2, D), F32)
    out = pl.pallas_call(
        functools.partial(_fft2_kernel, n2=n2, cc=cc, scale=scale, spectrum_only=kf is None),
        grid=(n1, B),
        in_specs=specs,
        out_specs=[out_spec, out_spec],
        out_shape=[out_shape] * 2,
        compiler_params=_params(("parallel", "parallel"), 40),
        name="fft_stage2",
    )(*args)
    if kf is not None:
        out = [o.reshape(B, FFT_J, C, n1 * SUBLANES, LANES) for o in out]
    return out


def _fft3_kernel(qr_ref, qi_ref, gh_ref, gl_ref, z_ref, x0_ref, skip_ref, o_ref, *, n1):
    gh = gh_ref[...]
    gl = gl_ref[...]
    cb, rows = z_ref.shape[1], z_ref.shape[2] // SUBLANES
    for i in range(SUBLANES):
        qsel = pl.ds(i, n1, stride=SUBLANES)
        sel = pl.ds(i, rows, stride=SUBLANES)
        q = jnp.concatenate([jnp.concatenate([ref[c, qsel, :] for c in range(cb)], axis=1)
                             for ref in (qr_ref, qi_ref)], axis=0)
        y = _dot3c(gh, gl, q)
        for p in range(2):
            for c in range(cb):
                lanes = slice(c * LANES, (c + 1) * LANES)
                conv = y[p * rows:(p + 1) * rows, lanes]
                o_ref[p, c, sel, :] = (conv + z_ref[p, c, sel, :] * skip_ref[:, lanes]) * x0_ref[p, c, sel, :]


def _fft_stage3(qr, qi, z, x0, skip, n1, g1):
    P, _, _, C, k8, _ = z.shape
    gh, gl = _hilo(g1)
    cb = min(C, FFT_LANE_TILES)
    qblk = pl.BlockSpec((None, None, cb, n1 * SUBLANES, LANES), lambda p, j, c: (p, j, c, 0, 0))
    zblk = pl.BlockSpec((None, 2, None, cb, k8, LANES), lambda p, j, c: (p, 0, j, c, 0, 0))
    const = lambda shape: pl.BlockSpec(shape, lambda p, j, c: (0,) * len(shape))
    return pl.pallas_call(
        functools.partial(_fft3_kernel, n1=n1),
        grid=(P, FFT_J, C // cb),
        in_specs=[qblk, qblk, const(g1.shape), const(g1.shape), zblk, zblk,
                  pl.BlockSpec((1, cb * LANES), lambda p, j, c: (0, c))],
        out_specs=zblk,
        out_shape=jax.ShapeDtypeStruct(z.shape, F32),
        compiler_params=_params(("parallel", "parallel", "parallel"), 52),
        name="fft_stage3",
    )(qr, qi, gh, gl, z, x0, skip.reshape(1, C * LANES).astype(F32))


def long_conv_gate(z, x0, skip, kern):
    B, _, C, k1 = z.shape[:4]
    D = C * LANES
    n2 = FFT_N2
    n1 = 2 * k1
    assert B % 2 == 0
    t = _dft_tables(n1, n2)
    ka = _fft_stage1(kern.reshape(1, 1, FFT_J, C, n1 * SUBLANES, LANES), n1, t["f1_real"], t["tw1"])
    kf = _fft_stage2(ka[0], ka[1], n1, n2, D, t["f2"], scale=1.0 / (n1 * n2))
    kf = (kf[0].reshape(n1, n2, D), kf[1].reshape(n1, n2, D))
    pairs = lambda v: v.reshape(B // 2, 2, FFT_J, C, k1 * SUBLANES, LANES)
    a = _fft_stage1(pairs(z), n1, t["f1_cplx"], t["tw1"])
    q = _fft_stage2(a[0], a[1], n1, n2, D, t["f2"], kf=kf, g2=t["g2"], tw2=t["tw2"])
    return _fft_stage3(q[0], q[1], pairs(z), pairs(x0), skip, n1, t["g1_cplx"]).reshape(z.shape)


def _rope_tables(S):
    inv = ROPE_THETA ** (-jnp.arange(0, HEAD_DIM, 2, dtype=F32) / HEAD_DIM)
    ang = jnp.arange(S, dtype=F32)[:, None] * inv[None, :]
    cos, sin = jnp.cos(ang), jnp.sin(ang)
    cos2, sin2 = jnp.concatenate([cos, cos], axis=-1), jnp.concatenate([-sin, sin], axis=-1)
    return jnp.stack([cos2, jnp.ones_like(cos2)]), jnp.stack([sin2, jnp.zeros_like(sin2)])


def _trunk(x, p):
    B, S, D = x.shape
    H = D // HEAD_DIM
    G = len(GROUP_DILATIONS)
    T = B * S
    cos, sin = _rope_tables(S)
    x = x.reshape(T, D)
    depth = p["mix_norm"].shape[0]
    for i in range(depth):
        j = i // 2
        if i % 2 == 0:
            scale = HEAD_DIM ** -0.5
            gains = jnp.stack([p["attn_q_gain"][j] * scale, p["attn_k_gain"][j], jnp.ones_like(p["attn_k_gain"][j])],
                              axis=1).reshape(G * 3, 1, HEAD_DIM).astype(F32)
            qkv = [qkv_project(x, p["mix_norm"][i], p["attn_w_qkv"][j], gains, cos, sin, S, g) for g in range(G)]
            o = window_attention(qkv, B, S, H)
            x = matmul_resid(o, p["attn_w_out"][j], x)
        else:
            u = norm_matmul(x, p["mix_norm"][i], p["hy_w_in"][j], p["hy_b_in"][j])
            z, x0 = short_conv_gate(u.reshape(B, S, 3 * D), p["hy_conv_w"][j], p["hy_conv_b"][j])
            kern = conv_kernel_signal(S, D, p["hy_filt_w1"][j], p["hy_filt_b1"][j], p["hy_filt_w2"][j],
                                      p["hy_filt_b2"][j], p["hy_filt_w3"][j], p["hy_filt_b3"][j],
                                      p["hy_filt_freq"][j], p["hy_filt_w4"][j])
            y = long_conv_gate(z, x0, p["hy_skip"][j], kern)
            x = matmul_resid(y, p["hy_w_out"][j], x, p["hy_b_out"][j])
        h = norm_matmul(x, p["mlp_norm"][i], p["mlp_w1"][i], relu2=True, out_dtype=BF16)
        x = matmul_resid(h, p["mlp_w2"][i], x)
    return x.reshape(B, S, D)


def kernel(x_prompt, x_sample, mix_norm, mlp_norm, attn_w_qkv, attn_q_gain, attn_k_gain, attn_w_out, hy_w_in, hy_b_in, hy_conv_w, hy_conv_b, hy_filt_w1, hy_filt_b1, hy_filt_w2, hy_filt_b2, hy_filt_w3, hy_filt_b3, hy_filt_freq, hy_filt_w4, hy_skip, hy_w_out, hy_b_out, mlp_w1, mlp_w2):
    p = dict(
        mix_norm=mix_norm, mlp_norm=mlp_norm,
        attn_w_qkv=attn_w_qkv.astype(BF16), attn_q_gain=attn_q_gain, attn_k_gain=attn_k_gain,
        attn_w_out=attn_w_out.astype(BF16),
        hy_w_in=hy_w_in.astype(BF16), hy_b_in=hy_b_in, hy_conv_w=hy_conv_w, hy_conv_b=hy_conv_b,
        hy_filt_w1=hy_filt_w1, hy_filt_b1=hy_filt_b1, hy_filt_w2=hy_filt_w2, hy_filt_b2=hy_filt_b2,
        hy_filt_w3=hy_filt_w3, hy_filt_b3=hy_filt_b3, hy_filt_freq=hy_filt_freq, hy_filt_w4=hy_filt_w4,
        hy_skip=hy_skip, hy_w_out=hy_w_out.astype(BF16), hy_b_out=hy_b_out,
        mlp_w1=mlp_w1.astype(BF16), mlp_w2=mlp_w2.astype(BF16),
    )
    return (_trunk(x_prompt, p), _trunk(x_sample, p))
```

```python
import functools
import math

import numpy as np
import jax
import jax.numpy as jnp
from jax import lax
from jax.experimental import pallas as pl
from jax.experimental.pallas import tpu as pltpu

F32 = jnp.float32
BF16 = jnp.bfloat16

HEAD_DIM = 128
GROUP_WINDOWS = (128, 512, 2048)
GROUP_DILATIONS = (1, 4, 16)
ROPE_THETA = 10000.0
FILTER_EMB = 33
FILTER_BANDS = (FILTER_EMB - 1) // 2
DECAY_TARGET = 1e-2
FAST_DECAY_PCT = 0.3
SLOW_DECAY_PCT = 1.5
EPS = 1e-6

LANES = 128
SUBLANES = 8
ATTN_TILE = 1024
ATTN_BATCH = 8
FFT_N2 = 128
FFT_J = FFT_N2 // SUBLANES
FFT_LANE_TILES = 4
MIB = 1024 * 1024


def _params(semantics, vmem_mib):
    return pltpu.CompilerParams(dimension_semantics=semantics, vmem_limit_bytes=vmem_mib * MIB)


def _tile(n, pref, quantum=LANES):
    if n <= pref:
        return n
    t = (pref // quantum) * quantum
    while t > quantum and n % t:
        t -= quantum
    assert n % t == 0, (n, pref)
    return t


def _split(x):
    hi = x.astype(BF16)
    lo = (x - hi.astype(F32)).astype(BF16)
    return hi, lo


def _dot(a, b):
    return jnp.dot(a, b, preferred_element_type=F32)


def _dot3(a, b):
    ah, al = _split(a)
    bh, bl = _split(b)
    return _dot(ah, bh) + (_dot(ah, bl) + _dot(al, bh))


def _dot3c(ch, cl, b):
    bh, bl = _split(b)
    return _dot(ch, bh) + (_dot(ch, bl) + _dot(cl, bh))


def _rms_rows_to(x_ref, g_ref, xn_ref, row_chunk):
    def body(c, carry):
        rows = pl.ds(pl.multiple_of(c * row_chunk, row_chunk), row_chunk)
        x = x_ref[rows, :]
        inv = lax.rsqrt(jnp.mean(x * x, axis=-1, keepdims=True) + EPS)
        xn_ref[rows, :] = (x * inv * g_ref[...]).astype(BF16)
        return carry

    lax.fori_loop(0, x_ref.shape[0] // row_chunk, body, 0)


def _store_grouped(o_ref, v):
    for c in range(o_ref.shape[1]):
        for a in range(o_ref.shape[2]):
            o_ref[:, c, a] = v[a * FFT_N2:(a + 1) * FFT_N2, c * LANES:(c + 1) * LANES].reshape(FFT_J, SUBLANES, LANES)


def _load_grouped(a_ref):
    rows = [jnp.concatenate([a_ref[:, c, s].reshape(FFT_N2, LANES) for c in range(a_ref.shape[1])], axis=1)
            for s in range(a_ref.shape[2])]
    return rows[0] if len(rows) == 1 else jnp.concatenate(rows, axis=0)


def _norm_mm_kernel(x_ref, g_ref, w_ref, *rest, has_bias, relu2, row_chunk):
    if has_bias:
        b_ref, o_ref, xn_ref = rest
    else:
        o_ref, xn_ref = rest

    @pl.when(pl.program_id(1) == 0)
    def _():
        _rms_rows_to(x_ref, g_ref, xn_ref, row_chunk)

    acc = _dot(xn_ref[...], w_ref[...])
    if has_bias:
        acc = acc + b_ref[...]
    if relu2:
        acc = jnp.maximum(acc, 0.0)
        acc = acc * acc
    o_ref[...] = acc.astype(o_ref.dtype)


def norm_matmul(x, g, w, bias=None, *, relu2=False, out_dtype=F32, tm=1024, tn=1024):
    T, K = x.shape
    N = w.shape[1]
    tm = _tile(T, tm)
    tn = _tile(N, tn)
    in_specs = [
        pl.BlockSpec((tm, K), lambda i, j: (i, 0)),
        pl.BlockSpec((1, K), lambda i, j: (0, 0)),
        pl.BlockSpec((K, tn), lambda i, j: (0, j)),
    ]
    args = [x, g.reshape(1, K), w]
    if bias is not None:
        in_specs.append(pl.BlockSpec((1, tn), lambda i, j: (0, j)))
        args.append(bias.reshape(1, N))
    return pl.pallas_call(
        functools.partial(_norm_mm_kernel, has_bias=bias is not None, relu2=relu2, row_chunk=min(tm, 128)),
        grid=(T // tm, N // tn),
        in_specs=in_specs,
        out_specs=pl.BlockSpec((tm, tn), lambda i, j: (i, j)),
        out_shape=jax.ShapeDtypeStruct((T, N), out_dtype),
        scratch_shapes=[pltpu.VMEM((tm, K), BF16)],
        compiler_params=_params(("parallel", "arbitrary"), 48),
        name="norm_matmul",
    )(*args)


def _qkv_kernel(x_ref, g_ref, w_ref, gain_ref, cos_ref, sin_ref, o_ref, xn_ref, acc_a, acc_b, stage_ref, *,
                row_chunk, tiles_per_part, n_j, n_tiles, d):
    n = pl.program_id(0)
    tile = jnp.minimum(n, n_tiles - 1)
    done = jnp.maximum(n - 1, 0)
    tm = xn_ref.shape[0]

    @pl.when(n == 0)
    def _():
        acc_b[...] = jnp.zeros_like(acc_b)

    @pl.when((tile % n_j == 0) & (n < n_tiles))
    def _():
        _rms_rows_to(x_ref, g_ref, xn_ref, row_chunk)

    def step(acc_w, acc_r):
        acc_w[...] = _dot(xn_ref[...], w_ref[...])
        is_v = (done % n_j) // tiles_per_part == 2
        gain = gain_ref[...]
        cos = cos_ref[...]
        sin = sin_ref[...]
        for s in range(acc_r.shape[1] // HEAD_DIM):
            a = acc_r[:, s * HEAD_DIM:(s + 1) * HEAD_DIM]
            inv = jnp.where(is_v, 1.0, lax.rsqrt(jnp.mean(a * a, axis=-1, keepdims=True) + EPS))
            y = a * inv * gain
            y = y * cos + pltpu.roll(y, HEAD_DIM // 2, axis=1) * sin
            if d == 1:
                o_ref[s] = y.astype(o_ref.dtype)
            else:
                stage_ref[s] = y
                n_r = tm // d
                for r in range(d):
                    o_ref[s, r * n_r:(r + 1) * n_r, :] = stage_ref[s, pl.ds(r, n_r, stride=d), :].astype(o_ref.dtype)

    @pl.when(n % 2 == 0)
    def _():
        step(acc_a, acc_b)

    @pl.when(n % 2 == 1)
    def _():
        step(acc_b, acc_a)


def qkv_project(x, g, w, gains, cos, sin, seq_len, group, *, tn=1024):
    T, K = x.shape
    part_width = w.shape[1] // (len(GROUP_DILATIONS) * 3)
    tm = ATTN_TILE
    assert seq_len % tm == 0
    tn = _tile(part_width, tn)
    pos_tiles = seq_len // tm
    tiles_per_part = part_width // tn
    n_j = 3 * tiles_per_part
    n_tiles = (T // tm) * n_j
    mm = lambda n: jnp.minimum(n, n_tiles - 1)
    ep = lambda n: jnp.maximum(n - 1, 0)
    table = lambda n: ((ep(n) % n_j) // tiles_per_part // 2, (ep(n) // n_j) % pos_tiles, 0)
    return pl.pallas_call(
        functools.partial(_qkv_kernel, row_chunk=min(tm, 128), tiles_per_part=tiles_per_part, n_j=n_j,
                          n_tiles=n_tiles, d=GROUP_DILATIONS[group]),
        grid=(n_tiles + 1,),
        in_specs=[
            pl.BlockSpec((tm, K), lambda n: (mm(n) // n_j, 0)),
            pl.BlockSpec((1, K), lambda n: (0, 0)),
            pl.BlockSpec((K, tn), lambda n: (0, group * n_j + mm(n) % n_j)),
            pl.BlockSpec((None, 1, HEAD_DIM), lambda n: (group * 3 + (ep(n) % n_j) // tiles_per_part, 0, 0)),
            pl.BlockSpec((None, tm, HEAD_DIM), table),
            pl.BlockSpec((None, tm, HEAD_DIM), table),
        ],
        out_specs=pl.BlockSpec(
            (tn // HEAD_DIM, None, tm, HEAD_DIM),
            lambda n: (ep(n) % tiles_per_part, ((ep(n) % n_j) // tiles_per_part + 2) % 3, ep(n) // n_j, 0)),
        out_shape=jax.ShapeDtypeStruct((part_width // HEAD_DIM, 3, T, HEAD_DIM), BF16),
        scratch_shapes=[pltpu.VMEM((tm, K), BF16), pltpu.VMEM((tm, tn), F32), pltpu.VMEM((tm, tn), F32),
                        pltpu.VMEM((tn // HEAD_DIM, tm, HEAD_DIM), F32)],
        compiler_params=_params(("arbitrary",), 52),
        name="qkv_project",
    )(x, g.reshape(1, K), w, gains, cos, sin)


def _mm_resid_kernel(a_ref, w_ref, *rest, has_bias, grouped):
    if has_bias:
        b_ref, r_ref, o_ref = rest
    else:
        r_ref, o_ref = rest
    k = pl.program_id(2)
    a = _load_grouped(a_ref) if grouped else a_ref[...]
    part = _dot(a.astype(BF16), w_ref[...])

    @pl.when(k == 0)
    def _():
        first = part + r_ref[...]
        if has_bias:
            first = first + b_ref[...]
        o_ref[...] = first

    @pl.when(k > 0)
    def _():
        o_ref[...] += part


def matmul_resid(a, w, resid, bias=None, *, tm=1024, tn=1024, tk=2048):
    grouped = a.ndim == 6
    T, N = resid.shape
    K = w.shape[0]
    tn = _tile(N, tn)
    tk = _tile(K, tk)
    if grouped:
        seq_len = a.shape[3] * FFT_N2
        tm = _tile(seq_len, tm)
        seq_tiles = seq_len // tm
        a_spec = pl.BlockSpec((None, FFT_J, tk // LANES, tm // FFT_N2, SUBLANES, LANES),
                              lambda i, j, k: (i // seq_tiles, 0, k, i % seq_tiles, 0, 0))
    else:
        tm = _tile(T, tm)
        a_spec = pl.BlockSpec((tm, tk), lambda i, j, k: (i, k))
    in_specs = [
        a_spec,
        pl.BlockSpec((tk, tn), lambda i, j, k: (k, j)),
    ]
    args = [a, w]
    if bias is not None:
        in_specs.append(pl.BlockSpec((1, tn), lambda i, j, k: (0, j)))
        args.append(bias.reshape(1, N))
    in_specs.append(pl.BlockSpec((tm, tn), lambda i, j, k: (i, j)))
    args.append(resid)
    return pl.pallas_call(
        functools.partial(_mm_resid_kernel, has_bias=bias is not None, grouped=grouped),
        grid=(T // tm, N // tn, K // tk),
        in_specs=in_specs,
        out_specs=pl.BlockSpec((tm, tn), lambda i, j, k: (i, j)),
        out_shape=jax.ShapeDtypeStruct((T, N), F32),
        compiler_params=_params(("parallel", "parallel", "arbitrary"), 48),
        name="matmul_resid",
    )(*args)


def _attn_kernel(*refs, tq, dils, half, n_tiles):
    G = len(dils)
    cur_refs, prev_refs, next_refs = refs[0:G], refs[G:2 * G], refs[2 * G:3 * G]
    o_ref = refs[3 * G]
    acc_ref, m_ref, l_ref, out_ref, bias_ref = refs[3 * G + 1:]
    i = pl.program_id(2)
    has_prev = i > 0
    has_next = i < n_tiles - 1

    for g, d in enumerate(dils):
        n_r = tq // d
        qs = min(128, n_r)
        ks = qs + 2 * half
        n_sub = n_r // qs
        qq = lax.broadcasted_iota(jnp.int32, (qs, ks), 0)
        kk = lax.broadcasted_iota(jnp.int32, (qs, ks), 1)
        band = (kk >= qq) & (kk <= qq + 2 * half)
        prev_ok = (kk >= half) | has_prev
        next_ok = (kk < ks - half) | has_next
        for idx, valid in enumerate((band & prev_ok, band, band & next_ok, band & prev_ok & next_ok)):
            bias_ref[idx, 0:qs, 0:ks] = jnp.where(valid, 0.0, -jnp.inf)

        def window(c, r, u, g=g, n_r=n_r, qs=qs):
            lo, hi = u * qs - half, (u + 1) * qs + half
            pieces = [prev_refs[g][c, r]] if lo < 0 else []
            pieces.append(cur_refs[g][c, r, max(lo, 0):min(hi, n_r), :])
            if hi > n_r:
                pieces.append(next_refs[g][c, r])
            return pieces[0] if len(pieces) == 1 else jnp.concatenate(pieces, axis=0)

        blocks = [(r, u) for r in range(d) for u in range(n_sub)]
        for b0 in range(0, len(blocks), ATTN_BATCH):
            batch = blocks[b0:b0 + ATTN_BATCH]
            edge = lambda u: (3 if n_sub == 1 else 0) if u == 0 else (2 if u == n_sub - 1 else 1)
            q = jnp.stack([cur_refs[g][2, r, u * qs:(u + 1) * qs, :] for r, u in batch])
            kw = jnp.stack([window(0, r, u) for r, u in batch])
            vw = jnp.stack([window(1, r, u) for r, u in batch])
            bias = jnp.stack([bias_ref[edge(u), 0:qs, 0:ks] for _, u in batch])
            s = jnp.einsum("bqd,bkd->bqk", q, kw, preferred_element_type=F32) + bias
            m_blk = jnp.max(s, axis=-1, keepdims=True)
            p = jnp.exp(s - m_blk)
            l_blk = jnp.sum(p, axis=-1, keepdims=True)
            pv = jnp.einsum("bqk,bkd->bqd", p.astype(BF16), vw, preferred_element_type=F32)
            for b, (r, u) in enumerate(batch):
                rows = pl.ds(r * n_r + u * qs, qs)
                m_ref[g, rows, :] = m_blk[b]
                l_ref[g, rows, :] = l_blk[b]
                acc_ref[g, rows, :] = pv[b]

    dmax = max(dils)
    n_max = tq // dmax
    for r in range(dmax):
        sel = [pl.ds((r % d) * (tq // d) + r // d, n_max, stride=dmax // d) if d < dmax else pl.ds(r * n_max, n_max)
               for d in dils]
        ms = [m_ref[g, sel[g], :] for g in range(G)]
        m = functools.reduce(jnp.maximum, ms)
        ws = [jnp.exp(mg - m) for mg in ms]
        num = sum(ws[g] * acc_ref[g, sel[g], :] for g in range(G))
        den = sum(ws[g] * l_ref[g, sel[g], :] for g in range(G))
        out_ref[r * n_max:(r + 1) * n_max, :] = num / den
    o_ref[...] = jnp.concatenate([out_ref[pl.ds(j, dmax, stride=n_max), :] for j in range(n_max)],
                                 axis=0).astype(o_ref.dtype)


def window_attention(qkv, batch, seq_len, n_heads):
    G = len(GROUP_DILATIONS)
    tq = ATTN_TILE
    assert seq_len % tq == 0
    n_tiles = seq_len // tq
    total_tiles = batch * n_tiles
    half = GROUP_WINDOWS[0] // (2 * GROUP_DILATIONS[0])
    for wdw, d in zip(GROUP_WINDOWS, GROUP_DILATIONS):
        assert wdw // (2 * d) == half and half * d <= tq and tq % (d * min(128, tq // d)) == 0
    H = n_heads
    dils = GROUP_DILATIONS

    cur, prev, nxt = [], [], []
    for d in dils:
        cur.append(pl.BlockSpec((None, 3, None, d, tq // d, HEAD_DIM),
                                lambda b, h, i: (h, 0, b * n_tiles + i, 0, 0, 0)))
        prev.append(pl.BlockSpec((None, 2, None, d, half, HEAD_DIM),
                                 lambda b, h, i, d=d: (h, 0, jnp.maximum(b * n_tiles + i - 1, 0), 0,
                                                       tq // d // half - 1, 0)))
        nxt.append(pl.BlockSpec((None, 2, None, d, half, HEAD_DIM),
                                lambda b, h, i: (h, 0, jnp.minimum(b * n_tiles + i + 1, total_tiles - 1), 0, 0, 0)))
    views = [t.reshape(H, 3, total_tiles, d, tq // d, HEAD_DIM) for t, d in zip(qkv, dils)]
    in_specs = cur + prev + nxt
    return pl.pallas_call(
        functools.partial(_attn_kernel, tq=tq, dils=dils, half=half, n_tiles=n_tiles),
        grid=(batch, H, n_tiles),
        in_specs=in_specs,
        out_specs=pl.BlockSpec((tq, HEAD_DIM), lambda b, h, i: (b * n_tiles + i, h)),
        out_shape=jax.ShapeDtypeStruct((batch * seq_len, H * HEAD_DIM), BF16),
        scratch_shapes=[
            pltpu.VMEM((G, tq, HEAD_DIM), F32),
            pltpu.VMEM((G, tq, 1), F32),
            pltpu.VMEM((G, tq, 1), F32),
            pltpu.VMEM((tq, HEAD_DIM), F32),
            pltpu.VMEM((4, 128, 128 + 2 * half), F32),
        ],
        compiler_params=_params(("parallel", "parallel", "parallel"), 40),
        name="window_attention",
    )(*(views * 3))


def _sconv_kernel(*refs, n_tiles):
    (x0m, x0p, x0n, x1m, x1p, x1n, vm, vp, vn, w0, w1, w2, b0, b1, b2, z_ref, x0_ref) = refs
    i = pl.program_id(1)

    def conv(m_ref, p_ref, n_ref, w_ref, b_ref):
        x = m_ref[...]
        ts = x.shape[0]
        prev_row = jnp.where(i > 0, p_ref[7:8, :], 0.0)
        next_row = jnp.where(i < n_tiles - 1, n_ref[0:1, :], 0.0)
        row = lax.broadcasted_iota(jnp.int32, x.shape, 0)
        up = jnp.where(row == 0, prev_row, pltpu.roll(x, 1, axis=0))
        dn = jnp.where(row == ts - 1, next_row, pltpu.roll(x, ts - 1, axis=0))
        return up * w_ref[0:1, :] + x * w_ref[1:2, :] + dn * w_ref[2:3, :] + b_ref[...]

    _store_grouped(x0_ref, conv(x0m, x0p, x0n, w0, b0))
    _store_grouped(z_ref, conv(vm, vp, vn, w2, b2) * conv(x1m, x1p, x1n, w1, b1))


def short_conv_gate(u, conv_w, conv_b, *, ts=512, ct=512):
    B, L, D3 = u.shape
    D = D3 // 3
    ts = _tile(L, ts, FFT_N2)
    ct = _tile(D, ct)
    n_tiles = L // ts
    n_ct = D // ct
    rows8 = ts // 8
    last8 = L // 8 - 1
    specs, args = [], []
    for part in range(3):
        off = part * n_ct
        specs += [
            pl.BlockSpec((None, ts, ct), lambda b, i, j, off=off: (b, i, off + j)),
            pl.BlockSpec((None, 8, ct), lambda b, i, j, off=off: (b, jnp.maximum(i * rows8 - 1, 0), off + j)),
            pl.BlockSpec((None, 8, ct), lambda b, i, j, off=off: (b, jnp.minimum((i + 1) * rows8, last8), off + j)),
        ]
        args += [u, u, u]
    for part in range(3):
        specs.append(pl.BlockSpec((3, ct), lambda b, i, j, off=part * n_ct: (0, off + j)))
        args.append(conv_w)
    for part in range(3):
        specs.append(pl.BlockSpec((1, ct), lambda b, i, j, off=part * n_ct: (0, off + j)))
        args.append(conv_b.reshape(1, D3))
    out_spec = pl.BlockSpec((None, FFT_J, ct // LANES, ts // FFT_N2, SUBLANES, LANES),
                            lambda b, i, j: (b, 0, j, i, 0, 0))
    return pl.pallas_call(
        functools.partial(_sconv_kernel, n_tiles=n_tiles),
        grid=(B, n_tiles, n_ct),
        in_specs=specs,
        out_specs=[out_spec, out_spec],
        out_shape=[jax.ShapeDtypeStruct((B, FFT_J, D // LANES, L // FFT_N2, SUBLANES, LANES), F32)] * 2,
        compiler_params=_params(("parallel", "parallel", "parallel"), 32),
        name="short_conv_gate",
    )(*args)


def _filter_kernel(feat_ref, tv_ref, w1, b1, w2, b2, w3, b3, fr_ref, w4, delta_ref, o_ref):
    fr = fr_ref[...]
    h = jnp.sin(fr * (_dot3(feat_ref[...], w1[...]) + b1[...]))
    h = jnp.sin(fr * (_dot3(h, w2[...]) + b2[...]))
    h = jnp.sin(fr * (_dot3(h, w3[...]) + b3[...]))
    out = _dot3(h, w4[...])
    t = tv_ref[:, 0:1]
    valid = tv_ref[:, 1:2]
    _store_grouped(o_ref, out * jnp.exp(-t * delta_ref[...]) * valid)


def conv_kernel_signal(L, D, fw1, fb1, fw2, fb2, fw3, fb3, ffreq, fw4, *, tl=256):
    FH = fw2.shape[0]
    FE = 64
    p = jnp.arange(2 * L, dtype=jnp.int32)
    pos = jnp.where(p < L, p, 2 * L - p).astype(F32)
    t = pos / (L - 1)
    bands = jnp.linspace(1e-4, FILTER_BANDS - 1, FILTER_BANDS, dtype=F32)
    ang = (2.0 * math.pi / L) * pos[:, None] * bands[None, :]
    feat = jnp.concatenate(
        [t[:, None], jnp.cos(ang), -jnp.sin(ang), jnp.zeros((2 * L, FE - FILTER_EMB), F32)], axis=-1)
    tv = jnp.stack([t, (p != L).astype(F32)], axis=-1)
    deltas = np.abs(np.linspace(math.log(DECAY_TARGET) / SLOW_DECAY_PCT, math.log(DECAY_TARGET) / FAST_DECAY_PCT, D,
                                dtype=np.float32)).reshape(1, D)
    w1p = jnp.pad(fw1.astype(F32), ((0, FE - FILTER_EMB), (0, 0)))
    tl = _tile(L, tl, FFT_N2)
    half_tiles = L // tl
    full = lambda shape: pl.BlockSpec(shape, lambda i: (0, 0))
    return pl.pallas_call(
        _filter_kernel,
        grid=(2 * L // tl,),
        in_specs=[
            pl.BlockSpec((tl, FE), lambda i: (i, 0)),
            pl.BlockSpec((tl, 2), lambda i: (i, 0)),
            full((FE, FH)), full((1, FH)), full((FH, FH)), full((1, FH)), full((FH, FH)), full((1, FH)),
            full((1, FH)),
            pl.BlockSpec((FH, D), lambda i: (0, i // half_tiles)),
            full((1, D)),
        ],
        out_specs=pl.BlockSpec((FFT_J, D // LANES, tl // FFT_N2, SUBLANES, LANES), lambda i: (0, 0, i, 0, 0)),
        out_shape=jax.ShapeDtypeStruct((FFT_J, D // LANES, 2 * L // FFT_N2, SUBLANES, LANES), F32),
        compiler_params=_params(("parallel",), 32),
        name="hyena_filter",
    )(feat, tv, w1p, fb1.reshape(1, FH).astype(F32), fw2.astype(F32), fb2.reshape(1, FH).astype(F32),
      fw3.astype(F32), fb3.reshape(1, FH).astype(F32), ffreq.reshape(1, FH).astype(F32), fw4.astype(F32),
      jnp.asarray(deltas))


def _hilo(m):
    m = jnp.asarray(np.asarray(m, np.float64), dtype=F32)
    return _split(m)


def _stack(re, im):
    return np.block([[re, -im], [im, re]])


def _dft_tables(n1, n2):
    n, k1 = n1 * n2, n1 // 2
    a1 = 2.0 * np.pi * np.outer(np.arange(n1), np.arange(n1)) / n1
    c1, s1 = np.cos(a1), np.sin(a1)
    a2 = 2.0 * np.pi * np.outer(np.arange(n2), np.arange(n2)) / n2
    c2, s2 = np.cos(a2), np.sin(a2)
    at = 2.0 * np.pi * np.outer(np.arange(n2), np.arange(n1)) / n
    bcast = lambda t: jnp.broadcast_to(jnp.asarray(t, F32)[:, :, None], t.shape + (LANES,))
    return dict(
        f1_real=np.concatenate([c1, -s1], axis=0),
        f1_cplx=_stack(c1[:, :k1], -s1[:, :k1]),
        f2=_stack(c2, -s2),
        g2=_stack(c2, s2),
        g1_cplx=_stack(c1[:k1], s1[:k1]),
        tw1=(bcast(np.cos(at)), bcast(-np.sin(at))),
        tw2=(bcast(np.cos(at.T)), bcast(np.sin(at.T))),
    )


def _fft1_kernel(x_ref, fh_ref, fl_ref, twr_ref, twi_ref, ar_ref, ai_ref, *, n1):
    fh = fh_ref[...]
    fl = fl_ref[...]
    parts, cb, rows = x_ref.shape[0], x_ref.shape[1], x_ref.shape[2] // SUBLANES
    for i in range(SUBLANES):
        sel = pl.ds(i, rows, stride=SUBLANES)
        x = jnp.concatenate([jnp.concatenate([x_ref[p, c, sel, :] for c in range(cb)], axis=1)
                             for p in range(parts)], axis=0)
        r = _dot3c(fh, fl, x)
        re, im = r[:n1], r[n1:]
        twr = jnp.concatenate([twr_ref[i]] * cb, axis=1)
        twi = jnp.concatenate([twi_ref[i]] * cb, axis=1)
        ar = re * twr - im * twi
        ai = re * twi + im * twr
        for c in range(cb):
            ar_ref[c, pl.ds(i, n1, stride=SUBLANES), :] = ar[:, c * LANES:(c + 1) * LANES]
            ai_ref[c, pl.ds(i, n1, stride=SUBLANES), :] = ai[:, c * LANES:(c + 1) * LANES]


def _fft_stage1(x, n1, f1, tw1):
    P, parts, _, C, k8, _ = x.shape
    fh, fl = _hilo(f1)
    cb = min(C, FFT_LANE_TILES)
    const = lambda shape: pl.BlockSpec(shape, lambda j, p, c: (0,) * len(shape))
    out_spec = pl.BlockSpec((None, None, cb, n1 * SUBLANES, LANES), lambda j, p, c: (p, j, c, 0, 0))
    tw_spec = pl.BlockSpec((SUBLANES, n1, LANES), lambda j, p, c: (j, 0, 0))
    return pl.pallas_call(
        functools.partial(_fft1_kernel, n1=n1),
        grid=(FFT_J, P, C // cb),
        in_specs=[
            pl.BlockSpec((None, parts, None, cb, k8, LANES), lambda j, p, c: (p, 0, j, c, 0, 0)),
            const(f1.shape), const(f1.shape), tw_spec, tw_spec,
        ],
        out_specs=[out_spec, out_spec],
        out_shape=[jax.ShapeDtypeStruct((P, FFT_J, C, n1 * SUBLANES, LANES), F32)] * 2,
        compiler_params=_params(("parallel", "parallel", "parallel"), 40),
        name="fft_stage1",
    )(x, fh, fl, tw1[0], tw1[1])


def _fft2_kernel(*refs, n2, cc, scale, spectrum_only):
    if spectrum_only:
        ar_ref, ai_ref, fh_ref, fl_ref, yr_ref, yi_ref = refs
    else:
        ar_ref, ai_ref, kr_ref, ki_ref, fh_ref, fl_ref, gh_ref, gl_ref, twr_ref, twi_ref, yr_ref, yi_ref = refs
        twr = jnp.concatenate([twr_ref[...]] * (cc // LANES), axis=1)
        twi = jnp.concatenate([twi_ref[...]] * (cc // LANES), axis=1)
    fh = fh_ref[...]
    fl = fl_ref[...]
    lt = cc // LANES
    rows = lambda ref, c: jnp.concatenate([ref[:, c * lt + t].reshape(n2, LANES) for t in range(lt)], axis=1)
    for c in range(ar_ref.shape[1] // lt):
        cols = slice(c * cc, (c + 1) * cc)
        y = _dot3c(fh, fl, jnp.concatenate([rows(ar_ref, c), rows(ai_ref, c)], axis=0))
        yr, yi = y[:n2], y[n2:]
        if spectrum_only:
            yr_ref[:, cols] = yr * scale
            yi_ref[:, cols] = yi * scale
        else:
            kr = kr_ref[:, cols]
            ki = ki_ref[:, cols]
            pr = yr * kr - yi * ki
            pi = yr * ki + yi * kr
            q = _dot3c(gh_ref[...], gl_ref[...], jnp.concatenate([pr, pi], axis=0))
            qr, qi = q[:n2], q[n2:]
            outr = qr * twr - qi * twi
            outi = qr * twi + qi * twr
            for t in range(lt):
                yr_ref[:, c * lt + t] = outr[:, t * LANES:(t + 1) * LANES].reshape(FFT_J, SUBLANES, LANES)
                yi_ref[:, c * lt + t] = outi[:, t * LANES:(t + 1) * LANES].reshape(FFT_J, SUBLANES, LANES)


def _fft_stage2(ar, ai, n1, n2, D, f2, *, kf=None, g2=None, tw2=None, scale=1.0):
    B, _, C = ar.shape[:3]
    ar = ar.reshape(B, FFT_J, C, n1, SUBLANES, LANES)
    ai = ai.reshape(B, FFT_J, C, n1, SUBLANES, LANES)
    fh, fl = _hilo(f2)
    cc = _tile(D, 512)
    blk = pl.BlockSpec((None, FFT_J, C, None, SUBLANES, LANES), lambda f, b: (b, 0, 0, f, 0, 0))
    const = lambda shape: pl.BlockSpec(shape, lambda f, b: (0,) * len(shape))
    specs = [blk, blk]
    args = [ar, ai]
    if kf is not None:
        gh, gl = _hilo(g2)
        kblk = pl.BlockSpec((None, n2, D), lambda f, b: (f, 0, 0))
        tblk = pl.BlockSpec((None, n2, LANES), lambda f, b: (f, 0, 0))
        specs += [kblk, kblk, const((2 * n2, 2 * n2)), const((2 * n2, 2 * n2)), const((2 * n2, 2 * n2)),
                  const((2 * n2, 2 * n2)), tblk, tblk]
        args += [kf[0], kf[1], fh, fl, gh, gl, tw2[0], tw2[1]]
        out_spec = blk
        out_shape = jax.ShapeDtypeStruct((B, FFT_J, C, n1, SUBLANES, LANES), F32)
    else:
        specs += [const((2 * n2, 2 * n2)), const((2 * n2, 2 * n2))]
        args += [fh, fl]
        out_spec = pl.BlockSpec((None, None, n2, D), lambda f, b: (b, f, 0, 0))
        out_shape = jax.ShapeDtypeStruct((B, n1, n2, D), F32)
    out = pl.pallas_call(
        functools.partial(_fft2_kernel, n2=n2, cc=cc, scale=scale, spectrum_only=kf is None),
        grid=(n1, B),
        in_specs=specs,
        out_specs=[out_spec, out_spec],
        out_shape=[out_shape] * 2,
        compiler_params=_params(("parallel", "parallel"), 40),
        name="fft_stage2",
    )(*args)
    if kf is not None:
        out = [o.reshape(B, FFT_J, C, n1 * SUBLANES, LANES) for o in out]
    return out


def _fft3_kernel(qr_ref, qi_ref, gh_ref, gl_ref, z_ref, x0_ref, skip_ref, o_ref, *, n1):
    gh = gh_ref[...]
    gl = gl_ref[...]
    cb, rows = z_ref.shape[1], z_ref.shape[2] // SUBLANES
    for i in range(SUBLANES):
        qsel = pl.ds(i, n1, stride=SUBLANES)
        sel = pl.ds(i, rows, stride=SUBLANES)
        q = jnp.concatenate([jnp.concatenate([ref[c, qsel, :] for c in range(cb)], axis=1)
                             for ref in (qr_ref, qi_ref)], axis=0)
        y = _dot3c(gh, gl, q)
        for p in range(2):
            for c in range(cb):
                lanes = slice(c * LANES, (c + 1) * LANES)
                conv = y[p * rows:(p + 1) * rows, lanes]
                o_ref[p, c, sel, :] = (conv + z_ref[p, c, sel, :] * skip_ref[:, lanes]) * x0_ref[p, c, sel, :]


def _fft_stage3(qr, qi, z, x0, skip, n1, g1):
    P, _, _, C, k8, _ = z.shape
    gh, gl = _hilo(g1)
    cb = min(C, FFT_LANE_TILES)
    qblk = pl.BlockSpec((None, None, cb, n1 * SUBLANES, LANES), lambda p, j, c: (p, j, c, 0, 0))
    zblk = pl.BlockSpec((None, 2, None, cb, k8, LANES), lambda p, j, c: (p, 0, j, c, 0, 0))
    const = lambda shape: pl.BlockSpec(shape, lambda p, j, c: (0,) * len(shape))
    return pl.pallas_call(
        functools.partial(_fft3_kernel, n1=n1),
        grid=(P, FFT_J, C // cb),
        in_specs=[qblk, qblk, const(g1.shape), const(g1.shape), zblk, zblk,
                  pl.BlockSpec((1, cb * LANES), lambda p, j, c: (0, c))],
        out_specs=zblk,
        out_shape=jax.ShapeDtypeStruct(z.shape, F32),
        compiler_params=_params(("parallel", "parallel", "parallel"), 52),
        name="fft_stage3",
    )(qr, qi, gh, gl, z, x0, skip.reshape(1, C * LANES).astype(F32))


def long_conv_gate(z, x0, skip, kern):
    B, _, C, k1 = z.shape[:4]
    D = C * LANES
    n2 = FFT_N2
    n1 = 2 * k1
    assert B % 2 == 0
    t = _dft_tables(n1, n2)
    ka = _fft_stage1(kern.reshape(1, 1, FFT_J, C, n1 * SUBLANES, LANES), n1, t["f1_real"], t["tw1"])
    kf = _fft_stage2(ka[0], ka[1], n1, n2, D, t["f2"], scale=1.0 / (n1 * n2))
    kf = (kf[0].reshape(n1, n2, D), kf[1].reshape(n1, n2, D))
    pairs = lambda v: v.reshape(B // 2, 2, FFT_J, C, k1 * SUBLANES, LANES)
    a = _fft_stage1(pairs(z), n1, t["f1_cplx"], t["tw1"])
    q = _fft_stage2(a[0], a[1], n1, n2, D, t["f2"], kf=kf, g2=t["g2"], tw2=t["tw2"])
    return _fft_stage3(q[0], q[1], pairs(z), pairs(x0), skip, n1, t["g1_cplx"]).reshape(z.shape)


def _rope_tables(S):
    inv = ROPE_THETA ** (-jnp.arange(0, HEAD_DIM, 2, dtype=F32) / HEAD_DIM)
    ang = jnp.arange(S, dtype=F32)[:, None] * inv[None, :]
    cos, sin = jnp.cos(ang), jnp.sin(ang)
    cos2, sin2 = jnp.concatenate([cos, cos], axis=-1), jnp.concatenate([-sin, sin], axis=-1)
    return jnp.stack([cos2, jnp.ones_like(cos2)]), jnp.stack([sin2, jnp.zeros_like(sin2)])


def _trunk(x, p):
    B, S, D = x.shape
    H = D // HEAD_DIM
    G = len(GROUP_DILATIONS)
    T = B * S
    cos, sin = _rope_tables(S)
    x = x.reshape(T, D)
    depth = p["mix_norm"].shape[0]
    for i in range(depth):
        j = i // 2
        if i % 2 == 0:
            scale = HEAD_DIM ** -0.5
            gains = jnp.stack([p["attn_q_gain"][j] * scale, p["attn_k_gain"][j], jnp.ones_like(p["attn_k_gain"][j])],
                              axis=1).reshape(G * 3, 1, HEAD_DIM).astype(F32)
            qkv = [qkv_project(x, p["mix_norm"][i], p["attn_w_qkv"][j], gains, cos, sin, S, g) for g in range(G)]
            o = window_attention(qkv, B, S, H)
            x = matmul_resid(o, p["attn_w_out"][j], x)
        else:
            u = norm_matmul(x, p["mix_norm"][i], p["hy_w_in"][j], p["hy_b_in"][j])
            z, x0 = short_conv_gate(u.reshape(B, S, 3 * D), p["hy_conv_w"][j], p["hy_conv_b"][j])
            kern = conv_kernel_signal(S, D, p["hy_filt_w1"][j], p["hy_filt_b1"][j], p["hy_filt_w2"][j],
                                      p["hy_filt_b2"][j], p["hy_filt_w3"][j], p["hy_filt_b3"][j],
                                      p["hy_filt_freq"][j], p["hy_filt_w4"][j])
            y = long_conv_gate(z, x0, p["hy_skip"][j], kern)
            x = matmul_resid(y, p["hy_w_out"][j], x, p["hy_b_out"][j])
        h = norm_matmul(x, p["mlp_norm"][i], p["mlp_w1"][i], relu2=True, out_dtype=BF16)
        x = matmul_resid(h, p["mlp_w2"][i], x)
    return x.reshape(B, S, D)


def kernel(x_prompt, x_sample, mix_norm, mlp_norm, attn_w_qkv, attn_q_gain, attn_k_gain, attn_w_out, hy_w_in, hy_b_in, hy_conv_w, hy_conv_b, hy_filt_w1, hy_filt_b1, hy_filt_w2, hy_filt_b2, hy_filt_w3, hy_filt_b3, hy_filt_freq, hy_filt_w4, hy_skip, hy_w_out, hy_b_out, mlp_w1, mlp_w2):
    p = dict(
        mix_norm=mix_norm, mlp_norm=mlp_norm,
        attn_w_qkv=attn_w_qkv.astype(BF16), attn_q_gain=attn_q_gain, attn_k_gain=attn_k_gain,
        attn_w_out=attn_w_out.astype(BF16),
        hy_w_in=hy_w_in.astype(BF16), hy_b_in=hy_b_in, hy_conv_w=hy_conv_w, hy_conv_b=hy_conv_b,
        hy_filt_w1=hy_filt_w1, hy_filt_b1=hy_filt_b1, hy_filt_w2=hy_filt_w2, hy_filt_b2=hy_filt_b2,
        hy_filt_w3=hy_filt_w3, hy_filt_b3=hy_filt_b3, hy_filt_freq=hy_filt_freq, hy_filt_w4=hy_filt_w4,
        hy_skip=hy_skip, hy_w_out=hy_w_out.astype(BF16), hy_b_out=hy_b_out,
        mlp_w1=mlp_w1.astype(BF16), mlp_w2=mlp_w2.astype(BF16),
    )
    return (_trunk(x_prompt, p), _trunk(x_sample, p))
```

```python
import functools
import math

import numpy as np
import jax
import jax.numpy as jnp
from jax import lax
from jax.experimental import pallas as pl
from jax.experimental.pallas import tpu as pltpu

F32 = jnp.float32
BF16 = jnp.bfloat16

HEAD_DIM = 128
GROUP_WINDOWS = (128, 512, 2048)
GROUP_DILATIONS = (1, 4, 16)
ROPE_THETA = 10000.0
FILTER_EMB = 33
FILTER_BANDS = (FILTER_EMB - 1) // 2
DECAY_TARGET = 1e-2
FAST_DECAY_PCT = 0.3
SLOW_DECAY_PCT = 1.5
EPS = 1e-6

LANES = 128
SUBLANES = 8
ATTN_TILE = 1024
ATTN_BATCH = 8
FFT_N2 = 128
FFT_J = FFT_N2 // SUBLANES
FFT_LANE_TILES = 4
MIB = 1024 * 1024


def _params(semantics, vmem_mib):
    return pltpu.CompilerParams(dimension_semantics=semantics, vmem_limit_bytes=vmem_mib * MIB)


def _tile(n, pref, quantum=LANES):
    if n <= pref:
        return n
    t = (pref // quantum) * quantum
    while t > quantum and n % t:
        t -= quantum
    assert n % t == 0, (n, pref)
    return t


def _split(x):
    hi = x.astype(BF16)
    lo = (x - hi.astype(F32)).astype(BF16)
    return hi, lo


def _dot(a, b):
    return jnp.dot(a, b, preferred_element_type=F32)


def _dot3(a, b):
    ah, al = _split(a)
    bh, bl = _split(b)
    return _dot(ah, bh) + (_dot(ah, bl) + _dot(al, bh))


def _dotc(c, b):
    return _dot(c, b.astype(BF16))


def _rms_rows_to(x_ref, g_ref, xn_ref, row_chunk):
    def body(c, carry):
        rows = pl.ds(pl.multiple_of(c * row_chunk, row_chunk), row_chunk)
        x = x_ref[rows, :]
        inv = lax.rsqrt(jnp.mean(x * x, axis=-1, keepdims=True) + EPS)
        xn_ref[rows, :] = (x * inv * g_ref[...]).astype(BF16)
        return carry

    lax.fori_loop(0, x_ref.shape[0] // row_chunk, body, 0)


def _store_grouped(o_ref, v):
    for c in range(o_ref.shape[1]):
        for a in range(o_ref.shape[2]):
            o_ref[:, c, a] = v[a * FFT_N2:(a + 1) * FFT_N2, c * LANES:(c + 1) * LANES].reshape(FFT_J, SUBLANES, LANES)


def _load_grouped(a_ref):
    rows = [jnp.concatenate([a_ref[:, c, s].reshape(FFT_N2, LANES) for c in range(a_ref.shape[1])], axis=1)
            for s in range(a_ref.shape[2])]
    return rows[0] if len(rows) == 1 else jnp.concatenate(rows, axis=0)


def _norm_mm_kernel(x_ref, g_ref, w_ref, *rest, has_bias, relu2, row_chunk):
    if has_bias:
        b_ref, o_ref, xn_ref = rest
    else:
        o_ref, xn_ref = rest

    @pl.when(pl.program_id(1) == 0)
    def _():
        _rms_rows_to(x_ref, g_ref, xn_ref, row_chunk)

    acc = _dot(xn_ref[...], w_ref[...])
    if has_bias:
        acc = acc + b_ref[...]
    if relu2:
        acc = jnp.maximum(acc, 0.0)
        acc = acc * acc
    o_ref[...] = acc.astype(o_ref.dtype)


def norm_matmul(x, g, w, bias=None, *, relu2=False, out_dtype=F32, tm=1024, tn=1024):
    T, K = x.shape
    N = w.shape[1]
    tm = _tile(T, tm)
    tn = _tile(N, tn)
    in_specs = [
        pl.BlockSpec((tm, K), lambda i, j: (i, 0)),
        pl.BlockSpec((1, K), lambda i, j: (0, 0)),
        pl.BlockSpec((K, tn), lambda i, j: (0, j)),
    ]
    args = [x, g.reshape(1, K), w]
    if bias is not None:
        in_specs.append(pl.BlockSpec((1, tn), lambda i, j: (0, j)))
        args.append(bias.reshape(1, N))
    return pl.pallas_call(
        functools.partial(_norm_mm_kernel, has_bias=bias is not None, relu2=relu2, row_chunk=min(tm, 128)),
        grid=(T // tm, N // tn),
        in_specs=in_specs,
        out_specs=pl.BlockSpec((tm, tn), lambda i, j: (i, j)),
        out_shape=jax.ShapeDtypeStruct((T, N), out_dtype),
        scratch_shapes=[pltpu.VMEM((tm, K), BF16)],
        compiler_params=_params(("parallel", "arbitrary"), 56),
        name="norm_matmul",
    )(*args)


def _qkv_kernel(x_ref, g_ref, w_ref, gain_ref, cos_ref, sin_ref, o_ref, xn_ref, acc_a, acc_b, stage_ref, *,
                row_chunk, tiles_per_part, n_j, n_tiles, d):
    n = pl.program_id(0)
    tile = jnp.minimum(n, n_tiles - 1)
    done = jnp.maximum(n - 1, 0)
    tm = xn_ref.shape[0]

    @pl.when(n == 0)
    def _():
        acc_b[...] = jnp.zeros_like(acc_b)

    @pl.when((tile % n_j == 0) & (n < n_tiles))
    def _():
        _rms_rows_to(x_ref, g_ref, xn_ref, row_chunk)

    def step(acc_w, acc_r):
        acc_w[...] = _dot(xn_ref[...], w_ref[...])
        is_v = (done % n_j) // tiles_per_part == 2
        gain = gain_ref[...]
        cos = cos_ref[...]
        sin = sin_ref[...]
        for s in range(acc_r.shape[1] // HEAD_DIM):
            a = acc_r[:, s * HEAD_DIM:(s + 1) * HEAD_DIM]
            inv = jnp.where(is_v, 1.0, lax.rsqrt(jnp.mean(a * a, axis=-1, keepdims=True) + EPS))
            y = a * inv * gain
            y = y * cos + pltpu.roll(y, HEAD_DIM // 2, axis=1) * sin
            if d == 1:
                o_ref[s] = y.astype(o_ref.dtype)
            else:
                stage_ref[s] = y
                n_r = tm // d
                for r in range(d):
                    o_ref[s, r * n_r:(r + 1) * n_r, :] = stage_ref[s, pl.ds(r, n_r, stride=d), :].astype(o_ref.dtype)

    @pl.when(n % 2 == 0)
    def _():
        step(acc_a, acc_b)

    @pl.when(n % 2 == 1)
    def _():
        step(acc_b, acc_a)


def qkv_project(x, g, w, gains, cos, sin, seq_len, group, *, tn=1024):
    T, K = x.shape
    part_width = w.shape[1] // (len(GROUP_DILATIONS) * 3)
    tm = ATTN_TILE
    assert seq_len % tm == 0
    tn = _tile(part_width, tn)
    pos_tiles = seq_len // tm
    tiles_per_part = part_width // tn
    n_j = 3 * tiles_per_part
    n_tiles = (T // tm) * n_j
    mm = lambda n: jnp.minimum(n, n_tiles - 1)
    ep = lambda n: jnp.maximum(n - 1, 0)
    table = lambda n: ((ep(n) % n_j) // tiles_per_part // 2, (ep(n) // n_j) % pos_tiles, 0)
    return pl.pallas_call(
        functools.partial(_qkv_kernel, row_chunk=min(tm, 128), tiles_per_part=tiles_per_part, n_j=n_j,
                          n_tiles=n_tiles, d=GROUP_DILATIONS[group]),
        grid=(n_tiles + 1,),
        in_specs=[
            pl.BlockSpec((tm, K), lambda n: (mm(n) // n_j, 0)),
            pl.BlockSpec((1, K), lambda n: (0, 0)),
            pl.BlockSpec((K, tn), lambda n: (0, group * n_j + mm(n) % n_j)),
            pl.BlockSpec((None, 1, HEAD_DIM), lambda n: (group * 3 + (ep(n) % n_j) // tiles_per_part, 0, 0)),
            pl.BlockSpec((None, tm, HEAD_DIM), table),
            pl.BlockSpec((None, tm, HEAD_DIM), table),
        ],
        out_specs=pl.BlockSpec(
            (tn // HEAD_DIM, None, tm, HEAD_DIM),
            lambda n: (ep(n) % tiles_per_part, ((ep(n) % n_j) // tiles_per_part + 2) % 3, ep(n) // n_j, 0)),
        out_shape=jax.ShapeDtypeStruct((part_width // HEAD_DIM, 3, T, HEAD_DIM), BF16),
        scratch_shapes=[pltpu.VMEM((tm, K), BF16), pltpu.VMEM((tm, tn), F32), pltpu.VMEM((tm, tn), F32),
                        pltpu.VMEM((tn // HEAD_DIM, tm, HEAD_DIM), F32)],
        compiler_params=_params(("arbitrary",), 52),
        name="qkv_project",
    )(x, g.reshape(1, K), w, gains, cos, sin)


def _mm_resid_kernel(a_ref, w_ref, *rest, has_bias, grouped):
    if has_bias:
        b_ref, r_ref, o_ref = rest
    else:
        r_ref, o_ref = rest
    k = pl.program_id(2)
    a = _load_grouped(a_ref) if grouped else a_ref[...]
    part = _dot(a.astype(BF16), w_ref[...])

    @pl.when(k == 0)
    def _():
        first = part + r_ref[...]
        if has_bias:
            first = first + b_ref[...]
        o_ref[...] = first

    @pl.when(k > 0)
    def _():
        o_ref[...] += part


def matmul_resid(a, w, resid, bias=None, *, tm=1024, tn=1024, tk=2048):
    grouped = a.ndim == 6
    T, N = resid.shape
    K = w.shape[0]
    tn = _tile(N, tn)
    tk = _tile(K, tk)
    if grouped:
        seq_len = a.shape[3] * FFT_N2
        tm = _tile(seq_len, tm)
        seq_tiles = seq_len // tm
        a_spec = pl.BlockSpec((None, FFT_J, tk // LANES, tm // FFT_N2, SUBLANES, LANES),
                              lambda i, j, k: (i // seq_tiles, 0, k, i % seq_tiles, 0, 0))
    else:
        tm = _tile(T, tm)
        a_spec = pl.BlockSpec((tm, tk), lambda i, j, k: (i, k))
    in_specs = [
        a_spec,
        pl.BlockSpec((tk, tn), lambda i, j, k: (k, j)),
    ]
    args = [a, w]
    if bias is not None:
        in_specs.append(pl.BlockSpec((1, tn), lambda i, j, k: (0, j)))
        args.append(bias.reshape(1, N))
    in_specs.append(pl.BlockSpec((tm, tn), lambda i, j, k: (i, j)))
    args.append(resid)
    return pl.pallas_call(
        functools.partial(_mm_resid_kernel, has_bias=bias is not None, grouped=grouped),
        grid=(T // tm, N // tn, K // tk),
        in_specs=in_specs,
        out_specs=pl.BlockSpec((tm, tn), lambda i, j, k: (i, j)),
        out_shape=jax.ShapeDtypeStruct((T, N), F32),
        compiler_params=_params(("parallel", "parallel", "arbitrary"), 48),
        name="matmul_resid",
    )(*args)


def _attn_kernel(*refs, tq, dils, half, n_tiles):
    G = len(dils)
    cur_refs, prev_refs, next_refs = refs[0:G], refs[G:2 * G], refs[2 * G:3 * G]
    o_ref = refs[3 * G]
    acc_ref, m_ref, l_ref, out_ref, bias_ref = refs[3 * G + 1:]
    i = pl.program_id(2)
    has_prev = i > 0
    has_next = i < n_tiles - 1

    for g, d in enumerate(dils):
        n_r = tq // d
        qs = min(128, n_r)
        ks = qs + 2 * half
        n_sub = n_r // qs
        qq = lax.broadcasted_iota(jnp.int32, (qs, ks), 0)
        kk = lax.broadcasted_iota(jnp.int32, (qs, ks), 1)
        band = (kk >= qq) & (kk <= qq + 2 * half)
        prev_ok = (kk >= half) | has_prev
        next_ok = (kk < ks - half) | has_next
        for idx, valid in enumerate((band & prev_ok, band, band & next_ok, band & prev_ok & next_ok)):
            bias_ref[idx, 0:qs, 0:ks] = jnp.where(valid, 0.0, -jnp.inf)

        def window(c, r, u, g=g, n_r=n_r, qs=qs):
            lo, hi = u * qs - half, (u + 1) * qs + half
            pieces = [prev_refs[g][c, r]] if lo < 0 else []
            pieces.append(cur_refs[g][c, r, max(lo, 0):min(hi, n_r), :])
            if hi > n_r:
                pieces.append(next_refs[g][c, r])
            return pieces[0] if len(pieces) == 1 else jnp.concatenate(pieces, axis=0)

        blocks = [(r, u) for r in range(d) for u in range(n_sub)]
        for b0 in range(0, len(blocks), ATTN_BATCH):
            batch = blocks[b0:b0 + ATTN_BATCH]
            edge = lambda u: (3 if n_sub == 1 else 0) if u == 0 else (2 if u == n_sub - 1 else 1)
            q = jnp.stack([cur_refs[g][2, r, u * qs:(u + 1) * qs, :] for r, u in batch])
            kw = jnp.stack([window(0, r, u) for r, u in batch])
            vw = jnp.stack([window(1, r, u) for r, u in batch])
            bias = jnp.stack([bias_ref[edge(u), 0:qs, 0:ks] for _, u in batch])
            s = jnp.einsum("bqd,bkd->bqk", q, kw, preferred_element_type=F32) + bias
            m_blk = jnp.max(s, axis=-1, keepdims=True)
            p = jnp.exp(s - m_blk)
            l_blk = jnp.sum(p, axis=-1, keepdims=True)
            pv = jnp.einsum("bqk,bkd->bqd", p.astype(BF16), vw, preferred_element_type=F32)
            for b, (r, u) in enumerate(batch):
                rows = pl.ds(r * n_r + u * qs, qs)
                m_ref[g, rows, :] = m_blk[b]
                l_ref[g, rows, :] = l_blk[b]
                acc_ref[g, rows, :] = pv[b]

    dmax = max(dils)
    n_max = tq // dmax
    for r in range(dmax):
        sel = [pl.ds((r % d) * (tq // d) + r // d, n_max, stride=dmax // d) if d < dmax else pl.ds(r * n_max, n_max)
               for d in dils]
        ms = [m_ref[g, sel[g], :] for g in range(G)]
        m = functools.reduce(jnp.maximum, ms)
        ws = [jnp.exp(mg - m) for mg in ms]
        num = sum(ws[g] * acc_ref[g, sel[g], :] for g in range(G))
        den = sum(ws[g] * l_ref[g, sel[g], :] for g in range(G))
        out_ref[r * n_max:(r + 1) * n_max, :] = num / den
    o_ref[...] = jnp.concatenate([out_ref[pl.ds(j, dmax, stride=n_max), :] for j in range(n_max)],
                                 axis=0).astype(o_ref.dtype)


def window_attention(qkv, batch, seq_len, n_heads):
    G = len(GROUP_DILATIONS)
    tq = ATTN_TILE
    assert seq_len % tq == 0
    n_tiles = seq_len // tq
    total_tiles = batch * n_tiles
    half = GROUP_WINDOWS[0] // (2 * GROUP_DILATIONS[0])
    for wdw, d in zip(GROUP_WINDOWS, GROUP_DILATIONS):
        assert wdw // (2 * d) == half and half * d <= tq and tq % (d * min(128, tq // d)) == 0
    H = n_heads
    dils = GROUP_DILATIONS

    cur, prev, nxt = [], [], []
    for d in dils:
        cur.append(pl.BlockSpec((None, 3, None, d, tq // d, HEAD_DIM),
                                lambda b, h, i: (h, 0, b * n_tiles + i, 0, 0, 0)))
        prev.append(pl.BlockSpec((None, 2, None, d, half, HEAD_DIM),
                                 lambda b, h, i, d=d: (h, 0, jnp.maximum(b * n_tiles + i - 1, 0), 0,
                                                       tq // d // half - 1, 0)))
        nxt.append(pl.BlockSpec((None, 2, None, d, half, HEAD_DIM),
                                lambda b, h, i: (h, 0, jnp.minimum(b * n_tiles + i + 1, total_tiles - 1), 0, 0, 0)))
    views = [t.reshape(H, 3, total_tiles, d, tq // d, HEAD_DIM) for t, d in zip(qkv, dils)]
    in_specs = cur + prev + nxt
    return pl.pallas_call(
        functools.partial(_attn_kernel, tq=tq, dils=dils, half=half, n_tiles=n_tiles),
        grid=(batch, H, n_tiles),
        in_specs=in_specs,
        out_specs=pl.BlockSpec((tq, HEAD_DIM), lambda b, h, i: (b * n_tiles + i, h)),
        out_shape=jax.ShapeDtypeStruct((batch * seq_len, H * HEAD_DIM), BF16),
        scratch_shapes=[
            pltpu.VMEM((G, tq, HEAD_DIM), F32),
            pltpu.VMEM((G, tq, 1), F32),
            pltpu.VMEM((G, tq, 1), F32),
            pltpu.VMEM((tq, HEAD_DIM), F32),
            pltpu.VMEM((4, 128, 128 + 2 * half), F32),
        ],
        compiler_params=_params(("parallel", "parallel", "parallel"), 40),
        name="window_attention",
    )(*(views * 3))


def _sconv_kernel(*refs, n_tiles):
    (x0m, x0p, x0n, x1m, x1p, x1n, vm, vp, vn, w0, w1, w2, b0, b1, b2, z_ref, x0_ref) = refs
    i = pl.program_id(1)

    def conv(m_ref, p_ref, n_ref, w_ref, b_ref):
        x = m_ref[...]
        ts = x.shape[0]
        prev_row = jnp.where(i > 0, p_ref[7:8, :], 0.0)
        next_row = jnp.where(i < n_tiles - 1, n_ref[0:1, :], 0.0)
        row = lax.broadcasted_iota(jnp.int32, x.shape, 0)
        up = jnp.where(row == 0, prev_row, pltpu.roll(x, 1, axis=0))
        dn = jnp.where(row == ts - 1, next_row, pltpu.roll(x, ts - 1, axis=0))
        return up * w_ref[0:1, :] + x * w_ref[1:2, :] + dn * w_ref[2:3, :] + b_ref[...]

    _store_grouped(x0_ref, conv(x0m, x0p, x0n, w0, b0))
    _store_grouped(z_ref, conv(vm, vp, vn, w2, b2) * conv(x1m, x1p, x1n, w1, b1))


def short_conv_gate(u, conv_w, conv_b, *, ts=512, ct=512):
    B, L, D3 = u.shape
    D = D3 // 3
    ts = _tile(L, ts, FFT_N2)
    ct = _tile(D, ct)
    n_tiles = L // ts
    n_ct = D // ct
    rows8 = ts // 8
    last8 = L // 8 - 1
    specs, args = [], []
    for part in range(3):
        off = part * n_ct
        specs += [
            pl.BlockSpec((None, ts, ct), lambda b, i, j, off=off: (b, i, off + j)),
            pl.BlockSpec((None, 8, ct), lambda b, i, j, off=off: (b, jnp.maximum(i * rows8 - 1, 0), off + j)),
            pl.BlockSpec((None, 8, ct), lambda b, i, j, off=off: (b, jnp.minimum((i + 1) * rows8, last8), off + j)),
        ]
        args += [u, u, u]
    for part in range(3):
        specs.append(pl.BlockSpec((3, ct), lambda b, i, j, off=part * n_ct: (0, off + j)))
        args.append(conv_w)
    for part in range(3):
        specs.append(pl.BlockSpec((1, ct), lambda b, i, j, off=part * n_ct: (0, off + j)))
        args.append(conv_b.reshape(1, D3))
    out_spec = pl.BlockSpec((None, FFT_J, ct // LANES, ts // FFT_N2, SUBLANES, LANES),
                            lambda b, i, j: (b, 0, j, i, 0, 0))
    return pl.pallas_call(
        functools.partial(_sconv_kernel, n_tiles=n_tiles),
        grid=(B, n_tiles, n_ct),
        in_specs=specs,
        out_specs=[out_spec, out_spec],
        out_shape=[jax.ShapeDtypeStruct((B, FFT_J, D // LANES, L // FFT_N2, SUBLANES, LANES), F32)] * 2,
        compiler_params=_params(("parallel", "parallel", "parallel"), 32),
        name="short_conv_gate",
    )(*args)


def _filter_kernel(feat_ref, tv_ref, w1, b1, w2, b2, w3, b3, fr_ref, w4, delta_ref, o_ref):
    fr = fr_ref[...]
    h = jnp.sin(fr * (_dot3(feat_ref[...], w1[...]) + b1[...]))
    h = jnp.sin(fr * (_dot3(h, w2[...]) + b2[...]))
    h = jnp.sin(fr * (_dot3(h, w3[...]) + b3[...]))
    out = _dot3(h, w4[...])
    t = tv_ref[:, 0:1]
    valid = tv_ref[:, 1:2]
    _store_grouped(o_ref, out * jnp.exp(-t * delta_ref[...]) * valid)


def conv_kernel_signal(L, D, fw1, fb1, fw2, fb2, fw3, fb3, ffreq, fw4, *, tl=256):
    FH = fw2.shape[0]
    FE = 64
    p = jnp.arange(2 * L, dtype=jnp.int32)
    pos = jnp.where(p < L, p, 2 * L - p).astype(F32)
    t = pos / (L - 1)
    bands = jnp.linspace(1e-4, FILTER_BANDS - 1, FILTER_BANDS, dtype=F32)
    ang = (2.0 * math.pi / L) * pos[:, None] * bands[None, :]
    feat = jnp.concatenate(
        [t[:, None], jnp.cos(ang), -jnp.sin(ang), jnp.zeros((2 * L, FE - FILTER_EMB), F32)], axis=-1)
    tv = jnp.stack([t, (p != L).astype(F32)], axis=-1)
    deltas = np.abs(np.linspace(math.log(DECAY_TARGET) / SLOW_DECAY_PCT, math.log(DECAY_TARGET) / FAST_DECAY_PCT, D,
                                dtype=np.float32)).reshape(1, D)
    w1p = jnp.pad(fw1.astype(F32), ((0, FE - FILTER_EMB), (0, 0)))
    tl = _tile(L, tl, FFT_N2)
    half_tiles = L // tl
    full = lambda shape: pl.BlockSpec(shape, lambda i: (0, 0))
    return pl.pallas_call(
        _filter_kernel,
        grid=(2 * L // tl,),
        in_specs=[
            pl.BlockSpec((tl, FE), lambda i: (i, 0)),
            pl.BlockSpec((tl, 2), lambda i: (i, 0)),
            full((FE, FH)), full((1, FH)), full((FH, FH)), full((1, FH)), full((FH, FH)), full((1, FH)),
            full((1, FH)),
            pl.BlockSpec((FH, D), lambda i: (0, i // half_tiles)),
            full((1, D)),
        ],
        out_specs=pl.BlockSpec((FFT_J, D // LANES, tl // FFT_N2, SUBLANES, LANES), lambda i: (0, 0, i, 0, 0)),
        out_shape=jax.ShapeDtypeStruct((FFT_J, D // LANES, 2 * L // FFT_N2, SUBLANES, LANES), F32),
        compiler_params=_params(("parallel",), 32),
        name="hyena_filter",
    )(feat, tv, w1p, fb1.reshape(1, FH).astype(F32), fw2.astype(F32), fb2.reshape(1, FH).astype(F32),
      fw3.astype(F32), fb3.reshape(1, FH).astype(F32), ffreq.reshape(1, FH).astype(F32), fw4.astype(F32),
      jnp.asarray(deltas))


def _const(m):
    return jnp.asarray(np.asarray(m, np.float64), dtype=F32).astype(BF16)


def _stack(re, im):
    return np.block([[re, -im], [im, re]])


def _dft_tables(n1, n2):
    n, k1 = n1 * n2, n1 // 2
    a1 = 2.0 * np.pi * np.outer(np.arange(n1), np.arange(n1)) / n1
    c1, s1 = np.cos(a1), np.sin(a1)
    a2 = 2.0 * np.pi * np.outer(np.arange(n2), np.arange(n2)) / n2
    c2, s2 = np.cos(a2), np.sin(a2)
    at = 2.0 * np.pi * np.outer(np.arange(n2), np.arange(n1)) / n
    bcast = lambda t: jnp.broadcast_to(jnp.asarray(t, F32)[:, :, None], t.shape + (LANES,))
    return dict(
        f1_real=np.concatenate([c1, -s1], axis=0),
        f1_cplx=_stack(c1[:, :k1], -s1[:, :k1]),
        f2=_stack(c2, -s2),
        g2=_stack(c2, s2),
        g1_cplx=_stack(c1[:k1], s1[:k1]),
        tw1=(bcast(np.cos(at)), bcast(-np.sin(at))),
        tw2=(bcast(np.cos(at.T)), bcast(np.sin(at.T))),
    )


def _fft1_kernel(x_ref, f_ref, twr_ref, twi_ref, ar_ref, ai_ref, *, n1):
    f = f_ref[...]
    parts, cb, rows = x_ref.shape[0], x_ref.shape[1], x_ref.shape[2] // SUBLANES
    for i in range(SUBLANES):
        sel = pl.ds(i, rows, stride=SUBLANES)
        x = jnp.concatenate([jnp.concatenate([x_ref[p, c, sel, :] for c in range(cb)], axis=1)
                             for p in range(parts)], axis=0)
        r = _dotc(f, x)
        re, im = r[:n1], r[n1:]
        twr = jnp.concatenate([twr_ref[i]] * cb, axis=1)
        twi = jnp.concatenate([twi_ref[i]] * cb, axis=1)
        ar = re * twr - im * twi
        ai = re * twi + im * twr
        for c in range(cb):
            ar_ref[c, pl.ds(i, n1, stride=SUBLANES), :] = ar[:, c * LANES:(c + 1) * LANES]
            ai_ref[c, pl.ds(i, n1, stride=SUBLANES), :] = ai[:, c * LANES:(c + 1) * LANES]


def _fft_stage1(x, n1, f1, tw1):
    P, parts, _, C, k8, _ = x.shape
    fc = _const(f1)
    cb = min(C, FFT_LANE_TILES)
    const = lambda shape: pl.BlockSpec(shape, lambda j, p, c: (0,) * len(shape))
    out_spec = pl.BlockSpec((None, None, cb, n1 * SUBLANES, LANES), lambda j, p, c: (p, j, c, 0, 0))
    tw_spec = pl.BlockSpec((SUBLANES, n1, LANES), lambda j, p, c: (j, 0, 0))
    return pl.pallas_call(
        functools.partial(_fft1_kernel, n1=n1),
        grid=(FFT_J, P, C // cb),
        in_specs=[
            pl.BlockSpec((None, parts, None, cb, k8, LANES), lambda j, p, c: (p, 0, j, c, 0, 0)),
            const(f1.shape), tw_spec, tw_spec,
        ],
        out_specs=[out_spec, out_spec],
        out_shape=[jax.ShapeDtypeStruct((P, FFT_J, C, n1 * SUBLANES, LANES), F32)] * 2,
        compiler_params=_params(("parallel", "parallel", "parallel"), 40),
        name="fft_stage1",
    )(x, fc, tw1[0], tw1[1])


def _fft2_kernel(*refs, n2, cc, scale, spectrum_only):
    if spectrum_only:
        ar_ref, ai_ref, f_ref, yr_ref, yi_ref = refs
    else:
        ar_ref, ai_ref, kr_ref, ki_ref, f_ref, g_ref, twr_ref, twi_ref, yr_ref, yi_ref = refs
        twr = jnp.concatenate([twr_ref[...]] * (cc // LANES), axis=1)
        twi = jnp.concatenate([twi_ref[...]] * (cc // LANES), axis=1)
    f = f_ref[...]
    lt = cc // LANES
    rows = lambda ref, c: jnp.concatenate([ref[:, c * lt + t].reshape(n2, LANES) for t in range(lt)], axis=1)
    for c in range(ar_ref.shape[1] // lt):
        cols = slice(c * cc, (c + 1) * cc)
        y = _dotc(f, jnp.concatenate([rows(ar_ref, c), rows(ai_ref, c)], axis=0))
        yr, yi = y[:n2], y[n2:]
        if spectrum_only:
            yr_ref[:, cols] = yr * scale
            yi_ref[:, cols] = yi * scale
        else:
            kr = kr_ref[:, cols]
            ki = ki_ref[:, cols]
            pr = yr * kr - yi * ki
            pi = yr * ki + yi * kr
            q = _dotc(g_ref[...], jnp.concatenate([pr, pi], axis=0))
            qr, qi = q[:n2], q[n2:]
            outr = qr * twr - qi * twi
            outi = qr * twi + qi * twr
            for t in range(lt):
                yr_ref[:, c * lt + t] = outr[:, t * LANES:(t + 1) * LANES].reshape(FFT_J, SUBLANES, LANES)
                yi_ref[:, c * lt + t] = outi[:, t * LANES:(t + 1) * LANES].reshape(FFT_J, SUBLANES, LANES)


def _fft_stage2(ar, ai, n1, n2, D, f2, *, kf=None, g2=None, tw2=None, scale=1.0):
    B, _, C = ar.shape[:3]
    ar = ar.reshape(B, FFT_J, C, n1, SUBLANES, LANES)
    ai = ai.reshape(B, FFT_J, C, n1, SUBLANES, LANES)
    fc = _const(f2)
    cc = _tile(D, 512)
    blk = pl.BlockSpec((None, FFT_J, C, None, SUBLANES, LANES), lambda f, b: (b, 0, 0, f, 0, 0))
    const = lambda shape: pl.BlockSpec(shape, lambda f, b: (0,) * len(shape))
    specs = [blk, blk]
    args = [ar, ai]
    if kf is not None:
        gc = _const(g2)
        kblk = pl.BlockSpec((None, n2, D), lambda f, b: (f, 0, 0))
        tblk = pl.BlockSpec((None, n2, LANES), lambda f, b: (f, 0, 0))
        specs += [kblk, kblk, const((2 * n2, 2 * n2)), const((2 * n2, 2 * n2)), tblk, tblk]
        args += [kf[0], kf[1], fc, gc, tw2[0], tw2[1]]
        out_spec = blk
        out_shape = jax.ShapeDtypeStruct((B, FFT_J, C, n1, SUBLANES, LANES), F32)
    else:
        specs += [const((2 * n2, 2 * n2))]
        args += [fc]
        out_spec = pl.BlockSpec((None, None, n2, D), lambda f, b: (b, f, 0, 0))
        out_shape = jax.ShapeDtypeStruct((B, n1, n2, D), F32)
    out = pl.pallas_call(
        functools.partial(_fft2_kernel, n2=n2, cc=cc, scale=scale, spectrum_only=kf is None),
        grid=(n1, B),
        in_specs=specs,
        out_specs=[out_spec, out_spec],
        out_shape=[out_shape] * 2,
        compiler_params=_params(("parallel", "parallel"), 40),
        name="fft_stage2",
    )(*args)
    if kf is not None:
        out = [o.reshape(B, FFT_J, C, n1 * SUBLANES, LANES) for o in out]
    return out


def _fft3_kernel(qr_ref, qi_ref, g_ref, z_ref, x0_ref, skip_ref, o_ref, *, n1):
    g = g_ref[...]
    cb, rows = z_ref.shape[1], z_ref.shape[2] // SUBLANES
    for i in range(SUBLANES):
        qsel = pl.ds(i, n1, stride=SUBLANES)
        sel = pl.ds(i, rows, stride=SUBLANES)
        q = jnp.concatenate([jnp.concatenate([ref[c, qsel, :] for c in range(cb)], axis=1)
                             for ref in (qr_ref, qi_ref)], axis=0)
        y = _dotc(g, q)
        for p in range(2):
            for c in range(cb):
                lanes = slice(c * LANES, (c + 1) * LANES)
                conv = y[p * rows:(p + 1) * rows, lanes]
                o_ref[p, c, sel, :] = (conv + z_ref[p, c, sel, :] * skip_ref[:, lanes]) * x0_ref[p, c, sel, :]


def _fft_stage3(qr, qi, z, x0, skip, n1, g1):
    P, _, _, C, k8, _ = z.shape
    gc = _const(g1)
    cb = min(C, FFT_LANE_TILES)
    qblk = pl.BlockSpec((None, None, cb, n1 * SUBLANES, LANES), lambda p, j, c: (p, j, c, 0, 0))
    zblk = pl.BlockSpec((None, 2, None, cb, k8, LANES), lambda p, j, c: (p, 0, j, c, 0, 0))
    const = lambda shape: pl.BlockSpec(shape, lambda p, j, c: (0,) * len(shape))
    return pl.pallas_call(
        functools.partial(_fft3_kernel, n1=n1),
        grid=(P, FFT_J, C // cb),
        in_specs=[qblk, qblk, const(g1.shape), zblk, zblk,
                  pl.BlockSpec((1, cb * LANES), lambda p, j, c: (0, c))],
        out_specs=zblk,
        out_shape=jax.ShapeDtypeStruct(z.shape, F32),
        compiler_params=_params(("parallel", "parallel", "parallel"), 52),
        name="fft_stage3",
    )(qr, qi, gc, z, x0, skip.reshape(1, C * LANES).astype(F32))


def long_conv_gate(z, x0, skip, kern):
    B, _, C, k1 = z.shape[:4]
    D = C * LANES
    n2 = FFT_N2
    n1 = 2 * k1
    assert B % 2 == 0
    t = _dft_tables(n1, n2)
    ka = _fft_stage1(kern.reshape(1, 1, FFT_J, C, n1 * SUBLANES, LANES), n1, t["f1_real"], t["tw1"])
    kf = _fft_stage2(ka[0], ka[1], n1, n2, D, t["f2"], scale=1.0 / (n1 * n2))
    kf = (kf[0].reshape(n1, n2, D), kf[1].reshape(n1, n2, D))
    pairs = lambda v: v.reshape(B // 2, 2, FFT_J, C, k1 * SUBLANES, LANES)
    a = _fft_stage1(pairs(z), n1, t["f1_cplx"], t["tw1"])
    q = _fft_stage2(a[0], a[1], n1, n2, D, t["f2"], kf=kf, g2=t["g2"], tw2=t["tw2"])
    return _fft_stage3(q[0], q[1], pairs(z), pairs(x0), skip, n1, t["g1_cplx"]).reshape(z.shape)


def _rope_tables(S):
    inv = ROPE_THETA ** (-jnp.arange(0, HEAD_DIM, 2, dtype=F32) / HEAD_DIM)
    ang = jnp.arange(S, dtype=F32)[:, None] * inv[None, :]
    cos, sin = jnp.cos(ang), jnp.sin(ang)
    cos2, sin2 = jnp.concatenate([cos, cos], axis=-1), jnp.concatenate([-sin, sin], axis=-1)
    return jnp.stack([cos2, jnp.ones_like(cos2)]), jnp.stack([sin2, jnp.zeros_like(sin2)])


def _trunk(x, p):
    B, S, D = x.shape
    H = D // HEAD_DIM
    G = len(GROUP_DILATIONS)
    T = B * S
    cos, sin = _rope_tables(S)
    x = x.reshape(T, D)
    depth = p["mix_norm"].shape[0]
    for i in range(depth):
        j = i // 2
        if i % 2 == 0:
            scale = HEAD_DIM ** -0.5
            gains = jnp.stack([p["attn_q_gain"][j] * scale, p["attn_k_gain"][j], jnp.ones_like(p["attn_k_gain"][j])],
                              axis=1).reshape(G * 3, 1, HEAD_DIM).astype(F32)
            qkv = [qkv_project(x, p["mix_norm"][i], p["attn_w_qkv"][j], gains, cos, sin, S, g) for g in range(G)]
            o = window_attention(qkv, B, S, H)
            x = matmul_resid(o, p["attn_w_out"][j], x)
        else:
            u = norm_matmul(x, p["mix_norm"][i], p["hy_w_in"][j], p["hy_b_in"][j])
            z, x0 = short_conv_gate(u.reshape(B, S, 3 * D), p["hy_conv_w"][j], p["hy_conv_b"][j])
            kern = conv_kernel_signal(S, D, p["hy_filt_w1"][j], p["hy_filt_b1"][j], p["hy_filt_w2"][j],
                                      p["hy_filt_b2"][j], p["hy_filt_w3"][j], p["hy_filt_b3"][j],
                                      p["hy_filt_freq"][j], p["hy_filt_w4"][j])
            y = long_conv_gate(z, x0, p["hy_skip"][j], kern)
            x = matmul_resid(y, p["hy_w_out"][j], x, p["hy_b_out"][j])
        h = norm_matmul(x, p["mlp_norm"][i], p["mlp_w1"][i], relu2=True, out_dtype=BF16, tn=2048)
        x = matmul_resid(h, p["mlp_w2"][i], x)
    return x.reshape(B, S, D)


def kernel(x_prompt, x_sample, mix_norm, mlp_norm, attn_w_qkv, attn_q_gain, attn_k_gain, attn_w_out, hy_w_in, hy_b_in, hy_conv_w, hy_conv_b, hy_filt_w1, hy_filt_b1, hy_filt_w2, hy_filt_b2, hy_filt_w3, hy_filt_b3, hy_filt_freq, hy_filt_w4, hy_skip, hy_w_out, hy_b_out, mlp_w1, mlp_w2):
    p = dict(
        mix_norm=mix_norm, mlp_norm=mlp_norm,
        attn_w_qkv=attn_w_qkv.astype(BF16), attn_q_gain=attn_q_gain, attn_k_gain=attn_k_gain,
        attn_w_out=attn_w_out.astype(BF16),
        hy_w_in=hy_w_in.astype(BF16), hy_b_in=hy_b_in, hy_conv_w=hy_conv_w, hy_conv_b=hy_conv_b,
        hy_filt_w1=hy_filt_w1, hy_filt_b1=hy_filt_b1, hy_filt_w2=hy_filt_w2, hy_filt_b2=hy_filt_b2,
        hy_filt_w3=hy_filt_w3, hy_filt_b3=hy_filt_b3, hy_filt_freq=hy_filt_freq, hy_filt_w4=hy_filt_w4,
        hy_skip=hy_skip, hy_w_out=hy_w_out.astype(BF16), hy_b_out=hy_b_out,
        mlp_w1=mlp_w1.astype(BF16), mlp_w2=mlp_w2.astype(BF16),
    )
    return (_trunk(x_prompt, p), _trunk(x_sample, p))
```

```python
import functools
import math

import numpy as np
import jax
import jax.numpy as jnp
from jax import lax
from jax.experimental import pallas as pl
from jax.experimental.pallas import tpu as pltpu

F32 = jnp.float32
BF16 = jnp.bfloat16

HEAD_DIM = 128
GROUP_WINDOWS = (128, 512, 2048)
GROUP_DILATIONS = (1, 4, 16)
ROPE_THETA = 10000.0
FILTER_EMB = 33
FILTER_BANDS = (FILTER_EMB - 1) // 2
DECAY_TARGET = 1e-2
FAST_DECAY_PCT = 0.3
SLOW_DECAY_PCT = 1.5
EPS = 1e-6

LANES = 128
SUBLANES = 8
ATTN_TILE = 1024
ATTN_BATCH = 8
FFT_N2 = 128
FFT_J = FFT_N2 // SUBLANES
FFT_LANE_TILES = 4
MIB = 1024 * 1024


def _params(semantics, vmem_mib):
    return pltpu.CompilerParams(dimension_semantics=semantics, vmem_limit_bytes=vmem_mib * MIB)


def _tile(n, pref, quantum=LANES):
    if n <= pref:
        return n
    t = (pref // quantum) * quantum
    while t > quantum and n % t:
        t -= quantum
    assert n % t == 0, (n, pref)
    return t


def _split(x):
    hi = x.astype(BF16)
    lo = (x - hi.astype(F32)).astype(BF16)
    return hi, lo


def _dot(a, b):
    return jnp.dot(a, b, preferred_element_type=F32)


def _dot3(a, b):
    ah, al = _split(a)
    bh, bl = _split(b)
    return _dot(ah, bh) + (_dot(ah, bl) + _dot(al, bh))


def _dotc(c, b):
    return _dot(c, b.astype(BF16))


def _rms_rows_to(x_ref, g_ref, xn_ref, row_chunk):
    def body(c, carry):
        rows = pl.ds(pl.multiple_of(c * row_chunk, row_chunk), row_chunk)
        x = x_ref[rows, :]
        inv = lax.rsqrt(jnp.mean(x * x, axis=-1, keepdims=True) + EPS)
        xn_ref[rows, :] = (x * inv * g_ref[...]).astype(BF16)
        return carry

    lax.fori_loop(0, x_ref.shape[0] // row_chunk, body, 0)


def _store_grouped(o_ref, v):
    for c in range(o_ref.shape[1]):
        for a in range(o_ref.shape[2]):
            o_ref[:, c, a] = v[a * FFT_N2:(a + 1) * FFT_N2, c * LANES:(c + 1) * LANES].reshape(FFT_J, SUBLANES, LANES)


def _load_grouped(a_ref):
    rows = [jnp.concatenate([a_ref[:, c, s].reshape(FFT_N2, LANES) for c in range(a_ref.shape[1])], axis=1)
            for s in range(a_ref.shape[2])]
    return rows[0] if len(rows) == 1 else jnp.concatenate(rows, axis=0)


def _norm_mm_kernel(x_ref, g_ref, w_ref, *rest, has_bias, relu2, row_chunk):
    if has_bias:
        b_ref, o_ref, xn_ref = rest
    else:
        o_ref, xn_ref = rest

    @pl.when(pl.program_id(1) == 0)
    def _():
        _rms_rows_to(x_ref, g_ref, xn_ref, row_chunk)

    acc = _dot(xn_ref[...], w_ref[...])
    if has_bias:
        acc = acc + b_ref[...]
    if relu2:
        acc = jnp.maximum(acc, 0.0)
        acc = acc * acc
    o_ref[...] = acc.astype(o_ref.dtype)


def norm_matmul(x, g, w, bias=None, *, relu2=False, out_dtype=F32, tm=1024, tn=1024):
    T, K = x.shape
    N = w.shape[1]
    tm = _tile(T, tm)
    tn = _tile(N, tn)
    in_specs = [
        pl.BlockSpec((tm, K), lambda i, j: (i, 0)),
        pl.BlockSpec((1, K), lambda i, j: (0, 0)),
        pl.BlockSpec((K, tn), lambda i, j: (0, j)),
    ]
    args = [x, g.reshape(1, K), w]
    if bias is not None:
        in_specs.append(pl.BlockSpec((1, tn), lambda i, j: (0, j)))
        args.append(bias.reshape(1, N))
    return pl.pallas_call(
        functools.partial(_norm_mm_kernel, has_bias=bias is not None, relu2=relu2, row_chunk=min(tm, 128)),
        grid=(T // tm, N // tn),
        in_specs=in_specs,
        out_specs=pl.BlockSpec((tm, tn), lambda i, j: (i, j)),
        out_shape=jax.ShapeDtypeStruct((T, N), out_dtype),
        scratch_shapes=[pltpu.VMEM((tm, K), BF16)],
        compiler_params=_params(("parallel", "arbitrary"), 56),
        name="norm_matmul",
    )(*args)


def _qkv_kernel(x_ref, g_ref, w_ref, gain_ref, cos_ref, sin_ref, o_ref, xn_ref, acc_a, acc_b, stage_ref, *,
                row_chunk, tiles_per_part, n_j, n_tiles, d):
    n = pl.program_id(0)
    tile = jnp.minimum(n, n_tiles - 1)
    done = jnp.maximum(n - 1, 0)
    tm = xn_ref.shape[0]

    @pl.when(n == 0)
    def _():
        acc_b[...] = jnp.zeros_like(acc_b)

    @pl.when((tile % n_j == 0) & (n < n_tiles))
    def _():
        _rms_rows_to(x_ref, g_ref, xn_ref, row_chunk)

    def step(acc_w, acc_r):
        acc_w[...] = _dot(xn_ref[...], w_ref[...])
        is_v = (done % n_j) // tiles_per_part == 2
        gain = gain_ref[...]
        cos = cos_ref[...]
        sin = sin_ref[...]
        for s in range(acc_r.shape[1] // HEAD_DIM):
            a = acc_r[:, s * HEAD_DIM:(s + 1) * HEAD_DIM]
            inv = jnp.where(is_v, 1.0, lax.rsqrt(jnp.mean(a * a, axis=-1, keepdims=True) + EPS))
            y = a * inv * gain
            y = y * cos + pltpu.roll(y, HEAD_DIM // 2, axis=1) * sin
            if d == 1:
                o_ref[s] = y.astype(o_ref.dtype)
            else:
                stage_ref[s] = y
                n_r = tm // d
                for r in range(d):
                    o_ref[s, r * n_r:(r + 1) * n_r, :] = stage_ref[s, pl.ds(r, n_r, stride=d), :].astype(o_ref.dtype)

    @pl.when(n % 2 == 0)
    def _():
        step(acc_a, acc_b)

    @pl.when(n % 2 == 1)
    def _():
        step(acc_b, acc_a)


def qkv_project(x, g, w, gains, cos, sin, seq_len, group, *, tn=1024):
    T, K = x.shape
    part_width = w.shape[1] // (len(GROUP_DILATIONS) * 3)
    tm = ATTN_TILE
    assert seq_len % tm == 0
    tn = _tile(part_width, tn)
    pos_tiles = seq_len // tm
    tiles_per_part = part_width // tn
    n_j = 3 * tiles_per_part
    n_tiles = (T // tm) * n_j
    mm = lambda n: jnp.minimum(n, n_tiles - 1)
    ep = lambda n: jnp.maximum(n - 1, 0)
    table = lambda n: ((ep(n) % n_j) // tiles_per_part // 2, (ep(n) // n_j) % pos_tiles, 0)
    return pl.pallas_call(
        functools.partial(_qkv_kernel, row_chunk=min(tm, 128), tiles_per_part=tiles_per_part, n_j=n_j,
                          n_tiles=n_tiles, d=GROUP_DILATIONS[group]),
        grid=(n_tiles + 1,),
        in_specs=[
            pl.BlockSpec((tm, K), lambda n: (mm(n) // n_j, 0)),
            pl.BlockSpec((1, K), lambda n: (0, 0)),
            pl.BlockSpec((K, tn), lambda n: (0, group * n_j + mm(n) % n_j)),
            pl.BlockSpec((None, 1, HEAD_DIM), lambda n: (group * 3 + (ep(n) % n_j) // tiles_per_part, 0, 0)),
            pl.BlockSpec((None, tm, HEAD_DIM), table),
            pl.BlockSpec((None, tm, HEAD_DIM), table),
        ],
        out_specs=pl.BlockSpec(
            (tn // HEAD_DIM, None, tm, HEAD_DIM),
            lambda n: (ep(n) % tiles_per_part, ((ep(n) % n_j) // tiles_per_part + 2) % 3, ep(n) // n_j, 0)),
        out_shape=jax.ShapeDtypeStruct((part_width // HEAD_DIM, 3, T, HEAD_DIM), BF16),
        scratch_shapes=[pltpu.VMEM((tm, K), BF16), pltpu.VMEM((tm, tn), F32), pltpu.VMEM((tm, tn), F32),
                        pltpu.VMEM((tn // HEAD_DIM, tm, HEAD_DIM), F32)],
        compiler_params=_params(("arbitrary",), 52),
        name="qkv_project",
    )(x, g.reshape(1, K), w, gains, cos, sin)


def _mm_resid_kernel(a_ref, w_ref, *rest, has_bias, grouped):
    if has_bias:
        b_ref, r_ref, o_ref = rest
    else:
        r_ref, o_ref = rest
    k = pl.program_id(2)
    a = _load_grouped(a_ref) if grouped else a_ref[...]
    part = _dot(a.astype(BF16), w_ref[...])

    @pl.when(k == 0)
    def _():
        first = part + r_ref[...]
        if has_bias:
            first = first + b_ref[...]
        o_ref[...] = first

    @pl.when(k > 0)
    def _():
        o_ref[...] += part


def matmul_resid(a, w, resid, bias=None, *, tm=1024, tn=1024, tk=2048):
    grouped = a.ndim == 6
    T, N = resid.shape
    K = w.shape[0]
    tn = _tile(N, tn)
    tk = _tile(K, tk)
    if grouped:
        seq_len = a.shape[3] * FFT_N2
        tm = _tile(seq_len, tm)
        seq_tiles = seq_len // tm
        a_spec = pl.BlockSpec((None, FFT_J, tk // LANES, tm // FFT_N2, SUBLANES, LANES),
                              lambda i, j, k: (i // seq_tiles, 0, k, i % seq_tiles, 0, 0))
    else:
        tm = _tile(T, tm)
        a_spec = pl.BlockSpec((tm, tk), lambda i, j, k: (i, k))
    in_specs = [
        a_spec,
        pl.BlockSpec((tk, tn), lambda i, j, k: (k, j)),
    ]
    args = [a, w]
    if bias is not None:
        in_specs.append(pl.BlockSpec((1, tn), lambda i, j, k: (0, j)))
        args.append(bias.reshape(1, N))
    in_specs.append(pl.BlockSpec((tm, tn), lambda i, j, k: (i, j)))
    args.append(resid)
    return pl.pallas_call(
        functools.partial(_mm_resid_kernel, has_bias=bias is not None, grouped=grouped),
        grid=(T // tm, N // tn, K // tk),
        in_specs=in_specs,
        out_specs=pl.BlockSpec((tm, tn), lambda i, j, k: (i, j)),
        out_shape=jax.ShapeDtypeStruct((T, N), F32),
        compiler_params=_params(("parallel", "parallel", "arbitrary"), 58),
        name="matmul_resid",
    )(*args)


def _attn_kernel(*refs, tq, dils, half, n_tiles):
    G = len(dils)
    cur_refs, prev_refs, next_refs = refs[0:G], refs[G:2 * G], refs[2 * G:3 * G]
    o_ref = refs[3 * G]
    acc_ref, m_ref, l_ref, out_ref, bias_ref = refs[3 * G + 1:]
    i = pl.program_id(2)
    has_prev = i > 0
    has_next = i < n_tiles - 1
    dmax = max(dils)

    for g, d in enumerate(dils):
        n_r = tq // d
        qs = min(128, n_r)
        ks = qs + 2 * half
        n_sub = n_r // qs
        qq = lax.broadcasted_iota(jnp.int32, (qs, ks), 0)
        kk = lax.broadcasted_iota(jnp.int32, (qs, ks), 1)
        band = (kk >= qq) & (kk <= qq + 2 * half)
        prev_ok = (kk >= half) | has_prev
        next_ok = (kk < ks - half) | has_next
        for idx, valid in enumerate((band & prev_ok, band, band & next_ok, band & prev_ok & next_ok)):
            bias_ref[idx, 0:qs, 0:ks] = jnp.where(valid, 0.0, -jnp.inf)

        def window(c, r, u, g=g, n_r=n_r, qs=qs):
            lo, hi = u * qs - half, (u + 1) * qs + half
            pieces = [prev_refs[g][c, r]] if lo < 0 else []
            pieces.append(cur_refs[g][c, r, max(lo, 0):min(hi, n_r), :])
            if hi > n_r:
                pieces.append(next_refs[g][c, r])
            return pieces[0] if len(pieces) == 1 else jnp.concatenate(pieces, axis=0)

        blocks = [(r, u) for r in range(d) for u in range(n_sub)]
        for b0 in range(0, len(blocks), ATTN_BATCH):
            batch = blocks[b0:b0 + ATTN_BATCH]
            edge = lambda u: (3 if n_sub == 1 else 0) if u == 0 else (2 if u == n_sub - 1 else 1)
            q = jnp.stack([cur_refs[g][2, r, u * qs:(u + 1) * qs, :] for r, u in batch])
            kw = jnp.stack([window(0, r, u) for r, u in batch])
            vw = jnp.stack([window(1, r, u) for r, u in batch])
            bias = jnp.stack([bias_ref[edge(u), 0:qs, 0:ks] for _, u in batch])
            s = jnp.einsum("bqd,bkd->bqk", q, kw, preferred_element_type=F32) + bias
            m_blk = jnp.max(s, axis=-1, keepdims=True)
            p = jnp.exp(s - m_blk)
            l_blk = jnp.sum(p, axis=-1, keepdims=True)
            pv = jnp.einsum("bqk,bkd->bqd", p.astype(BF16), vw, preferred_element_type=F32)
            st, pitch = _merge_pitch(dmax // d)
            for b, (r, u) in enumerate(batch):
                base = r * n_r + u * qs
                chunk = qs if st == pitch else st
                for k in range(qs // chunk):
                    src = slice(k * chunk, (k + 1) * chunk)
                    dst = pl.ds(base + k * chunk if st == pitch else (base // st + k) * pitch, chunk)
                    m_ref[g, dst, :] = m_blk[b][src]
                    l_ref[g, dst, :] = l_blk[b][src]
                    acc_ref[g, dst, :] = pv[b][src]

    n_max = tq // dmax
    out_pitch = n_max + SUBLANES
    for r in range(dmax):
        sel = []
        for d in dils:
            st, pitch = _merge_pitch(dmax // d)
            start = (r % d) * (tq // d // st) * pitch + r // d
            sel.append(pl.ds(start, n_max, stride=pitch * (dmax // d) // st) if d < dmax else pl.ds(start, n_max))
        ms = [m_ref[g, sel[g], :] for g in range(G)]
        m = functools.reduce(jnp.maximum, ms)
        ws = [jnp.exp(mg - m) for mg in ms]
        num = sum(ws[g] * acc_ref[g, sel[g], :] for g in range(G))
        den = sum(ws[g] * l_ref[g, sel[g], :] for g in range(G))
        out_ref[r * out_pitch:r * out_pitch + n_max, :] = num / den
    o_ref[...] = jnp.concatenate([out_ref[pl.ds(j, dmax, stride=out_pitch), :] for j in range(n_max)],
                                 axis=0).astype(o_ref.dtype)


def _merge_pitch(stride):
    if stride % SUBLANES:
        return SUBLANES, SUBLANES
    return stride, stride + SUBLANES


def window_attention(qkv, batch, seq_len, n_heads):
    G = len(GROUP_DILATIONS)
    tq = ATTN_TILE
    assert seq_len % tq == 0
    n_tiles = seq_len // tq
    total_tiles = batch * n_tiles
    half = GROUP_WINDOWS[0] // (2 * GROUP_DILATIONS[0])
    for wdw, d in zip(GROUP_WINDOWS, GROUP_DILATIONS):
        assert wdw // (2 * d) == half and half * d <= tq and tq % (d * min(128, tq // d)) == 0
    H = n_heads
    dils = GROUP_DILATIONS

    cur, prev, nxt = [], [], []
    for d in dils:
        cur.append(pl.BlockSpec((None, 3, None, d, tq // d, HEAD_DIM),
                                lambda b, h, i: (h, 0, b * n_tiles + i, 0, 0, 0)))
        prev.append(pl.BlockSpec((None, 2, None, d, half, HEAD_DIM),
                                 lambda b, h, i, d=d: (h, 0, jnp.maximum(b * n_tiles + i - 1, 0), 0,
                                                       tq // d // half - 1, 0)))
        nxt.append(pl.BlockSpec((None, 2, None, d, half, HEAD_DIM),
                                lambda b, h, i: (h, 0, jnp.minimum(b * n_tiles + i + 1, total_tiles - 1), 0, 0, 0)))
    views = [t.reshape(H, 3, total_tiles, d, tq // d, HEAD_DIM) for t, d in zip(qkv, dils)]
    merge_rows = max(tq // st * pitch for st, pitch in (_merge_pitch(max(dils) // d) for d in dils))
    in_specs = cur + prev + nxt
    return pl.pallas_call(
        functools.partial(_attn_kernel, tq=tq, dils=dils, half=half, n_tiles=n_tiles),
        grid=(batch, H, n_tiles),
        in_specs=in_specs,
        out_specs=pl.BlockSpec((tq, HEAD_DIM), lambda b, h, i: (b * n_tiles + i, h)),
        out_shape=jax.ShapeDtypeStruct((batch * seq_len, H * HEAD_DIM), BF16),
        scratch_shapes=[
            pltpu.VMEM((G, merge_rows, HEAD_DIM), F32),
            pltpu.VMEM((G, merge_rows, 1), F32),
            pltpu.VMEM((G, merge_rows, 1), F32),
            pltpu.VMEM((max(dils) * (tq // max(dils) + SUBLANES), HEAD_DIM), F32),
            pltpu.VMEM((4, 128, 128 + 2 * half), F32),
        ],
        compiler_params=_params(("parallel", "parallel", "parallel"), 40),
        name="window_attention",
    )(*(views * 3))


def _sconv_kernel(*refs, n_tiles):
    (x0m, x0p, x0n, x1m, x1p, x1n, vm, vp, vn, w0, w1, w2, b0, b1, b2, z_ref, x0_ref) = refs
    i = pl.program_id(1)

    def conv(m_ref, p_ref, n_ref, w_ref, b_ref):
        x = m_ref[...]
        ts = x.shape[0]
        prev_row = jnp.where(i > 0, p_ref[7:8, :], 0.0)
        next_row = jnp.where(i < n_tiles - 1, n_ref[0:1, :], 0.0)
        row = lax.broadcasted_iota(jnp.int32, x.shape, 0)
        up = jnp.where(row == 0, prev_row, pltpu.roll(x, 1, axis=0))
        dn = jnp.where(row == ts - 1, next_row, pltpu.roll(x, ts - 1, axis=0))
        return up * w_ref[0:1, :] + x * w_ref[1:2, :] + dn * w_ref[2:3, :] + b_ref[...]

    _store_grouped(x0_ref, conv(x0m, x0p, x0n, w0, b0))
    _store_grouped(z_ref, conv(vm, vp, vn, w2, b2) * conv(x1m, x1p, x1n, w1, b1))


def short_conv_gate(u, conv_w, conv_b, *, ts=512, ct=512):
    B, L, D3 = u.shape
    D = D3 // 3
    ts = _tile(L, ts, FFT_N2)
    ct = _tile(D, ct)
    n_tiles = L // ts
    n_ct = D // ct
    rows8 = ts // 8
    last8 = L // 8 - 1
    specs, args = [], []
    for part in range(3):
        off = part * n_ct
        specs += [
            pl.BlockSpec((None, ts, ct), lambda b, i, j, off=off: (b, i, off + j)),
            pl.BlockSpec((None, 8, ct), lambda b, i, j, off=off: (b, jnp.maximum(i * rows8 - 1, 0), off + j)),
            pl.BlockSpec((None, 8, ct), lambda b, i, j, off=off: (b, jnp.minimum((i + 1) * rows8, last8), off + j)),
        ]
        args += [u, u, u]
    for part in range(3):
        specs.append(pl.BlockSpec((3, ct), lambda b, i, j, off=part * n_ct: (0, off + j)))
        args.append(conv_w)
    for part in range(3):
        specs.append(pl.BlockSpec((1, ct), lambda b, i, j, off=part * n_ct: (0, off + j)))
        args.append(conv_b.reshape(1, D3))
    out_spec = pl.BlockSpec((None, FFT_J, ct // LANES, ts // FFT_N2, SUBLANES, LANES),
                            lambda b, i, j: (b, 0, j, i, 0, 0))
    return pl.pallas_call(
        functools.partial(_sconv_kernel, n_tiles=n_tiles),
        grid=(B, n_tiles, n_ct),
        in_specs=specs,
        out_specs=[out_spec, out_spec],
        out_shape=[jax.ShapeDtypeStruct((B, FFT_J, D // LANES, L // FFT_N2, SUBLANES, LANES), F32)] * 2,
        compiler_params=_params(("parallel", "parallel", "parallel"), 32),
        name="short_conv_gate",
    )(*args)


def _filter_kernel(feat_ref, tv_ref, w1, b1, w2, b2, w3, b3, fr_ref, w4, delta_ref, o_ref):
    fr = fr_ref[...]
    h = jnp.sin(fr * (_dot3(feat_ref[...], w1[...]) + b1[...]))
    h = jnp.sin(fr * (_dot3(h, w2[...]) + b2[...]))
    h = jnp.sin(fr * (_dot3(h, w3[...]) + b3[...]))
    out = _dot3(h, w4[...])
    t = tv_ref[:, 0:1]
    valid = tv_ref[:, 1:2]
    _store_grouped(o_ref, out * jnp.exp(-t * delta_ref[...]) * valid)


def conv_kernel_signal(L, D, fw1, fb1, fw2, fb2, fw3, fb3, ffreq, fw4, *, tl=256):
    FH = fw2.shape[0]
    FE = 64
    p = jnp.arange(2 * L, dtype=jnp.int32)
    pos = jnp.where(p < L, p, 2 * L - p).astype(F32)
    t = pos / (L - 1)
    bands = jnp.linspace(1e-4, FILTER_BANDS - 1, FILTER_BANDS, dtype=F32)
    ang = (2.0 * math.pi / L) * pos[:, None] * bands[None, :]
    feat = jnp.concatenate(
        [t[:, None], jnp.cos(ang), -jnp.sin(ang), jnp.zeros((2 * L, FE - FILTER_EMB), F32)], axis=-1)
    tv = jnp.stack([t, (p != L).astype(F32)], axis=-1)
    deltas = np.abs(np.linspace(math.log(DECAY_TARGET) / SLOW_DECAY_PCT, math.log(DECAY_TARGET) / FAST_DECAY_PCT, D,
                                dtype=np.float32)).reshape(1, D)
    w1p = jnp.pad(fw1.astype(F32), ((0, FE - FILTER_EMB), (0, 0)))
    tl = _tile(L, tl, FFT_N2)
    half_tiles = L // tl
    full = lambda shape: pl.BlockSpec(shape, lambda i: (0, 0))
    return pl.pallas_call(
        _filter_kernel,
        grid=(2 * L // tl,),
        in_specs=[
            pl.BlockSpec((tl, FE), lambda i: (i, 0)),
            pl.BlockSpec((tl, 2), lambda i: (i, 0)),
            full((FE, FH)), full((1, FH)), full((FH, FH)), full((1, FH)), full((FH, FH)), full((1, FH)),
            full((1, FH)),
            pl.BlockSpec((FH, D), lambda i: (0, i // half_tiles)),
            full((1, D)),
        ],
        out_specs=pl.BlockSpec((FFT_J, D // LANES, tl // FFT_N2, SUBLANES, LANES), lambda i: (0, 0, i, 0, 0)),
        out_shape=jax.ShapeDtypeStruct((FFT_J, D // LANES, 2 * L // FFT_N2, SUBLANES, LANES), F32),
        compiler_params=_params(("parallel",), 32),
        name="hyena_filter",
    )(feat, tv, w1p, fb1.reshape(1, FH).astype(F32), fw2.astype(F32), fb2.reshape(1, FH).astype(F32),
      fw3.astype(F32), fb3.reshape(1, FH).astype(F32), ffreq.reshape(1, FH).astype(F32), fw4.astype(F32),
      jnp.asarray(deltas))


def _const(m):
    return jnp.asarray(np.asarray(m, np.float64), dtype=F32).astype(BF16)


def _stack(re, im):
    return np.block([[re, -im], [im, re]])


def _dft_tables(n1, n2):
    n, k1 = n1 * n2, n1 // 2
    a1 = 2.0 * np.pi * np.outer(np.arange(n1), np.arange(n1)) / n1
    c1, s1 = np.cos(a1), np.sin(a1)
    a2 = 2.0 * np.pi * np.outer(np.arange(n2), np.arange(n2)) / n2
    c2, s2 = np.cos(a2), np.sin(a2)
    at = 2.0 * np.pi * np.outer(np.arange(n2), np.arange(n1)) / n
    bcast = lambda t: jnp.broadcast_to(jnp.asarray(t, F32)[:, :, None], t.shape + (LANES,))
    return dict(
        f1_real=np.concatenate([c1, -s1], axis=0),
        f1_cplx=_stack(c1[:, :k1], -s1[:, :k1]),
        f2=_stack(c2, -s2),
        g2=_stack(c2, s2),
        g1_cplx=_stack(c1[:k1], s1[:k1]),
        tw1=(bcast(np.cos(at)), bcast(-np.sin(at))),
        tw2=(bcast(np.cos(at.T)), bcast(np.sin(at.T))),
    )


def _fft1_kernel(x_ref, f_ref, twr_ref, twi_ref, ar_ref, ai_ref, *, n1):
    f = f_ref[...]
    parts, cb, rows = x_ref.shape[0], x_ref.shape[1], x_ref.shape[2] // SUBLANES
    for i in range(SUBLANES):
        sel = pl.ds(i, rows, stride=SUBLANES)
        x = jnp.concatenate([jnp.concatenate([x_ref[p, c, sel, :] for c in range(cb)], axis=1)
                             for p in range(parts)], axis=0)
        r = _dotc(f, x)
        re, im = r[:n1], r[n1:]
        twr = jnp.concatenate([twr_ref[i]] * cb, axis=1)
        twi = jnp.concatenate([twi_ref[i]] * cb, axis=1)
        ar = re * twr - im * twi
        ai = re * twi + im * twr
        for c in range(cb):
            ar_ref[c, pl.ds(i, n1, stride=SUBLANES), :] = ar[:, c * LANES:(c + 1) * LANES]
            ai_ref[c, pl.ds(i, n1, stride=SUBLANES), :] = ai[:, c * LANES:(c + 1) * LANES]


def _fft_stage1(x, n1, f1, tw1):
    P, parts, _, C, k8, _ = x.shape
    fc = _const(f1)
    cb = min(C, FFT_LANE_TILES)
    const = lambda shape: pl.BlockSpec(shape, lambda j, p, c: (0,) * len(shape))
    out_spec = pl.BlockSpec((None, None, cb, n1 * SUBLANES, LANES), lambda j, p, c: (p, j, c, 0, 0))
    tw_spec = pl.BlockSpec((SUBLANES, n1, LANES), lambda j, p, c: (j, 0, 0))
    return pl.pallas_call(
        functools.partial(_fft1_kernel, n1=n1),
        grid=(FFT_J, P, C // cb),
        in_specs=[
            pl.BlockSpec((None, parts, None, cb, k8, LANES), lambda j, p, c: (p, 0, j, c, 0, 0)),
            const(f1.shape), tw_spec, tw_spec,
        ],
        out_specs=[out_spec, out_spec],
        out_shape=[jax.ShapeDtypeStruct((P, FFT_J, C, n1 * SUBLANES, LANES), F32)] * 2,
        compiler_params=_params(("parallel", "parallel", "parallel"), 40),
        name="fft_stage1",
    )(x, fc, tw1[0], tw1[1])


def _fft2_kernel(*refs, n2, cc, scale, spectrum_only):
    if spectrum_only:
        ar_ref, ai_ref, f_ref, yr_ref, yi_ref = refs
    else:
        ar_ref, ai_ref, kr_ref, ki_ref, f_ref, g_ref, twr_ref, twi_ref, yr_ref, yi_ref = refs
        twr = jnp.concatenate([twr_ref[...]] * (cc // LANES), axis=1)
        twi = jnp.concatenate([twi_ref[...]] * (cc // LANES), axis=1)
    f = f_ref[...]
    lt = cc // LANES
    rows = lambda ref, c: jnp.concatenate([ref[:, c * lt + t].reshape(n2, LANES) for t in range(lt)], axis=1)
    for c in range(ar_ref.shape[1] // lt):
        cols = slice(c * cc, (c + 1) * cc)
        y = _dotc(f, jnp.concatenate([rows(ar_ref, c), rows(ai_ref, c)], axis=0))
        yr, yi = y[:n2], y[n2:]
        if spectrum_only:
            yr_ref[:, cols] = yr * scale
            yi_ref[:, cols] = yi * scale
        else:
            kr = kr_ref[:, cols]
            ki = ki_ref[:, cols]
            pr = yr * kr - yi * ki
            pi = yr * ki + yi * kr
            q = _dotc(g_ref[...], jnp.concatenate([pr, pi], axis=0))
            qr, qi = q[:n2], q[n2:]
            outr = qr * twr - qi * twi
            outi = qr * twi + qi * twr
            for t in range(lt):
                yr_ref[:, c * lt + t] = outr[:, t * LANES:(t + 1) * LANES].reshape(FFT_J, SUBLANES, LANES)
                yi_ref[:, c * lt + t] = outi[:, t * LANES:(t + 1) * LANES].reshape(FFT_J, SUBLANES, LANES)


def _fft_stage2(ar, ai, n1, n2, D, f2, *, kf=None, g2=None, tw2=None, scale=1.0):
    B, _, C = ar.shape[:3]
    ar = ar.reshape(B, FFT_J, C, n1, SUBLANES, LANES)
    ai = ai.reshape(B, FFT_J, C, n1, SUBLANES, LANES)
    fc = _const(f2)
    cc = _tile(D, 512)
    blk = pl.BlockSpec((None, FFT_J, C, None, SUBLANES, LANES), lambda f, b: (b, 0, 0, f, 0, 0))
    const = lambda shape: pl.BlockSpec(shape, lambda f, b: (0,) * len(shape))
    specs = [blk, blk]
    args = [ar, ai]
    if kf is not None:
        gc = _const(g2)
        kblk = pl.BlockSpec((None, n2, D), lambda f, b: (f, 0, 0))
        tblk = pl.BlockSpec((None, n2, LANES), lambda f, b: (f, 0, 0))
        specs += [kblk, kblk, const((2 * n2, 2 * n2)), const((2 * n2, 2 * n2)), tblk, tblk]
        args += [kf[0], kf[1], fc, gc, tw2[0], tw2[1]]
        out_spec = blk
        out_shape = jax.ShapeDtypeStruct((B, FFT_J, C, n1, SUBLANES, LANES), F32)
    else:
        specs += [const((2 * n2, 2 * n2))]
        args += [fc]
        out_spec = pl.BlockSpec((None, None, n2, D), lambda f, b: (b, f, 0, 0))
        out_shape = jax.ShapeDtypeStruct((B, n1, n2, D), F32)
    out = pl.pallas_call(
        functools.partial(_fft2_kernel, n2=n2, cc=cc, scale=scale, spectrum_only=kf is None),
        grid=(n1, B),
        in_specs=specs,
        out_specs=[out_spec, out_spec],
        out_shape=[out_shape] * 2,
        compiler_params=_params(("parallel", "parallel"), 40),
        name="fft_stage2",
    )(*args)
    if kf is not None:
        out = [o.reshape(B, FFT_J, C, n1 * SUBLANES, LANES) for o in out]
    return out


def _fft3_kernel(qr_ref, qi_ref, g_ref, z_ref, x0_ref, skip_ref, o_ref, *, n1):
    g = g_ref[...]
    cb, rows = z_ref.shape[1], z_ref.shape[2] // SUBLANES
    for i in range(SUBLANES):
        qsel = pl.ds(i, n1, stride=SUBLANES)
        sel = pl.ds(i, rows, stride=SUBLANES)
        q = jnp.concatenate([jnp.concatenate([ref[c, qsel, :] for c in range(cb)], axis=1)
                             for ref in (qr_ref, qi_ref)], axis=0)
        y = _dotc(g, q)
        for p in range(2):
            for c in range(cb):
                lanes = slice(c * LANES, (c + 1) * LANES)
                conv = y[p * rows:(p + 1) * rows, lanes]
                o_ref[p, c, sel, :] = (conv + z_ref[p, c, sel, :] * skip_ref[:, lanes]) * x0_ref[p, c, sel, :]


def _fft_stage3(qr, qi, z, x0, skip, n1, g1):
    P, _, _, C, k8, _ = z.shape
    gc = _const(g1)
    cb = min(C, FFT_LANE_TILES)
    qblk = pl.BlockSpec((None, None, cb, n1 * SUBLANES, LANES), lambda p, j, c: (p, j, c, 0, 0))
    zblk = pl.BlockSpec((None, 2, None, cb, k8, LANES), lambda p, j, c: (p, 0, j, c, 0, 0))
    const = lambda shape: pl.BlockSpec(shape, lambda p, j, c: (0,) * len(shape))
    return pl.pallas_call(
        functools.partial(_fft3_kernel, n1=n1),
        grid=(P, FFT_J, C // cb),
        in_specs=[qblk, qblk, const(g1.shape), zblk, zblk,
                  pl.BlockSpec((1, cb * LANES), lambda p, j, c: (0, c))],
        out_specs=zblk,
        out_shape=jax.ShapeDtypeStruct(z.shape, F32),
        compiler_params=_params(("parallel", "parallel", "parallel"), 52),
        name="fft_stage3",
    )(qr, qi, gc, z, x0, skip.reshape(1, C * LANES).astype(F32))


def long_conv_gate(z, x0, skip, kern):
    B, _, C, k1 = z.shape[:4]
    D = C * LANES
    n2 = FFT_N2
    n1 = 2 * k1
    assert B % 2 == 0
    t = _dft_tables(n1, n2)
    ka = _fft_stage1(kern.reshape(1, 1, FFT_J, C, n1 * SUBLANES, LANES), n1, t["f1_real"], t["tw1"])
    kf = _fft_stage2(ka[0], ka[1], n1, n2, D, t["f2"], scale=1.0 / (n1 * n2))
    kf = (kf[0].reshape(n1, n2, D), kf[1].reshape(n1, n2, D))
    pairs = lambda v: v.reshape(B // 2, 2, FFT_J, C, k1 * SUBLANES, LANES)
    a = _fft_stage1(pairs(z), n1, t["f1_cplx"], t["tw1"])
    q = _fft_stage2(a[0], a[1], n1, n2, D, t["f2"], kf=kf, g2=t["g2"], tw2=t["tw2"])
    return _fft_stage3(q[0], q[1], pairs(z), pairs(x0), skip, n1, t["g1_cplx"]).reshape(z.shape)


def _rope_tables(S):
    inv = ROPE_THETA ** (-jnp.arange(0, HEAD_DIM, 2, dtype=F32) / HEAD_DIM)
    ang = jnp.arange(S, dtype=F32)[:, None] * inv[None, :]
    cos, sin = jnp.cos(ang), jnp.sin(ang)
    cos2, sin2 = jnp.concatenate([cos, cos], axis=-1), jnp.concatenate([-sin, sin], axis=-1)
    return jnp.stack([cos2, jnp.ones_like(cos2)]), jnp.stack([sin2, jnp.zeros_like(sin2)])


def _trunk(x, p):
    B, S, D = x.shape
    H = D // HEAD_DIM
    G = len(GROUP_DILATIONS)
    T = B * S
    cos, sin = _rope_tables(S)
    x = x.reshape(T, D)
    depth = p["mix_norm"].shape[0]
    for i in range(depth):
        j = i // 2
        if i % 2 == 0:
            scale = HEAD_DIM ** -0.5
            gains = jnp.stack([p["attn_q_gain"][j] * scale, p["attn_k_gain"][j], jnp.ones_like(p["attn_k_gain"][j])],
                              axis=1).reshape(G * 3, 1, HEAD_DIM).astype(F32)
            qkv = [qkv_project(x, p["mix_norm"][i], p["attn_w_qkv"][j], gains, cos, sin, S, g) for g in range(G)]
            o = window_attention(qkv, B, S, H)
            x = matmul_resid(o, p["attn_w_out"][j], x, tm=512, tn=2048)
        else:
            u = norm_matmul(x, p["mix_norm"][i], p["hy_w_in"][j], p["hy_b_in"][j])
            z, x0 = short_conv_gate(u.reshape(B, S, 3 * D), p["hy_conv_w"][j], p["hy_conv_b"][j])
            kern = conv_kernel_signal(S, D, p["hy_filt_w1"][j], p["hy_filt_b1"][j], p["hy_filt_w2"][j],
                                      p["hy_filt_b2"][j], p["hy_filt_w3"][j], p["hy_filt_b3"][j],
                                      p["hy_filt_freq"][j], p["hy_filt_w4"][j])
            y = long_conv_gate(z, x0, p["hy_skip"][j], kern)
            x = matmul_resid(y, p["hy_w_out"][j], x, p["hy_b_out"][j], tm=512, tn=2048)
        h = norm_matmul(x, p["mlp_norm"][i], p["mlp_w1"][i], relu2=True, out_dtype=BF16, tn=2048)
        x = matmul_resid(h, p["mlp_w2"][i], x, tk=4096)
    return x.reshape(B, S, D)


def kernel(x_prompt, x_sample, mix_norm, mlp_norm, attn_w_qkv, attn_q_gain, attn_k_gain, attn_w_out, hy_w_in, hy_b_in, hy_conv_w, hy_conv_b, hy_filt_w1, hy_filt_b1, hy_filt_w2, hy_filt_b2, hy_filt_w3, hy_filt_b3, hy_filt_freq, hy_filt_w4, hy_skip, hy_w_out, hy_b_out, mlp_w1, mlp_w2):
    p = dict(
        mix_norm=mix_norm, mlp_norm=mlp_norm,
        attn_w_qkv=attn_w_qkv.astype(BF16), attn_q_gain=attn_q_gain, attn_k_gain=attn_k_gain,
        attn_w_out=attn_w_out.astype(BF16),
        hy_w_in=hy_w_in.astype(BF16), hy_b_in=hy_b_in, hy_conv_w=hy_conv_w, hy_conv_b=hy_conv_b,
        hy_filt_w1=hy_filt_w1, hy_filt_b1=hy_filt_b1, hy_filt_w2=hy_filt_w2, hy_filt_b2=hy_filt_b2,
        hy_filt_w3=hy_filt_w3, hy_filt_b3=hy_filt_b3, hy_filt_freq=hy_filt_freq, hy_filt_w4=hy_filt_w4,
        hy_skip=hy_skip, hy_w_out=hy_w_out.astype(BF16), hy_b_out=hy_b_out,
        mlp_w1=mlp_w1.astype(BF16), mlp_w2=mlp_w2.astype(BF16),
    )
    return (_trunk(x_prompt, p), _trunk(x_sample, p))
```

```python
import functools
import math

import numpy as np
import jax
import jax.numpy as jnp
from jax import lax
from jax.experimental import pallas as pl
from jax.experimental.pallas import tpu as pltpu

F32 = jnp.float32
BF16 = jnp.bfloat16

HEAD_DIM = 128
GROUP_WINDOWS = (128, 512, 2048)
GROUP_DILATIONS = (1, 4, 16)
ROPE_THETA = 10000.0
FILTER_EMB = 33
FILTER_BANDS = (FILTER_EMB - 1) // 2
DECAY_TARGET = 1e-2
FAST_DECAY_PCT = 0.3
SLOW_DECAY_PCT = 1.5
EPS = 1e-6

LANES = 128
SUBLANES = 8
ATTN_TILE = 1024
ATTN_BATCH = 8
FFT_N2 = 128
FFT_J = FFT_N2 // SUBLANES
FFT_LANE_TILES = 4
MIB = 1024 * 1024


def _params(semantics, vmem_mib):
    return pltpu.CompilerParams(dimension_semantics=semantics, vmem_limit_bytes=vmem_mib * MIB)


def _tile(n, pref, quantum=LANES):
    if n <= pref:
        return n
    t = (pref // quantum) * quantum
    while t > quantum and n % t:
        t -= quantum
    assert n % t == 0, (n, pref)
    return t


def _split(x):
    hi = x.astype(BF16)
    lo = (x - hi.astype(F32)).astype(BF16)
    return hi, lo


def _dot(a, b):
    return jnp.dot(a, b, preferred_element_type=F32)


def _dot3(a, b):
    ah, al = _split(a)
    bh, bl = _split(b)
    return _dot(ah, bh) + (_dot(ah, bl) + _dot(al, bh))


def _dotc(c, b):
    return _dot(c, b.astype(BF16))


def _rms_rows_to(x_ref, g_ref, xn_ref, row_chunk):
    def body(c, carry):
        rows = pl.ds(pl.multiple_of(c * row_chunk, row_chunk), row_chunk)
        x = x_ref[rows, :]
        inv = lax.rsqrt(jnp.mean(x * x, axis=-1, keepdims=True) + EPS)
        xn_ref[rows, :] = (x * inv * g_ref[...]).astype(BF16)
        return carry

    lax.fori_loop(0, x_ref.shape[0] // row_chunk, body, 0)


def _store_grouped(o_ref, v):
    for c in range(o_ref.shape[1]):
        for a in range(o_ref.shape[2]):
            o_ref[:, c, a] = v[a * FFT_N2:(a + 1) * FFT_N2, c * LANES:(c + 1) * LANES].reshape(FFT_J, SUBLANES, LANES)


def _load_grouped(a_ref):
    rows = [jnp.concatenate([a_ref[:, c, s].reshape(FFT_N2, LANES) for c in range(a_ref.shape[1])], axis=1)
            for s in range(a_ref.shape[2])]
    return rows[0] if len(rows) == 1 else jnp.concatenate(rows, axis=0)


def _norm_mm_kernel(x_ref, g_ref, w_ref, *rest, has_bias, relu2, row_chunk):
    if has_bias:
        b_ref, o_ref, xn_ref = rest
    else:
        o_ref, xn_ref = rest

    @pl.when(pl.program_id(1) == 0)
    def _():
        _rms_rows_to(x_ref, g_ref, xn_ref, row_chunk)

    acc = _dot(xn_ref[...], w_ref[...])
    if has_bias:
        acc = acc + b_ref[...]
    if relu2:
        acc = jnp.maximum(acc, 0.0)
        acc = acc * acc
    o_ref[...] = acc.astype(o_ref.dtype)


def norm_matmul(x, g, w, layer, bias=None, *, relu2=False, out_dtype=F32, tm=1024, tn=1024):
    T, K = x.shape
    N = w.shape[2]
    tm = _tile(T, tm)
    tn = _tile(N, tn)
    in_specs = [
        pl.BlockSpec((tm, K), lambda i, j: (i, 0)),
        pl.BlockSpec((1, K), lambda i, j: (0, 0)),
        pl.BlockSpec((None, K, tn), lambda i, j: (layer, 0, j)),
    ]
    args = [x, g.reshape(1, K), w]
    if bias is not None:
        in_specs.append(pl.BlockSpec((1, tn), lambda i, j: (0, j)))
        args.append(bias.reshape(1, N))
    return pl.pallas_call(
        functools.partial(_norm_mm_kernel, has_bias=bias is not None, relu2=relu2, row_chunk=min(tm, 128)),
        grid=(T // tm, N // tn),
        in_specs=in_specs,
        out_specs=pl.BlockSpec((tm, tn), lambda i, j: (i, j)),
        out_shape=jax.ShapeDtypeStruct((T, N), out_dtype),
        scratch_shapes=[pltpu.VMEM((tm, K), BF16)],
        compiler_params=_params(("parallel", "arbitrary"), 56),
        name="norm_matmul",
    )(*args)


def _qkv_kernel(x_ref, g_ref, w_ref, gain_ref, cos_ref, sin_ref, o_ref, xn_ref, acc_a, acc_b, stage_ref, *,
                row_chunk, tiles_per_part, n_j, n_tiles, d):
    n = pl.program_id(0)
    tile = jnp.minimum(n, n_tiles - 1)
    done = jnp.maximum(n - 1, 0)
    tm = xn_ref.shape[0]

    @pl.when(n == 0)
    def _():
        acc_b[...] = jnp.zeros_like(acc_b)

    @pl.when((tile % n_j == 0) & (n < n_tiles))
    def _():
        _rms_rows_to(x_ref, g_ref, xn_ref, row_chunk)

    def step(acc_w, acc_r):
        acc_w[...] = _dot(xn_ref[...], w_ref[...])
        is_v = (done % n_j) // tiles_per_part == 2
        gain = gain_ref[...]
        cos = cos_ref[...]
        sin = sin_ref[...]
        for s in range(acc_r.shape[1] // HEAD_DIM):
            a = acc_r[:, s * HEAD_DIM:(s + 1) * HEAD_DIM]
            inv = jnp.where(is_v, 1.0, lax.rsqrt(jnp.mean(a * a, axis=-1, keepdims=True) + EPS))
            y = a * inv * gain
            y = y * cos + pltpu.roll(y, HEAD_DIM // 2, axis=1) * sin
            if d == 1:
                o_ref[s] = y.astype(o_ref.dtype)
            else:
                stage_ref[s] = y
                n_r = tm // d
                for r in range(d):
                    o_ref[s, r * n_r:(r + 1) * n_r, :] = stage_ref[s, pl.ds(r, n_r, stride=d), :].astype(o_ref.dtype)

    @pl.when(n % 2 == 0)
    def _():
        step(acc_a, acc_b)

    @pl.when(n % 2 == 1)
    def _():
        step(acc_b, acc_a)


def qkv_project(x, g, w, layer, gains, cos, sin, seq_len, group, *, tn=1024):
    T, K = x.shape
    part_width = w.shape[2] // (len(GROUP_DILATIONS) * 3)
    tm = ATTN_TILE
    assert seq_len % tm == 0
    tn = _tile(part_width, tn)
    pos_tiles = seq_len // tm
    tiles_per_part = part_width // tn
    n_j = 3 * tiles_per_part
    n_tiles = (T // tm) * n_j
    mm = lambda n: jnp.minimum(n, n_tiles - 1)
    ep = lambda n: jnp.maximum(n - 1, 0)
    table = lambda n: ((ep(n) % n_j) // tiles_per_part // 2, (ep(n) // n_j) % pos_tiles, 0)
    return pl.pallas_call(
        functools.partial(_qkv_kernel, row_chunk=min(tm, 128), tiles_per_part=tiles_per_part, n_j=n_j,
                          n_tiles=n_tiles, d=GROUP_DILATIONS[group]),
        grid=(n_tiles + 1,),
        in_specs=[
            pl.BlockSpec((tm, K), lambda n: (mm(n) // n_j, 0)),
            pl.BlockSpec((1, K), lambda n: (0, 0)),
            pl.BlockSpec((None, K, tn), lambda n: (layer, 0, group * n_j + mm(n) % n_j)),
            pl.BlockSpec((None, 1, HEAD_DIM), lambda n: (group * 3 + (ep(n) % n_j) // tiles_per_part, 0, 0)),
            pl.BlockSpec((None, tm, HEAD_DIM), table),
            pl.BlockSpec((None, tm, HEAD_DIM), table),
        ],
        out_specs=pl.BlockSpec(
            (tn // HEAD_DIM, None, tm, HEAD_DIM),
            lambda n: (ep(n) % tiles_per_part, ((ep(n) % n_j) // tiles_per_part + 2) % 3, ep(n) // n_j, 0)),
        out_shape=jax.ShapeDtypeStruct((part_width // HEAD_DIM, 3, T, HEAD_DIM), BF16),
        scratch_shapes=[pltpu.VMEM((tm, K), BF16), pltpu.VMEM((tm, tn), F32), pltpu.VMEM((tm, tn), F32),
                        pltpu.VMEM((tn // HEAD_DIM, tm, HEAD_DIM), F32)],
        compiler_params=_params(("arbitrary",), 52),
        name="qkv_project",
    )(x, g.reshape(1, K), w, gains, cos, sin)


def _mm_resid_kernel(a_ref, w_ref, *rest, has_bias, grouped):
    if has_bias:
        b_ref, r_ref, o_ref = rest
    else:
        r_ref, o_ref = rest
    k = pl.program_id(2)
    a = _load_grouped(a_ref) if grouped else a_ref[...]
    part = _dot(a.astype(BF16), w_ref[...])

    @pl.when(k == 0)
    def _():
        first = part + r_ref[...]
        if has_bias:
            first = first + b_ref[...]
        o_ref[...] = first

    @pl.when(k > 0)
    def _():
        o_ref[...] += part


def matmul_resid(a, w, layer, resid, bias=None, *, tm=1024, tn=1024, tk=2048):
    grouped = a.ndim == 6
    T, N = resid.shape
    K = w.shape[1]
    tn = _tile(N, tn)
    tk = _tile(K, tk)
    if grouped:
        seq_len = a.shape[3] * FFT_N2
        tm = _tile(seq_len, tm)
        seq_tiles = seq_len // tm
        a_spec = pl.BlockSpec((None, FFT_J, tk // LANES, tm // FFT_N2, SUBLANES, LANES),
                              lambda i, j, k: (i // seq_tiles, 0, k, i % seq_tiles, 0, 0))
    else:
        tm = _tile(T, tm)
        a_spec = pl.BlockSpec((tm, tk), lambda i, j, k: (i, k))
    in_specs = [
        a_spec,
        pl.BlockSpec((None, tk, tn), lambda i, j, k: (layer, k, j)),
    ]
    args = [a, w]
    if bias is not None:
        in_specs.append(pl.BlockSpec((1, tn), lambda i, j, k: (0, j)))
        args.append(bias.reshape(1, N))
    in_specs.append(pl.BlockSpec((tm, tn), lambda i, j, k: (i, j)))
    args.append(resid)
    return pl.pallas_call(
        functools.partial(_mm_resid_kernel, has_bias=bias is not None, grouped=grouped),
        grid=(T // tm, N // tn, K // tk),
        in_specs=in_specs,
        out_specs=pl.BlockSpec((tm, tn), lambda i, j, k: (i, j)),
        out_shape=jax.ShapeDtypeStruct((T, N), F32),
        compiler_params=_params(("parallel", "parallel", "arbitrary"), 58),
        name="matmul_resid",
    )(*args)


def _attn_kernel(*refs, tq, dils, half, n_tiles):
    G = len(dils)
    cur_refs, prev_refs, next_refs = refs[0:G], refs[G:2 * G], refs[2 * G:3 * G]
    o_ref = refs[3 * G]
    acc_ref, m_ref, l_ref, out_ref, bias_ref = refs[3 * G + 1:]
    i = pl.program_id(2)
    has_prev = i > 0
    has_next = i < n_tiles - 1
    dmax = max(dils)

    for g, d in enumerate(dils):
        n_r = tq // d
        qs = min(128, n_r)
        ks = qs + 2 * half
        n_sub = n_r // qs
        qq = lax.broadcasted_iota(jnp.int32, (qs, ks), 0)
        kk = lax.broadcasted_iota(jnp.int32, (qs, ks), 1)
        band = (kk >= qq) & (kk <= qq + 2 * half)
        prev_ok = (kk >= half) | has_prev
        next_ok = (kk < ks - half) | has_next
        for idx, valid in enumerate((band & prev_ok, band, band & next_ok, band & prev_ok & next_ok)):
            bias_ref[idx, 0:qs, 0:ks] = jnp.where(valid, 0.0, -jnp.inf)

        def window(c, r, u, g=g, n_r=n_r, qs=qs):
            lo, hi = u * qs - half, (u + 1) * qs + half
            pieces = [prev_refs[g][c, r]] if lo < 0 else []
            pieces.append(cur_refs[g][c, r, max(lo, 0):min(hi, n_r), :])
            if hi > n_r:
                pieces.append(next_refs[g][c, r])
            return pieces[0] if len(pieces) == 1 else jnp.concatenate(pieces, axis=0)

        blocks = [(r, u) for r in range(d) for u in range(n_sub)]
        for b0 in range(0, len(blocks), ATTN_BATCH):
            batch = blocks[b0:b0 + ATTN_BATCH]
            edge = lambda u: (3 if n_sub == 1 else 0) if u == 0 else (2 if u == n_sub - 1 else 1)
            q = jnp.stack([cur_refs[g][2, r, u * qs:(u + 1) * qs, :] for r, u in batch])
            kw = jnp.stack([window(0, r, u) for r, u in batch])
            vw = jnp.stack([window(1, r, u) for r, u in batch])
            bias = jnp.stack([bias_ref[edge(u), 0:qs, 0:ks] for _, u in batch])
            s = jnp.einsum("bqd,bkd->bqk", q, kw, preferred_element_type=F32) + bias
            m_blk = jnp.max(s, axis=-1, keepdims=True)
            p = jnp.exp(s - m_blk)
            l_blk = jnp.sum(p, axis=-1, keepdims=True)
            pv = jnp.einsum("bqk,bkd->bqd", p.astype(BF16), vw, preferred_element_type=F32)
            st, pitch = _merge_pitch(dmax // d)
            for b, (r, u) in enumerate(batch):
                base = r * n_r + u * qs
                chunk = qs if st == pitch else st
                for k in range(qs // chunk):
                    src = slice(k * chunk, (k + 1) * chunk)
                    dst = pl.ds(base + k * chunk if st == pitch else (base // st + k) * pitch, chunk)
                    m_ref[g, dst, :] = m_blk[b][src]
                    l_ref[g, dst, :] = l_blk[b][src]
                    acc_ref[g, dst, :] = pv[b][src]

    n_max = tq // dmax
    out_pitch = n_max + SUBLANES
    for r in range(dmax):
        sel = []
        for d in dils:
            st, pitch = _merge_pitch(dmax // d)
            start = (r % d) * (tq // d // st) * pitch + r // d
            sel.append(pl.ds(start, n_max, stride=pitch * (dmax // d) // st) if d < dmax else pl.ds(start, n_max))
        ms = [m_ref[g, sel[g], :] for g in range(G)]
        m = functools.reduce(jnp.maximum, ms)
        ws = [jnp.exp(mg - m) for mg in ms]
        num = sum(ws[g] * acc_ref[g, sel[g], :] for g in range(G))
        den = sum(ws[g] * l_ref[g, sel[g], :] for g in range(G))
        out_ref[r * out_pitch:r * out_pitch + n_max, :] = num / den
    o_ref[...] = jnp.concatenate([out_ref[pl.ds(j, dmax, stride=out_pitch), :] for j in range(n_max)],
                                 axis=0).astype(o_ref.dtype)


def _merge_pitch(stride):
    if stride % SUBLANES:
        return SUBLANES, SUBLANES
    return stride, stride + SUBLANES


def window_attention(qkv, batch, seq_len, n_heads):
    G = len(GROUP_DILATIONS)
    tq = ATTN_TILE
    assert seq_len % tq == 0
    n_tiles = seq_len // tq
    total_tiles = batch * n_tiles
    half = GROUP_WINDOWS[0] // (2 * GROUP_DILATIONS[0])
    for wdw, d in zip(GROUP_WINDOWS, GROUP_DILATIONS):
        assert wdw // (2 * d) == half and half * d <= tq and tq % (d * min(128, tq // d)) == 0
    H = n_heads
    dils = GROUP_DILATIONS

    cur, prev, nxt = [], [], []
    for d in dils:
        cur.append(pl.BlockSpec((None, 3, None, d, tq // d, HEAD_DIM),
                                lambda b, h, i: (h, 0, b * n_tiles + i, 0, 0, 0)))
        prev.append(pl.BlockSpec((None, 2, None, d, half, HEAD_DIM),
                                 lambda b, h, i, d=d: (h, 0, jnp.maximum(b * n_tiles + i - 1, 0), 0,
                                                       tq // d // half - 1, 0)))
        nxt.append(pl.BlockSpec((None, 2, None, d, half, HEAD_DIM),
                                lambda b, h, i: (h, 0, jnp.minimum(b * n_tiles + i + 1, total_tiles - 1), 0, 0, 0)))
    views = [t.reshape(H, 3, total_tiles, d, tq // d, HEAD_DIM) for t, d in zip(qkv, dils)]
    merge_rows = max(tq // st * pitch for st, pitch in (_merge_pitch(max(dils) // d) for d in dils))
    in_specs = cur + prev + nxt
    return pl.pallas_call(
        functools.partial(_attn_kernel, tq=tq, dils=dils, half=half, n_tiles=n_tiles),
        grid=(batch, H, n_tiles),
        in_specs=in_specs,
        out_specs=pl.BlockSpec((tq, HEAD_DIM), lambda b, h, i: (b * n_tiles + i, h)),
        out_shape=jax.ShapeDtypeStruct((batch * seq_len, H * HEAD_DIM), BF16),
        scratch_shapes=[
            pltpu.VMEM((G, merge_rows, HEAD_DIM), F32),
            pltpu.VMEM((G, merge_rows, 1), F32),
            pltpu.VMEM((G, merge_rows, 1), F32),
            pltpu.VMEM((max(dils) * (tq // max(dils) + SUBLANES), HEAD_DIM), F32),
            pltpu.VMEM((4, 128, 128 + 2 * half), F32),
        ],
        compiler_params=_params(("parallel", "parallel", "parallel"), 40),
        name="window_attention",
    )(*(views * 3))


def _sconv_kernel(*refs, n_tiles):
    (x0m, x0p, x0n, x1m, x1p, x1n, vm, vp, vn, w0, w1, w2, b0, b1, b2, z_ref, x0_ref) = refs
    i = pl.program_id(1)

    def conv(m_ref, p_ref, n_ref, w_ref, b_ref):
        x = m_ref[...]
        ts = x.shape[0]
        prev_row = jnp.where(i > 0, p_ref[7:8, :], 0.0)
        next_row = jnp.where(i < n_tiles - 1, n_ref[0:1, :], 0.0)
        row = lax.broadcasted_iota(jnp.int32, x.shape, 0)
        up = jnp.where(row == 0, prev_row, pltpu.roll(x, 1, axis=0))
        dn = jnp.where(row == ts - 1, next_row, pltpu.roll(x, ts - 1, axis=0))
        return up * w_ref[0:1, :] + x * w_ref[1:2, :] + dn * w_ref[2:3, :] + b_ref[...]

    _store_grouped(x0_ref, conv(x0m, x0p, x0n, w0, b0))
    _store_grouped(z_ref, conv(vm, vp, vn, w2, b2) * conv(x1m, x1p, x1n, w1, b1))


def short_conv_gate(u, conv_w, conv_b, *, ts=512, ct=512):
    B, L, D3 = u.shape
    D = D3 // 3
    ts = _tile(L, ts, FFT_N2)
    ct = _tile(D, ct)
    n_tiles = L // ts
    n_ct = D // ct
    rows8 = ts // 8
    last8 = L // 8 - 1
    specs, args = [], []
    for part in range(3):
        off = part * n_ct
        specs += [
            pl.BlockSpec((None, ts, ct), lambda b, i, j, off=off: (b, i, off + j)),
            pl.BlockSpec((None, 8, ct), lambda b, i, j, off=off: (b, jnp.maximum(i * rows8 - 1, 0), off + j)),
            pl.BlockSpec((None, 8, ct), lambda b, i, j, off=off: (b, jnp.minimum((i + 1) * rows8, last8), off + j)),
        ]
        args += [u, u, u]
    for part in range(3):
        specs.append(pl.BlockSpec((3, ct), lambda b, i, j, off=part * n_ct: (0, off + j)))
        args.append(conv_w)
    for part in range(3):
        specs.append(pl.BlockSpec((1, ct), lambda b, i, j, off=part * n_ct: (0, off + j)))
        args.append(conv_b.reshape(1, D3))
    out_spec = pl.BlockSpec((None, FFT_J, ct // LANES, ts // FFT_N2, SUBLANES, LANES),
                            lambda b, i, j: (b, 0, j, i, 0, 0))
    return pl.pallas_call(
        functools.partial(_sconv_kernel, n_tiles=n_tiles),
        grid=(B, n_tiles, n_ct),
        in_specs=specs,
        out_specs=[out_spec, out_spec],
        out_shape=[jax.ShapeDtypeStruct((B, FFT_J, D // LANES, L // FFT_N2, SUBLANES, LANES), F32)] * 2,
        compiler_params=_params(("parallel", "parallel", "parallel"), 32),
        name="short_conv_gate",
    )(*args)


def _filter_kernel(feat_ref, tv_ref, w1, b1, w2, b2, w3, b3, fr_ref, w4, delta_ref, o_ref):
    fr = fr_ref[...]
    h = jnp.sin(fr * (_dot3(feat_ref[...], w1[...]) + b1[...]))
    h = jnp.sin(fr * (_dot3(h, w2[...]) + b2[...]))
    h = jnp.sin(fr * (_dot3(h, w3[...]) + b3[...]))
    out = _dot3(h, w4[...])
    t = tv_ref[:, 0:1]
    valid = tv_ref[:, 1:2]
    _store_grouped(o_ref, out * jnp.exp(-t * delta_ref[...]) * valid)


def conv_kernel_signal(L, D, fw1, fb1, fw2, fb2, fw3, fb3, ffreq, fw4, *, tl=256):
    FH = fw2.shape[0]
    FE = 64
    p = jnp.arange(2 * L, dtype=jnp.int32)
    pos = jnp.where(p < L, p, 2 * L - p).astype(F32)
    t = pos / (L - 1)
    bands = jnp.linspace(1e-4, FILTER_BANDS - 1, FILTER_BANDS, dtype=F32)
    ang = (2.0 * math.pi / L) * pos[:, None] * bands[None, :]
    feat = jnp.concatenate(
        [t[:, None], jnp.cos(ang), -jnp.sin(ang), jnp.zeros((2 * L, FE - FILTER_EMB), F32)], axis=-1)
    tv = jnp.stack([t, (p != L).astype(F32)], axis=-1)
    deltas = np.abs(np.linspace(math.log(DECAY_TARGET) / SLOW_DECAY_PCT, math.log(DECAY_TARGET) / FAST_DECAY_PCT, D,
                                dtype=np.float32)).reshape(1, D)
    w1p = jnp.pad(fw1.astype(F32), ((0, FE - FILTER_EMB), (0, 0)))
    tl = _tile(L, tl, FFT_N2)
    half_tiles = L // tl
    full = lambda shape: pl.BlockSpec(shape, lambda i: (0, 0))
    return pl.pallas_call(
        _filter_kernel,
        grid=(2 * L // tl,),
        in_specs=[
            pl.BlockSpec((tl, FE), lambda i: (i, 0)),
            pl.BlockSpec((tl, 2), lambda i: (i, 0)),
            full((FE, FH)), full((1, FH)), full((FH, FH)), full((1, FH)), full((FH, FH)), full((1, FH)),
            full((1, FH)),
            pl.BlockSpec((FH, D), lambda i: (0, i // half_tiles)),
            full((1, D)),
        ],
        out_specs=pl.BlockSpec((FFT_J, D // LANES, tl // FFT_N2, SUBLANES, LANES), lambda i: (0, 0, i, 0, 0)),
        out_shape=jax.ShapeDtypeStruct((FFT_J, D // LANES, 2 * L // FFT_N2, SUBLANES, LANES), F32),
        compiler_params=_params(("parallel",), 32),
        name="hyena_filter",
    )(feat, tv, w1p, fb1.reshape(1, FH).astype(F32), fw2.astype(F32), fb2.reshape(1, FH).astype(F32),
      fw3.astype(F32), fb3.reshape(1, FH).astype(F32), ffreq.reshape(1, FH).astype(F32), fw4.astype(F32),
      jnp.asarray(deltas))


def _const(m):
    return jnp.asarray(np.asarray(m, np.float64), dtype=F32).astype(BF16)


def _stack(re, im):
    return np.block([[re, -im], [im, re]])


def _dft_tables(n1, n2):
    n, k1 = n1 * n2, n1 // 2
    a1 = 2.0 * np.pi * np.outer(np.arange(n1), np.arange(n1)) / n1
    c1, s1 = np.cos(a1), np.sin(a1)
    a2 = 2.0 * np.pi * np.outer(np.arange(n2), np.arange(n2)) / n2
    c2, s2 = np.cos(a2), np.sin(a2)
    at = 2.0 * np.pi * np.outer(np.arange(n2), np.arange(n1)) / n
    bcast = lambda t: jnp.broadcast_to(jnp.asarray(t, F32)[:, :, None], t.shape + (LANES,))
    return dict(
        f1_real=np.concatenate([c1, -s1], axis=0),
        f1_cplx=_stack(c1[:, :k1], -s1[:, :k1]),
        f2=_stack(c2, -s2),
        g2=_stack(c2, s2),
        g1_cplx=_stack(c1[:k1], s1[:k1]),
        tw1=(bcast(np.cos(at)), bcast(-np.sin(at))),
        tw2=(bcast(np.cos(at.T)), bcast(np.sin(at.T))),
    )


def _fft1_kernel(x_ref, f_ref, twr_ref, twi_ref, ar_ref, ai_ref, *, n1):
    f = f_ref[...]
    parts, cb, rows = x_ref.shape[0], x_ref.shape[1], x_ref.shape[2] // SUBLANES
    for i in range(SUBLANES):
        sel = pl.ds(i, rows, stride=SUBLANES)
        x = jnp.concatenate([jnp.concatenate([x_ref[p, c, sel, :] for c in range(cb)], axis=1)
                             for p in range(parts)], axis=0)
        r = _dotc(f, x)
        re, im = r[:n1], r[n1:]
        twr = jnp.concatenate([twr_ref[i]] * cb, axis=1)
        twi = jnp.concatenate([twi_ref[i]] * cb, axis=1)
        ar = re * twr - im * twi
        ai = re * twi + im * twr
        for c in range(cb):
            ar_ref[c, pl.ds(i, n1, stride=SUBLANES), :] = ar[:, c * LANES:(c + 1) * LANES]
            ai_ref[c, pl.ds(i, n1, stride=SUBLANES), :] = ai[:, c * LANES:(c + 1) * LANES]


def _fft_stage1(x, n1, f1, tw1):
    P, parts, _, C, k8, _ = x.shape
    fc = _const(f1)
    cb = min(C, FFT_LANE_TILES)
    const = lambda shape: pl.BlockSpec(shape, lambda j, p, c: (0,) * len(shape))
    out_spec = pl.BlockSpec((None, None, cb, n1 * SUBLANES, LANES), lambda j, p, c: (p, j, c, 0, 0))
    tw_spec = pl.BlockSpec((SUBLANES, n1, LANES), lambda j, p, c: (j, 0, 0))
    return pl.pallas_call(
        functools.partial(_fft1_kernel, n1=n1),
        grid=(FFT_J, P, C // cb),
        in_specs=[
            pl.BlockSpec((None, parts, None, cb, k8, LANES), lambda j, p, c: (p, 0, j, c, 0, 0)),
            const(f1.shape), tw_spec, tw_spec,
        ],
        out_specs=[out_spec, out_spec],
        out_shape=[jax.ShapeDtypeStruct((P, FFT_J, C, n1 * SUBLANES, LANES), F32)] * 2,
        compiler_params=_params(("parallel", "parallel", "parallel"), 40),
        name="fft_stage1",
    )(x, fc, tw1[0], tw1[1])


def _fft2_kernel(*refs, n2, cc, scale, spectrum_only):
    if spectrum_only:
        ar_ref, ai_ref, f_ref, yr_ref, yi_ref = refs
    else:
        ar_ref, ai_ref, kr_ref, ki_ref, f_ref, g_ref, twr_ref, twi_ref, yr_ref, yi_ref = refs
        twr = jnp.concatenate([twr_ref[...]] * (cc // LANES), axis=1)
        twi = jnp.concatenate([twi_ref[...]] * (cc // LANES), axis=1)
    f = f_ref[...]
    lt = cc // LANES
    rows = lambda ref, c: jnp.concatenate([ref[:, c * lt + t].reshape(n2, LANES) for t in range(lt)], axis=1)
    for c in range(ar_ref.shape[1] // lt):
        cols = slice(c * cc, (c + 1) * cc)
        y = _dotc(f, jnp.concatenate([rows(ar_ref, c), rows(ai_ref, c)], axis=0))
        yr, yi = y[:n2], y[n2:]
        if spectrum_only:
            yr_ref[:, cols] = (yr * scale).astype(yr_ref.dtype)
            yi_ref[:, cols] = (yi * scale).astype(yi_ref.dtype)
        else:
            kr = kr_ref[:, cols].astype(F32)
            ki = ki_ref[:, cols].astype(F32)
            pr = yr * kr - yi * ki
            pi = yr * ki + yi * kr
            q = _dotc(g_ref[...], jnp.concatenate([pr, pi], axis=0))
            qr, qi = q[:n2], q[n2:]
            outr = qr * twr - qi * twi
            outi = qr * twi + qi * twr
            for t in range(lt):
                yr_ref[:, c * lt + t] = outr[:, t * LANES:(t + 1) * LANES].reshape(FFT_J, SUBLANES, LANES)
                yi_ref[:, c * lt + t] = outi[:, t * LANES:(t + 1) * LANES].reshape(FFT_J, SUBLANES, LANES)


def _fft_stage2(ar, ai, n1, n2, D, f2, *, kf=None, g2=None, tw2=None, scale=1.0):
    B, _, C = ar.shape[:3]
    ar = ar.reshape(B, FFT_J, C, n1, SUBLANES, LANES)
    ai = ai.reshape(B, FFT_J, C, n1, SUBLANES, LANES)
    fc = _const(f2)
    cc = _tile(D, 512)
    blk = pl.BlockSpec((None, FFT_J, C, None, SUBLANES, LANES), lambda f, b: (b, 0, 0, f, 0, 0))
    const = lambda shape: pl.BlockSpec(shape, lambda f, b: (0,) * len(shape))
    specs = [blk, blk]
    args = [ar, ai]
    if kf is not None:
        gc = _const(g2)
        kblk = pl.BlockSpec((None, n2, D), lambda f, b: (f, 0, 0))
        tblk = pl.BlockSpec((None, n2, LANES), lambda f, b: (f, 0, 0))
        specs += [kblk, kblk, const((2 * n2, 2 * n2)), const((2 * n2, 2 * n2)), tblk, tblk]
        args += [kf[0], kf[1], fc, gc, tw2[0], tw2[1]]
        out_spec = blk
        out_shape = jax.ShapeDtypeStruct((B, FFT_J, C, n1, SUBLANES, LANES), F32)
    else:
        specs += [const((2 * n2, 2 * n2))]
        args += [fc]
        out_spec = pl.BlockSpec((None, None, n2, D), lambda f, b: (b, f, 0, 0))
        out_shape = jax.ShapeDtypeStruct((B, n1, n2, D), BF16)
    out = pl.pallas_call(
        functools.partial(_fft2_kernel, n2=n2, cc=cc, scale=scale, spectrum_only=kf is None),
        grid=(n1, B),
        in_specs=specs,
        out_specs=[out_spec, out_spec],
        out_shape=[out_shape] * 2,
        compiler_params=_params(("parallel", "parallel"), 40),
        name="fft_stage2",
    )(*args)
    if kf is not None:
        out = [o.reshape(B, FFT_J, C, n1 * SUBLANES, LANES) for o in out]
    return out


def _fft3_kernel(qr_ref, qi_ref, g_ref, z_ref, x0_ref, skip_ref, o_ref, *, n1):
    g = g_ref[...]
    cb, rows = z_ref.shape[1], z_ref.shape[2] // SUBLANES
    for i in range(SUBLANES):
        qsel = pl.ds(i, n1, stride=SUBLANES)
        sel = pl.ds(i, rows, stride=SUBLANES)
        q = jnp.concatenate([jnp.concatenate([ref[c, qsel, :] for c in range(cb)], axis=1)
                             for ref in (qr_ref, qi_ref)], axis=0)
        y = _dotc(g, q)
        for p in range(2):
            for c in range(cb):
                lanes = slice(c * LANES, (c + 1) * LANES)
                conv = y[p * rows:(p + 1) * rows, lanes]
                o_ref[p, c, sel, :] = (conv + z_ref[p, c, sel, :] * skip_ref[:, lanes]) * x0_ref[p, c, sel, :]


def _fft_stage3(qr, qi, z, x0, skip, n1, g1):
    P, _, _, C, k8, _ = z.shape
    gc = _const(g1)
    cb = min(C, FFT_LANE_TILES)
    qblk = pl.BlockSpec((None, None, cb, n1 * SUBLANES, LANES), lambda p, j, c: (p, j, c, 0, 0))
    zblk = pl.BlockSpec((None, 2, None, cb, k8, LANES), lambda p, j, c: (p, 0, j, c, 0, 0))
    const = lambda shape: pl.BlockSpec(shape, lambda p, j, c: (0,) * len(shape))
    return pl.pallas_call(
        functools.partial(_fft3_kernel, n1=n1),
        grid=(P, FFT_J, C // cb),
        in_specs=[qblk, qblk, const(g1.shape), zblk, zblk,
                  pl.BlockSpec((1, cb * LANES), lambda p, j, c: (0, c))],
        out_specs=zblk,
        out_shape=jax.ShapeDtypeStruct(z.shape, F32),
        compiler_params=_params(("parallel", "parallel", "parallel"), 52),
        name="fft_stage3",
    )(qr, qi, gc, z, x0, skip.reshape(1, C * LANES).astype(F32))


def long_conv_gate(z, x0, skip, kern):
    B, _, C, k1 = z.shape[:4]
    D = C * LANES
    n2 = FFT_N2
    n1 = 2 * k1
    assert B % 2 == 0
    t = _dft_tables(n1, n2)
    ka = _fft_stage1(kern.reshape(1, 1, FFT_J, C, n1 * SUBLANES, LANES), n1, t["f1_real"], t["tw1"])
    kf = _fft_stage2(ka[0], ka[1], n1, n2, D, t["f2"], scale=1.0 / (n1 * n2))
    kf = (kf[0].reshape(n1, n2, D), kf[1].reshape(n1, n2, D))
    pairs = lambda v: v.reshape(B // 2, 2, FFT_J, C, k1 * SUBLANES, LANES)
    a = _fft_stage1(pairs(z), n1, t["f1_cplx"], t["tw1"])
    q = _fft_stage2(a[0], a[1], n1, n2, D, t["f2"], kf=kf, g2=t["g2"], tw2=t["tw2"])
    return _fft_stage3(q[0], q[1], pairs(z), pairs(x0), skip, n1, t["g1_cplx"]).reshape(z.shape)


def _rope_tables(S):
    inv = ROPE_THETA ** (-jnp.arange(0, HEAD_DIM, 2, dtype=F32) / HEAD_DIM)
    ang = jnp.arange(S, dtype=F32)[:, None] * inv[None, :]
    cos, sin = jnp.cos(ang), jnp.sin(ang)
    cos2, sin2 = jnp.concatenate([cos, cos], axis=-1), jnp.concatenate([-sin, sin], axis=-1)
    return jnp.stack([cos2, jnp.ones_like(cos2)]), jnp.stack([sin2, jnp.zeros_like(sin2)])


def _trunk(x, p):
    B, S, D = x.shape
    H = D // HEAD_DIM
    G = len(GROUP_DILATIONS)
    T = B * S
    cos, sin = _rope_tables(S)
    x = x.reshape(T, D)
    depth = p["mix_norm"].shape[0]
    for i in range(depth):
        j = i // 2
        if i % 2 == 0:
            scale = HEAD_DIM ** -0.5
            gains = jnp.stack([p["attn_q_gain"][j] * scale, p["attn_k_gain"][j], jnp.ones_like(p["attn_k_gain"][j])],
                              axis=1).reshape(G * 3, 1, HEAD_DIM).astype(F32)
            qkv = [qkv_project(x, p["mix_norm"][i], p["attn_w_qkv"], j, gains, cos, sin, S, g) for g in range(G)]
            o = window_attention(qkv, B, S, H)
            x = matmul_resid(o, p["attn_w_out"], j, x, tm=512, tn=2048)
        else:
            u = norm_matmul(x, p["mix_norm"][i], p["hy_w_in"], j, p["hy_b_in"][j])
            z, x0 = short_conv_gate(u.reshape(B, S, 3 * D), p["hy_conv_w"][j], p["hy_conv_b"][j])
            kern = conv_kernel_signal(S, D, p["hy_filt_w1"][j], p["hy_filt_b1"][j], p["hy_filt_w2"][j],
                                      p["hy_filt_b2"][j], p["hy_filt_w3"][j], p["hy_filt_b3"][j],
                                      p["hy_filt_freq"][j], p["hy_filt_w4"][j])
            y = long_conv_gate(z, x0, p["hy_skip"][j], kern)
            x = matmul_resid(y, p["hy_w_out"], j, x, p["hy_b_out"][j], tm=512, tn=2048)
        h = norm_matmul(x, p["mlp_norm"][i], p["mlp_w1"], i, relu2=True, out_dtype=BF16, tn=2048)
        x = matmul_resid(h, p["mlp_w2"], i, x, tk=4096)
    return x.reshape(B, S, D)


def kernel(x_prompt, x_sample, mix_norm, mlp_norm, attn_w_qkv, attn_q_gain, attn_k_gain, attn_w_out, hy_w_in, hy_b_in, hy_conv_w, hy_conv_b, hy_filt_w1, hy_filt_b1, hy_filt_w2, hy_filt_b2, hy_filt_w3, hy_filt_b3, hy_filt_freq, hy_filt_w4, hy_skip, hy_w_out, hy_b_out, mlp_w1, mlp_w2):
    p = dict(
        mix_norm=mix_norm, mlp_norm=mlp_norm,
        attn_w_qkv=attn_w_qkv.astype(BF16), attn_q_gain=attn_q_gain, attn_k_gain=attn_k_gain,
        attn_w_out=attn_w_out.astype(BF16),
        hy_w_in=hy_w_in.astype(BF16), hy_b_in=hy_b_in, hy_conv_w=hy_conv_w, hy_conv_b=hy_conv_b,
        hy_filt_w1=hy_filt_w1, hy_filt_b1=hy_filt_b1, hy_filt_w2=hy_filt_w2, hy_filt_b2=hy_filt_b2,
        hy_filt_w3=hy_filt_w3, hy_filt_b3=hy_filt_b3, hy_filt_freq=hy_filt_freq, hy_filt_w4=hy_filt_w4,
        hy_skip=hy_skip, hy_w_out=hy_w_out.astype(BF16), hy_b_out=hy_b_out,
        mlp_w1=mlp_w1.astype(BF16), mlp_w2=mlp_w2.astype(BF16),
    )
    return (_trunk(x_prompt, p), _trunk(x_sample, p))
```

```python
import functools
import math

import numpy as np
import jax
import jax.numpy as jnp
from jax import lax
from jax.experimental import pallas as pl
from jax.experimental.pallas import tpu as pltpu

F32 = jnp.float32
BF16 = jnp.bfloat16

HEAD_DIM = 128
GROUP_WINDOWS = (128, 512, 2048)
GROUP_DILATIONS = (1, 4, 16)
ROPE_THETA = 10000.0
FILTER_EMB = 33
FILTER_BANDS = (FILTER_EMB - 1) // 2
DECAY_TARGET = 1e-2
FAST_DECAY_PCT = 0.3
SLOW_DECAY_PCT = 1.5
EPS = 1e-6

LANES = 128
SUBLANES = 8
ATTN_TILE = 1024
ATTN_BATCH = 8
FFT_N2 = 128
FFT_J = FFT_N2 // SUBLANES
FFT_LANE_TILES = 4
MIB = 1024 * 1024


def _params(semantics, vmem_mib):
    return pltpu.CompilerParams(dimension_semantics=semantics, vmem_limit_bytes=vmem_mib * MIB)


def _tile(n, pref, quantum=LANES):
    if n <= pref:
        return n
    t = (pref // quantum) * quantum
    while t > quantum and n % t:
        t -= quantum
    assert n % t == 0, (n, pref)
    return t


def _split(x):
    hi = x.astype(BF16)
    lo = (x - hi.astype(F32)).astype(BF16)
    return hi, lo


def _dot(a, b):
    return jnp.dot(a, b, preferred_element_type=F32)


def _dot3(a, b):
    ah, al = _split(a)
    bh, bl = _split(b)
    return _dot(ah, bh) + (_dot(ah, bl) + _dot(al, bh))


def _dotc(c, b):
    return _dot(c, b.astype(BF16))


def _rms_rows_to(x_ref, g_ref, xn_ref, row_chunk):
    def body(c, carry):
        rows = pl.ds(pl.multiple_of(c * row_chunk, row_chunk), row_chunk)
        x = x_ref[rows, :]
        inv = lax.rsqrt(jnp.mean(x * x, axis=-1, keepdims=True) + EPS)
        xn_ref[rows, :] = (x * inv * g_ref[...]).astype(BF16)
        return carry

    lax.fori_loop(0, x_ref.shape[0] // row_chunk, body, 0)


def _store_grouped(o_ref, v):
    for c in range(o_ref.shape[1]):
        for a in range(o_ref.shape[2]):
            o_ref[:, c, a] = v[a * FFT_N2:(a + 1) * FFT_N2, c * LANES:(c + 1) * LANES].reshape(FFT_J, SUBLANES, LANES)


def _load_grouped(a_ref):
    rows = [jnp.concatenate([a_ref[:, c, s].reshape(FFT_N2, LANES) for c in range(a_ref.shape[1])], axis=1)
            for s in range(a_ref.shape[2])]
    return rows[0] if len(rows) == 1 else jnp.concatenate(rows, axis=0)


def _norm_mm_kernel(x_ref, g_ref, w_ref, *rest, has_bias, relu2, row_chunk):
    if has_bias:
        b_ref, o_ref, xn_ref = rest
    else:
        o_ref, xn_ref = rest

    @pl.when(pl.program_id(1) == 0)
    def _():
        _rms_rows_to(x_ref, g_ref, xn_ref, row_chunk)

    acc = _dot(xn_ref[...], w_ref[...])
    if has_bias:
        acc = acc + b_ref[...]
    if relu2:
        acc = jnp.maximum(acc, 0.0)
        acc = acc * acc
    o_ref[...] = acc.astype(o_ref.dtype)


def norm_matmul(x, g, w, layer, bias=None, *, relu2=False, out_dtype=F32, tm=1024, tn=1024):
    T, K = x.shape
    N = w.shape[2]
    tm = _tile(T, tm)
    tn = _tile(N, tn)
    in_specs = [
        pl.BlockSpec((tm, K), lambda i, j: (i, 0)),
        pl.BlockSpec((1, K), lambda i, j: (0, 0)),
        pl.BlockSpec((None, K, tn), lambda i, j: (layer, 0, j)),
    ]
    args = [x, g.reshape(1, K), w]
    if bias is not None:
        in_specs.append(pl.BlockSpec((1, tn), lambda i, j: (0, j)))
        args.append(bias.reshape(1, N))
    return pl.pallas_call(
        functools.partial(_norm_mm_kernel, has_bias=bias is not None, relu2=relu2, row_chunk=min(tm, 128)),
        grid=(T // tm, N // tn),
        in_specs=in_specs,
        out_specs=pl.BlockSpec((tm, tn), lambda i, j: (i, j)),
        out_shape=jax.ShapeDtypeStruct((T, N), out_dtype),
        scratch_shapes=[pltpu.VMEM((tm, K), BF16)],
        compiler_params=_params(("parallel", "arbitrary"), 56),
        name="norm_matmul",
    )(*args)


def _qkv_kernel(x_ref, g_ref, w_ref, gain_ref, cos_ref, sin_ref, o_ref, xn_ref, acc_a, acc_b, stage_ref, *,
                row_chunk, tiles_per_part, n_j, n_tiles, d):
    n = pl.program_id(0)
    tile = jnp.minimum(n, n_tiles - 1)
    done = jnp.maximum(n - 1, 0)
    tm = xn_ref.shape[0]

    @pl.when(n == 0)
    def _():
        acc_b[...] = jnp.zeros_like(acc_b)

    @pl.when((tile % n_j == 0) & (n < n_tiles))
    def _():
        _rms_rows_to(x_ref, g_ref, xn_ref, row_chunk)

    def step(acc_w, acc_r):
        acc_w[...] = _dot(xn_ref[...], w_ref[...])
        is_v = (done % n_j) // tiles_per_part == 2
        gain = gain_ref[...]
        cos = cos_ref[...]
        sin = sin_ref[...]
        for s in range(acc_r.shape[1] // HEAD_DIM):
            a = acc_r[:, s * HEAD_DIM:(s + 1) * HEAD_DIM]
            inv = jnp.where(is_v, 1.0, lax.rsqrt(jnp.mean(a * a, axis=-1, keepdims=True) + EPS))
            y = a * inv * gain
            y = y * cos + pltpu.roll(y, HEAD_DIM // 2, axis=1) * sin
            if d == 1:
                o_ref[s] = y.astype(o_ref.dtype)
            else:
                stage_ref[s] = y
                n_r = tm // d
                for r in range(d):
                    o_ref[s, r * n_r:(r + 1) * n_r, :] = stage_ref[s, pl.ds(r, n_r, stride=d), :].astype(o_ref.dtype)

    @pl.when(n % 2 == 0)
    def _():
        step(acc_a, acc_b)

    @pl.when(n % 2 == 1)
    def _():
        step(acc_b, acc_a)


def qkv_project(x, g, w, layer, gains, cos, sin, seq_len, group, *, tn=1024):
    T, K = x.shape
    part_width = w.shape[2] // (len(GROUP_DILATIONS) * 3)
    tm = ATTN_TILE
    assert seq_len % tm == 0
    tn = _tile(part_width, tn)
    pos_tiles = seq_len // tm
    tiles_per_part = part_width // tn
    n_j = 3 * tiles_per_part
    n_tiles = (T // tm) * n_j
    mm = lambda n: jnp.minimum(n, n_tiles - 1)
    ep = lambda n: jnp.maximum(n - 1, 0)
    table = lambda n: ((ep(n) % n_j) // tiles_per_part // 2, (ep(n) // n_j) % pos_tiles, 0)
    return pl.pallas_call(
        functools.partial(_qkv_kernel, row_chunk=min(tm, 128), tiles_per_part=tiles_per_part, n_j=n_j,
                          n_tiles=n_tiles, d=GROUP_DILATIONS[group]),
        grid=(n_tiles + 1,),
        in_specs=[
            pl.BlockSpec((tm, K), lambda n: (mm(n) // n_j, 0)),
            pl.BlockSpec((1, K), lambda n: (0, 0)),
            pl.BlockSpec((None, K, tn), lambda n: (layer, 0, group * n_j + mm(n) % n_j)),
            pl.BlockSpec((None, 1, HEAD_DIM), lambda n: (group * 3 + (ep(n) % n_j) // tiles_per_part, 0, 0)),
            pl.BlockSpec((None, tm, HEAD_DIM), table),
            pl.BlockSpec((None, tm, HEAD_DIM), table),
        ],
        out_specs=pl.BlockSpec(
            (tn // HEAD_DIM, None, tm, HEAD_DIM),
            lambda n: (ep(n) % tiles_per_part, ((ep(n) % n_j) // tiles_per_part + 2) % 3, ep(n) // n_j, 0)),
        out_shape=jax.ShapeDtypeStruct((part_width // HEAD_DIM, 3, T, HEAD_DIM), BF16),
        scratch_shapes=[pltpu.VMEM((tm, K), BF16), pltpu.VMEM((tm, tn), F32), pltpu.VMEM((tm, tn), F32),
                        pltpu.VMEM((tn // HEAD_DIM, tm, HEAD_DIM), F32)],
        compiler_params=_params(("arbitrary",), 52),
        name="qkv_project",
    )(x, g.reshape(1, K), w, gains, cos, sin)


def _mm_resid_kernel(a_ref, w_ref, *rest, has_bias, grouped):
    if has_bias:
        b_ref, r_ref, o_ref = rest
    else:
        r_ref, o_ref = rest
    k = pl.program_id(2)
    a = _load_grouped(a_ref) if grouped else a_ref[...]
    part = _dot(a.astype(BF16), w_ref[...])

    @pl.when(k == 0)
    def _():
        first = part + r_ref[...]
        if has_bias:
            first = first + b_ref[...]
        o_ref[...] = first

    @pl.when(k > 0)
    def _():
        o_ref[...] += part


def matmul_resid(a, w, layer, resid, bias=None, *, tm=1024, tn=1024, tk=2048):
    grouped = a.ndim == 6
    T, N = resid.shape
    K = w.shape[1]
    tn = _tile(N, tn)
    tk = _tile(K, tk)
    if grouped:
        seq_len = a.shape[3] * FFT_N2
        tm = _tile(seq_len, tm)
        seq_tiles = seq_len // tm
        a_spec = pl.BlockSpec((None, FFT_J, tk // LANES, tm // FFT_N2, SUBLANES, LANES),
                              lambda i, j, k: (i // seq_tiles, 0, k, i % seq_tiles, 0, 0))
    else:
        tm = _tile(T, tm)
        a_spec = pl.BlockSpec((tm, tk), lambda i, j, k: (i, k))
    in_specs = [
        a_spec,
        pl.BlockSpec((None, tk, tn), lambda i, j, k: (layer, k, j)),
    ]
    args = [a, w]
    if bias is not None:
        in_specs.append(pl.BlockSpec((1, tn), lambda i, j, k: (0, j)))
        args.append(bias.reshape(1, N))
    in_specs.append(pl.BlockSpec((tm, tn), lambda i, j, k: (i, j)))
    args.append(resid)
    return pl.pallas_call(
        functools.partial(_mm_resid_kernel, has_bias=bias is not None, grouped=grouped),
        grid=(T // tm, N // tn, K // tk),
        in_specs=in_specs,
        out_specs=pl.BlockSpec((tm, tn), lambda i, j, k: (i, j)),
        out_shape=jax.ShapeDtypeStruct((T, N), F32),
        compiler_params=_params(("parallel", "parallel", "arbitrary"), 58),
        name="matmul_resid",
    )(*args)


def _attn_kernel(*refs, tq, dils, half, n_tiles):
    G = len(dils)
    cur_refs, prev_refs, next_refs = refs[0:G], refs[G:2 * G], refs[2 * G:3 * G]
    o_ref = refs[3 * G]
    acc_ref, m_ref, l_ref, out_ref, bias_ref = refs[3 * G + 1:]
    i = pl.program_id(2)
    has_prev = i > 0
    has_next = i < n_tiles - 1
    dmax = max(dils)

    for g, d in enumerate(dils):
        n_r = tq // d
        qs = min(128, n_r)
        ks = qs + 2 * half
        n_sub = n_r // qs
        qq = lax.broadcasted_iota(jnp.int32, (qs, ks), 0)
        kk = lax.broadcasted_iota(jnp.int32, (qs, ks), 1)
        band = (kk >= qq) & (kk <= qq + 2 * half)
        prev_ok = (kk >= half) | has_prev
        next_ok = (kk < ks - half) | has_next
        for idx, valid in enumerate((band & prev_ok, band, band & next_ok, band & prev_ok & next_ok)):
            bias_ref[idx, 0:qs, 0:ks] = jnp.where(valid, 0.0, -jnp.inf)

        def window(c, r, u, g=g, n_r=n_r, qs=qs):
            lo, hi = u * qs - half, (u + 1) * qs + half
            pieces = [prev_refs[g][c, r]] if lo < 0 else []
            pieces.append(cur_refs[g][c, r, max(lo, 0):min(hi, n_r), :])
            if hi > n_r:
                pieces.append(next_refs[g][c, r])
            return pieces[0] if len(pieces) == 1 else jnp.concatenate(pieces, axis=0)

        blocks = [(r, u) for r in range(d) for u in range(n_sub)]
        for b0 in range(0, len(blocks), ATTN_BATCH):
            batch = blocks[b0:b0 + ATTN_BATCH]
            edge = lambda u: (3 if n_sub == 1 else 0) if u == 0 else (2 if u == n_sub - 1 else 1)
            q = jnp.stack([cur_refs[g][2, r, u * qs:(u + 1) * qs, :] for r, u in batch])
            kw = jnp.stack([window(0, r, u) for r, u in batch])
            vw = jnp.stack([window(1, r, u) for r, u in batch])
            bias = jnp.stack([bias_ref[edge(u), 0:qs, 0:ks] for _, u in batch])
            s = jnp.einsum("bqd,bkd->bqk", q, kw, preferred_element_type=F32) + bias
            m_blk = jnp.max(s, axis=-1, keepdims=True)
            p = jnp.exp(s - m_blk)
            l_blk = jnp.sum(p, axis=-1, keepdims=True)
            pv = jnp.einsum("bqk,bkd->bqd", p.astype(BF16), vw, preferred_element_type=F32)
            st, pitch = _merge_pitch(dmax // d)
            for b, (r, u) in enumerate(batch):
                base = r * n_r + u * qs
                chunk = qs if st == pitch else st
                for k in range(qs // chunk):
                    src = slice(k * chunk, (k + 1) * chunk)
                    dst = pl.ds(base + k * chunk if st == pitch else (base // st + k) * pitch, chunk)
                    m_ref[g, dst, :] = m_blk[b][src]
                    l_ref[g, dst, :] = l_blk[b][src]
                    acc_ref[g, dst, :] = pv[b][src]

    n_max = tq // dmax
    out_pitch = n_max + SUBLANES
    for r in range(dmax):
        sel = []
        for d in dils:
            st, pitch = _merge_pitch(dmax // d)
            start = (r % d) * (tq // d // st) * pitch + r // d
            sel.append(pl.ds(start, n_max, stride=pitch * (dmax // d) // st) if d < dmax else pl.ds(start, n_max))
        ms = [m_ref[g, sel[g], :] for g in range(G)]
        m = functools.reduce(jnp.maximum, ms)
        ws = [jnp.exp(mg - m) for mg in ms]
        num = sum(ws[g] * acc_ref[g, sel[g], :] for g in range(G))
        den = sum(ws[g] * l_ref[g, sel[g], :] for g in range(G))
        out_ref[r * out_pitch:r * out_pitch + n_max, :] = num / den
    o_ref[...] = jnp.concatenate([out_ref[pl.ds(j, dmax, stride=out_pitch), :] for j in range(n_max)],
                                 axis=0).astype(o_ref.dtype)


def _merge_pitch(stride):
    if stride % SUBLANES:
        return SUBLANES, SUBLANES
    return stride, stride + SUBLANES


def window_attention(qkv, batch, seq_len, n_heads):
    G = len(GROUP_DILATIONS)
    tq = ATTN_TILE
    assert seq_len % tq == 0
    n_tiles = seq_len // tq
    total_tiles = batch * n_tiles
    half = GROUP_WINDOWS[0] // (2 * GROUP_DILATIONS[0])
    for wdw, d in zip(GROUP_WINDOWS, GROUP_DILATIONS):
        assert wdw // (2 * d) == half and half * d <= tq and tq % (d * min(128, tq // d)) == 0
    H = n_heads
    dils = GROUP_DILATIONS

    cur, prev, nxt = [], [], []
    for d in dils:
        cur.append(pl.BlockSpec((None, 3, None, d, tq // d, HEAD_DIM),
                                lambda b, h, i: (h, 0, b * n_tiles + i, 0, 0, 0)))
        prev.append(pl.BlockSpec((None, 2, None, d, half, HEAD_DIM),
                                 lambda b, h, i, d=d: (h, 0, jnp.maximum(b * n_tiles + i - 1, 0), 0,
                                                       tq // d // half - 1, 0)))
        nxt.append(pl.BlockSpec((None, 2, None, d, half, HEAD_DIM),
                                lambda b, h, i: (h, 0, jnp.minimum(b * n_tiles + i + 1, total_tiles - 1), 0, 0, 0)))
    views = [t.reshape(H, 3, total_tiles, d, tq // d, HEAD_DIM) for t, d in zip(qkv, dils)]
    merge_rows = max(tq // st * pitch for st, pitch in (_merge_pitch(max(dils) // d) for d in dils))
    in_specs = cur + prev + nxt
    return pl.pallas_call(
        functools.partial(_attn_kernel, tq=tq, dils=dils, half=half, n_tiles=n_tiles),
        grid=(batch, H, n_tiles),
        in_specs=in_specs,
        out_specs=pl.BlockSpec((tq, HEAD_DIM), lambda b, h, i: (b * n_tiles + i, h)),
        out_shape=jax.ShapeDtypeStruct((batch * seq_len, H * HEAD_DIM), BF16),
        scratch_shapes=[
            pltpu.VMEM((G, merge_rows, HEAD_DIM), F32),
            pltpu.VMEM((G, merge_rows, 1), F32),
            pltpu.VMEM((G, merge_rows, 1), F32),
            pltpu.VMEM((max(dils) * (tq // max(dils) + SUBLANES), HEAD_DIM), F32),
            pltpu.VMEM((4, 128, 128 + 2 * half), F32),
        ],
        compiler_params=_params(("parallel", "parallel", "parallel"), 40),
        name="window_attention",
    )(*(views * 3))


def _sconv_kernel(*refs, n_tiles):
    (x0m, x0p, x0n, x1m, x1p, x1n, vm, vp, vn, w0, w1, w2, b0, b1, b2, z_ref, x0_ref) = refs
    i = pl.program_id(1)

    def conv(m_ref, p_ref, n_ref, w_ref, b_ref):
        x = m_ref[...]
        ts = x.shape[0]
        prev_row = jnp.where(i > 0, p_ref[7:8, :], 0.0)
        next_row = jnp.where(i < n_tiles - 1, n_ref[0:1, :], 0.0)
        row = lax.broadcasted_iota(jnp.int32, x.shape, 0)
        up = jnp.where(row == 0, prev_row, pltpu.roll(x, 1, axis=0))
        dn = jnp.where(row == ts - 1, next_row, pltpu.roll(x, ts - 1, axis=0))
        return up * w_ref[0:1, :] + x * w_ref[1:2, :] + dn * w_ref[2:3, :] + b_ref[...]

    _store_grouped(x0_ref, conv(x0m, x0p, x0n, w0, b0))
    _store_grouped(z_ref, conv(vm, vp, vn, w2, b2) * conv(x1m, x1p, x1n, w1, b1))


def short_conv_gate(u, conv_w, conv_b, *, ts=512, ct=512):
    B, L, D3 = u.shape
    D = D3 // 3
    ts = _tile(L, ts, FFT_N2)
    ct = _tile(D, ct)
    n_tiles = L // ts
    n_ct = D // ct
    rows8 = ts // 8
    last8 = L // 8 - 1
    specs, args = [], []
    for part in range(3):
        off = part * n_ct
        specs += [
            pl.BlockSpec((None, ts, ct), lambda b, i, j, off=off: (b, i, off + j)),
            pl.BlockSpec((None, 8, ct), lambda b, i, j, off=off: (b, jnp.maximum(i * rows8 - 1, 0), off + j)),
            pl.BlockSpec((None, 8, ct), lambda b, i, j, off=off: (b, jnp.minimum((i + 1) * rows8, last8), off + j)),
        ]
        args += [u, u, u]
    for part in range(3):
        specs.append(pl.BlockSpec((3, ct), lambda b, i, j, off=part * n_ct: (0, off + j)))
        args.append(conv_w)
    for part in range(3):
        specs.append(pl.BlockSpec((1, ct), lambda b, i, j, off=part * n_ct: (0, off + j)))
        args.append(conv_b.reshape(1, D3))
    out_spec = pl.BlockSpec((None, FFT_J, ct // LANES, ts // FFT_N2, SUBLANES, LANES),
                            lambda b, i, j: (b, 0, j, i, 0, 0))
    return pl.pallas_call(
        functools.partial(_sconv_kernel, n_tiles=n_tiles),
        grid=(B, n_tiles, n_ct),
        in_specs=specs,
        out_specs=[out_spec, out_spec],
        out_shape=[jax.ShapeDtypeStruct((B, FFT_J, D // LANES, L // FFT_N2, SUBLANES, LANES), F32)] * 2,
        compiler_params=_params(("parallel", "parallel", "parallel"), 32),
        name="short_conv_gate",
    )(*args)


CONV_HALO = 16
CONV_ROWS = 32


def _inproj_kernel(x_ref, xp_ref, xq_ref, g_ref, w0_ref, w1_ref, w2_ref, b_ref, cw_ref, cb_ref, z_ref, x0_ref,
                   xn_ref, acc_a, acc_b, *, row_chunk, n_j, n_tiles, seq_tiles):
    n = pl.program_id(0)
    tile = jnp.minimum(n, n_tiles - 1)
    done = jnp.maximum(n - 1, 0)
    tm = x_ref.shape[0]
    h = CONV_HALO

    @pl.when(n == 0)
    def _():
        acc_b[...] = jnp.zeros_like(acc_b)

    @pl.when((tile % n_j == 0) & (n < n_tiles))
    def _():
        _rms_rows_to(xp_ref, g_ref, xn_ref.at[0:h], h)
        _rms_rows_to(x_ref, g_ref, xn_ref.at[h:h + tm], row_chunk)
        _rms_rows_to(xq_ref, g_ref, xn_ref.at[h + tm:2 * h + tm], h)

    def step(acc_w, acc_r):
        pos = (done // n_j) % seq_tiles
        rc = CONV_ROWS
        row = lax.broadcasted_iota(jnp.int32, (rc, 1), 0)
        for t0 in range(0, tm, rc):

            def conv(p):
                b = b_ref[p]
                up = acc_r[p, h + t0 - 1:h + t0 - 1 + rc, :] + b
                dn = acc_r[p, h + t0 + 1:h + t0 + 1 + rc, :] + b
                if t0 == 0:
                    up = jnp.where((row == 0) & (pos == 0), 0.0, up)
                if t0 + rc == tm:
                    dn = jnp.where((row == rc - 1) & (pos == seq_tiles - 1), 0.0, dn)
                cw = cw_ref[p]
                return up * cw[0:1] + (acc_r[p, h + t0:h + t0 + rc, :] + b) * cw[1:2] + dn * cw[2:3] + cb_ref[p]

            x0 = conv(0)
            z = conv(2) * conv(1)
            a, j0 = t0 // FFT_N2, (t0 % FFT_N2) // SUBLANES
            for c in range(z_ref.shape[1]):
                lanes = slice(c * LANES, (c + 1) * LANES)
                x0_ref[j0:j0 + rc // SUBLANES, c, a] = x0[:, lanes].reshape(rc // SUBLANES, SUBLANES, LANES)
                z_ref[j0:j0 + rc // SUBLANES, c, a] = z[:, lanes].reshape(rc // SUBLANES, SUBLANES, LANES)
        xn = xn_ref[...]
        for p, w_ref in enumerate((w0_ref, w1_ref, w2_ref)):
            acc_w[p] = _dot(xn, w_ref[...])

    @pl.when(n % 2 == 0)
    def _():
        step(acc_a, acc_b)

    @pl.when(n % 2 == 1)
    def _():
        step(acc_b, acc_a)


def inproj_conv_gate(x, g, w, layer, b_in, conv_w, conv_b, batch, seq_len, *, tm=1024, ct=256):
    T, K = x.shape
    D = w.shape[2] // 3
    tm = _tile(seq_len, tm, FFT_N2)
    ct = _tile(D, ct)
    h = CONV_HALO
    n_j = D // ct
    seq_tiles = seq_len // tm
    n_tiles = (T // tm) * n_j
    halo_per_tile = tm // h
    last_halo = T // h - 1
    mm = lambda n: jnp.minimum(n, n_tiles - 1) // n_j
    ep = lambda n: jnp.maximum(n - 1, 0)
    parts = lambda a: a.reshape(3, 1, D)
    wspec = lambda p: pl.BlockSpec((None, K, ct), lambda n: (layer, 0, p * n_j + jnp.minimum(n, n_tiles - 1) % n_j))
    out_spec = pl.BlockSpec((None, FFT_J, ct // LANES, tm // FFT_N2, SUBLANES, LANES),
                            lambda n: (ep(n) // n_j // seq_tiles, 0, ep(n) % n_j, (ep(n) // n_j) % seq_tiles, 0, 0))
    return pl.pallas_call(
        functools.partial(_inproj_kernel, row_chunk=min(tm, 128), n_j=n_j, n_tiles=n_tiles, seq_tiles=seq_tiles),
        grid=(n_tiles + 1,),
        in_specs=[
            pl.BlockSpec((tm, K), lambda n: (mm(n), 0)),
            pl.BlockSpec((h, K), lambda n: (jnp.maximum(mm(n) * halo_per_tile - 1, 0), 0)),
            pl.BlockSpec((h, K), lambda n: (jnp.minimum((mm(n) + 1) * halo_per_tile, last_halo), 0)),
            pl.BlockSpec((1, K), lambda n: (0, 0)),
            wspec(0), wspec(1), wspec(2),
            pl.BlockSpec((3, 1, ct), lambda n: (0, 0, ep(n) % n_j)),
            pl.BlockSpec((3, 3, ct), lambda n: (0, 0, ep(n) % n_j)),
            pl.BlockSpec((3, 1, ct), lambda n: (0, 0, ep(n) % n_j)),
        ],
        out_specs=[out_spec, out_spec],
        out_shape=[jax.ShapeDtypeStruct((batch, FFT_J, D // LANES, seq_len // FFT_N2, SUBLANES, LANES), F32)] * 2,
        scratch_shapes=[pltpu.VMEM((tm + 2 * h, K), BF16), pltpu.VMEM((3, tm + 2 * h, ct), F32),
                        pltpu.VMEM((3, tm + 2 * h, ct), F32)],
        compiler_params=_params(("arbitrary",), 48),
        name="inproj_conv_gate",
    )(x, x, x, g.reshape(1, K), w, w, w, parts(b_in), conv_w.reshape(3, 3, D).transpose(1, 0, 2), parts(conv_b))


def _filter_kernel(feat_ref, tv_ref, w1, b1, w2, b2, w3, b3, fr_ref, w4, delta_ref, o_ref):
    fr = fr_ref[...]
    h = jnp.sin(fr * (_dot3(feat_ref[...], w1[...]) + b1[...]))
    h = jnp.sin(fr * (_dot3(h, w2[...]) + b2[...]))
    h = jnp.sin(fr * (_dot3(h, w3[...]) + b3[...]))
    out = _dot3(h, w4[...])
    t = tv_ref[:, 0:1]
    valid = tv_ref[:, 1:2]
    _store_grouped(o_ref, out * jnp.exp(-t * delta_ref[...]) * valid)


def conv_kernel_signal(L, D, fw1, fb1, fw2, fb2, fw3, fb3, ffreq, fw4, *, tl=256):
    FH = fw2.shape[0]
    FE = 64
    p = jnp.arange(2 * L, dtype=jnp.int32)
    pos = jnp.where(p < L, p, 2 * L - p).astype(F32)
    t = pos / (L - 1)
    bands = jnp.linspace(1e-4, FILTER_BANDS - 1, FILTER_BANDS, dtype=F32)
    ang = (2.0 * math.pi / L) * pos[:, None] * bands[None, :]
    feat = jnp.concatenate(
        [t[:, None], jnp.cos(ang), -jnp.sin(ang), jnp.zeros((2 * L, FE - FILTER_EMB), F32)], axis=-1)
    tv = jnp.stack([t, (p != L).astype(F32)], axis=-1)
    deltas = np.abs(np.linspace(math.log(DECAY_TARGET) / SLOW_DECAY_PCT, math.log(DECAY_TARGET) / FAST_DECAY_PCT, D,
                                dtype=np.float32)).reshape(1, D)
    w1p = jnp.pad(fw1.astype(F32), ((0, FE - FILTER_EMB), (0, 0)))
    tl = _tile(L, tl, FFT_N2)
    half_tiles = L // tl
    full = lambda shape: pl.BlockSpec(shape, lambda i: (0, 0))
    return pl.pallas_call(
        _filter_kernel,
        grid=(2 * L // tl,),
        in_specs=[
            pl.BlockSpec((tl, FE), lambda i: (i, 0)),
            pl.BlockSpec((tl, 2), lambda i: (i, 0)),
            full((FE, FH)), full((1, FH)), full((FH, FH)), full((1, FH)), full((FH, FH)), full((1, FH)),
            full((1, FH)),
            pl.BlockSpec((FH, D), lambda i: (0, i // half_tiles)),
            full((1, D)),
        ],
        out_specs=pl.BlockSpec((FFT_J, D // LANES, tl // FFT_N2, SUBLANES, LANES), lambda i: (0, 0, i, 0, 0)),
        out_shape=jax.ShapeDtypeStruct((FFT_J, D // LANES, 2 * L // FFT_N2, SUBLANES, LANES), F32),
        compiler_params=_params(("parallel",), 32),
        name="hyena_filter",
    )(feat, tv, w1p, fb1.reshape(1, FH).astype(F32), fw2.astype(F32), fb2.reshape(1, FH).astype(F32),
      fw3.astype(F32), fb3.reshape(1, FH).astype(F32), ffreq.reshape(1, FH).astype(F32), fw4.astype(F32),
      jnp.asarray(deltas))


def _const(m):
    return jnp.asarray(np.asarray(m, np.float64), dtype=F32).astype(BF16)


def _stack(re, im):
    return np.block([[re, -im], [im, re]])


def _dft_tables(n1, n2):
    n, k1 = n1 * n2, n1 // 2
    a1 = 2.0 * np.pi * np.outer(np.arange(n1), np.arange(n1)) / n1
    c1, s1 = np.cos(a1), np.sin(a1)
    a2 = 2.0 * np.pi * np.outer(np.arange(n2), np.arange(n2)) / n2
    c2, s2 = np.cos(a2), np.sin(a2)
    at = 2.0 * np.pi * np.outer(np.arange(n2), np.arange(n1)) / n
    bcast = lambda t: jnp.broadcast_to(jnp.asarray(t, F32)[:, :, None], t.shape + (LANES,))
    return dict(
        f1_real=np.concatenate([c1, -s1], axis=0),
        f1_cplx=_stack(c1[:, :k1], -s1[:, :k1]),
        f2=_stack(c2, -s2),
        g2=_stack(c2, s2),
        g1_cplx=_stack(c1[:k1], s1[:k1]),
        tw1=(bcast(np.cos(at)), bcast(-np.sin(at))),
        tw2=(bcast(np.cos(at.T)), bcast(np.sin(at.T))),
    )


def _fft1_kernel(x_ref, f_ref, twr_ref, twi_ref, ar_ref, ai_ref, *, n1):
    f = f_ref[...]
    parts, cb, rows = x_ref.shape[0], x_ref.shape[1], x_ref.shape[2] // SUBLANES
    for i in range(SUBLANES):
        sel = pl.ds(i, rows, stride=SUBLANES)
        x = jnp.concatenate([jnp.concatenate([x_ref[p, c, sel, :] for c in range(cb)], axis=1)
                             for p in range(parts)], axis=0)
        r = _dotc(f, x)
        re, im = r[:n1], r[n1:]
        twr = jnp.concatenate([twr_ref[i]] * cb, axis=1)
        twi = jnp.concatenate([twi_ref[i]] * cb, axis=1)
        ar = re * twr - im * twi
        ai = re * twi + im * twr
        for c in range(cb):
            ar_ref[c, pl.ds(i, n1, stride=SUBLANES), :] = ar[:, c * LANES:(c + 1) * LANES]
            ai_ref[c, pl.ds(i, n1, stride=SUBLANES), :] = ai[:, c * LANES:(c + 1) * LANES]


def _fft_stage1(x, n1, f1, tw1):
    P, parts, _, C, k8, _ = x.shape
    fc = _const(f1)
    cb = min(C, FFT_LANE_TILES)
    const = lambda shape: pl.BlockSpec(shape, lambda j, p, c: (0,) * len(shape))
    out_spec = pl.BlockSpec((None, None, cb, n1 * SUBLANES, LANES), lambda j, p, c: (p, j, c, 0, 0))
    tw_spec = pl.BlockSpec((SUBLANES, n1, LANES), lambda j, p, c: (j, 0, 0))
    return pl.pallas_call(
        functools.partial(_fft1_kernel, n1=n1),
        grid=(FFT_J, P, C // cb),
        in_specs=[
            pl.BlockSpec((None, parts, None, cb, k8, LANES), lambda j, p, c: (p, 0, j, c, 0, 0)),
            const(f1.shape), tw_spec, tw_spec,
        ],
        out_specs=[out_spec, out_spec],
        out_shape=[jax.ShapeDtypeStruct((P, FFT_J, C, n1 * SUBLANES, LANES), F32)] * 2,
        compiler_params=_params(("parallel", "parallel", "parallel"), 40),
        name="fft_stage1",
    )(x, fc, tw1[0], tw1[1])


def _fft2_kernel(*refs, n2, cc, scale, spectrum_only):
    if spectrum_only:
        ar_ref, ai_ref, f_ref, yr_ref, yi_ref = refs
    else:
        ar_ref, ai_ref, kr_ref, ki_ref, f_ref, g_ref, twr_ref, twi_ref, yr_ref, yi_ref = refs
        twr = jnp.concatenate([twr_ref[...]] * (cc // LANES), axis=1)
        twi = jnp.concatenate([twi_ref[...]] * (cc // LANES), axis=1)
    f = f_ref[...]
    lt = cc // LANES
    rows = lambda ref, c: jnp.concatenate([ref[:, c * lt + t].reshape(n2, LANES) for t in range(lt)], axis=1)
    for c in range(ar_ref.shape[1] // lt):
        cols = slice(c * cc, (c + 1) * cc)
        y = _dotc(f, jnp.concatenate([rows(ar_ref, c), rows(ai_ref, c)], axis=0))
        yr, yi = y[:n2], y[n2:]
        if spectrum_only:
            yr_ref[:, cols] = (yr * scale).astype(yr_ref.dtype)
            yi_ref[:, cols] = (yi * scale).astype(yi_ref.dtype)
        else:
            kr = kr_ref[:, cols].astype(F32)
            ki = ki_ref[:, cols].astype(F32)
            pr = yr * kr - yi * ki
            pi = yr * ki + yi * kr
            q = _dotc(g_ref[...], jnp.concatenate([pr, pi], axis=0))
            qr, qi = q[:n2], q[n2:]
            outr = qr * twr - qi * twi
            outi = qr * twi + qi * twr
            for t in range(lt):
                yr_ref[:, c * lt + t] = outr[:, t * LANES:(t + 1) * LANES].reshape(FFT_J, SUBLANES, LANES)
                yi_ref[:, c * lt + t] = outi[:, t * LANES:(t + 1) * LANES].reshape(FFT_J, SUBLANES, LANES)


def _fft_stage2(ar, ai, n1, n2, D, f2, *, kf=None, g2=None, tw2=None, scale=1.0):
    B, _, C = ar.shape[:3]
    ar = ar.reshape(B, FFT_J, C, n1, SUBLANES, LANES)
    ai = ai.reshape(B, FFT_J, C, n1, SUBLANES, LANES)
    fc = _const(f2)
    cc = _tile(D, 512)
    blk = pl.BlockSpec((None, FFT_J, C, None, SUBLANES, LANES), lambda f, b: (b, 0, 0, f, 0, 0))
    const = lambda shape: pl.BlockSpec(shape, lambda f, b: (0,) * len(shape))
    specs = [blk, blk]
    args = [ar, ai]
    if kf is not None:
        gc = _const(g2)
        kblk = pl.BlockSpec((None, n2, D), lambda f, b: (f, 0, 0))
        tblk = pl.BlockSpec((None, n2, LANES), lambda f, b: (f, 0, 0))
        specs += [kblk, kblk, const((2 * n2, 2 * n2)), const((2 * n2, 2 * n2)), tblk, tblk]
        args += [kf[0], kf[1], fc, gc, tw2[0], tw2[1]]
        out_spec = blk
        out_shape = jax.ShapeDtypeStruct((B, FFT_J, C, n1, SUBLANES, LANES), F32)
    else:
        specs += [const((2 * n2, 2 * n2))]
        args += [fc]
        out_spec = pl.BlockSpec((None, None, n2, D), lambda f, b: (b, f, 0, 0))
        out_shape = jax.ShapeDtypeStruct((B, n1, n2, D), BF16)
    out = pl.pallas_call(
        functools.partial(_fft2_kernel, n2=n2, cc=cc, scale=scale, spectrum_only=kf is None),
        grid=(n1, B),
        in_specs=specs,
        out_specs=[out_spec, out_spec],
        out_shape=[out_shape] * 2,
        compiler_params=_params(("parallel", "parallel"), 40),
        name="fft_stage2",
    )(*args)
    if kf is not None:
        out = [o.reshape(B, FFT_J, C, n1 * SUBLANES, LANES) for o in out]
    return out


def _fft3_kernel(qr_ref, qi_ref, g_ref, z_ref, x0_ref, skip_ref, o_ref, *, n1):
    g = g_ref[...]
    cb, rows = z_ref.shape[1], z_ref.shape[2] // SUBLANES
    for i in range(SUBLANES):
        qsel = pl.ds(i, n1, stride=SUBLANES)
        sel = pl.ds(i, rows, stride=SUBLANES)
        q = jnp.concatenate([jnp.concatenate([ref[c, qsel, :] for c in range(cb)], axis=1)
                             for ref in (qr_ref, qi_ref)], axis=0)
        y = _dotc(g, q)
        for p in range(2):
            for c in range(cb):
                lanes = slice(c * LANES, (c + 1) * LANES)
                conv = y[p * rows:(p + 1) * rows, lanes]
                o_ref[p, c, sel, :] = (conv + z_ref[p, c, sel, :] * skip_ref[:, lanes]) * x0_ref[p, c, sel, :]


def _fft_stage3(qr, qi, z, x0, skip, n1, g1):
    P, _, _, C, k8, _ = z.shape
    gc = _const(g1)
    cb = min(C, FFT_LANE_TILES)
    qblk = pl.BlockSpec((None, None, cb, n1 * SUBLANES, LANES), lambda p, j, c: (p, j, c, 0, 0))
    zblk = pl.BlockSpec((None, 2, None, cb, k8, LANES), lambda p, j, c: (p, 0, j, c, 0, 0))
    const = lambda shape: pl.BlockSpec(shape, lambda p, j, c: (0,) * len(shape))
    return pl.pallas_call(
        functools.partial(_fft3_kernel, n1=n1),
        grid=(P, FFT_J, C // cb),
        in_specs=[qblk, qblk, const(g1.shape), zblk, zblk,
                  pl.BlockSpec((1, cb * LANES), lambda p, j, c: (0, c))],
        out_specs=zblk,
        out_shape=jax.ShapeDtypeStruct(z.shape, F32),
        compiler_params=_params(("parallel", "parallel", "parallel"), 52),
        name="fft_stage3",
    )(qr, qi, gc, z, x0, skip.reshape(1, C * LANES).astype(F32))


def long_conv_gate(z, x0, skip, kern):
    B, _, C, k1 = z.shape[:4]
    D = C * LANES
    n2 = FFT_N2
    n1 = 2 * k1
    assert B % 2 == 0
    t = _dft_tables(n1, n2)
    ka = _fft_stage1(kern.reshape(1, 1, FFT_J, C, n1 * SUBLANES, LANES), n1, t["f1_real"], t["tw1"])
    kf = _fft_stage2(ka[0], ka[1], n1, n2, D, t["f2"], scale=1.0 / (n1 * n2))
    kf = (kf[0].reshape(n1, n2, D), kf[1].reshape(n1, n2, D))
    pairs = lambda v: v.reshape(B // 2, 2, FFT_J, C, k1 * SUBLANES, LANES)
    a = _fft_stage1(pairs(z), n1, t["f1_cplx"], t["tw1"])
    q = _fft_stage2(a[0], a[1], n1, n2, D, t["f2"], kf=kf, g2=t["g2"], tw2=t["tw2"])
    return _fft_stage3(q[0], q[1], pairs(z), pairs(x0), skip, n1, t["g1_cplx"]).reshape(z.shape)


def _rope_tables(S):
    inv = ROPE_THETA ** (-jnp.arange(0, HEAD_DIM, 2, dtype=F32) / HEAD_DIM)
    ang = jnp.arange(S, dtype=F32)[:, None] * inv[None, :]
    cos, sin = jnp.cos(ang), jnp.sin(ang)
    cos2, sin2 = jnp.concatenate([cos, cos], axis=-1), jnp.concatenate([-sin, sin], axis=-1)
    return jnp.stack([cos2, jnp.ones_like(cos2)]), jnp.stack([sin2, jnp.zeros_like(sin2)])


def _trunk(x, p):
    B, S, D = x.shape
    H = D // HEAD_DIM
    G = len(GROUP_DILATIONS)
    T = B * S
    cos, sin = _rope_tables(S)
    x = x.reshape(T, D)
    depth = p["mix_norm"].shape[0]
    for i in range(depth):
        j = i // 2
        if i % 2 == 0:
            scale = HEAD_DIM ** -0.5
            gains = jnp.stack([p["attn_q_gain"][j] * scale, p["attn_k_gain"][j], jnp.ones_like(p["attn_k_gain"][j])],
                              axis=1).reshape(G * 3, 1, HEAD_DIM).astype(F32)
            qkv = [qkv_project(x, p["mix_norm"][i], p["attn_w_qkv"], j, gains, cos, sin, S, g) for g in range(G)]
            o = window_attention(qkv, B, S, H)
            x = matmul_resid(o, p["attn_w_out"], j, x, tm=512, tn=2048)
        else:
            z, x0 = inproj_conv_gate(x, p["mix_norm"][i], p["hy_w_in"], j, p["hy_b_in"][j], p["hy_conv_w"][j],
                                     p["hy_conv_b"][j], B, S)
            kern = conv_kernel_signal(S, D, p["hy_filt_w1"][j], p["hy_filt_b1"][j], p["hy_filt_w2"][j],
                                      p["hy_filt_b2"][j], p["hy_filt_w3"][j], p["hy_filt_b3"][j],
                                      p["hy_filt_freq"][j], p["hy_filt_w4"][j])
            y = long_conv_gate(z, x0, p["hy_skip"][j], kern)
            x = matmul_resid(y, p["hy_w_out"], j, x, p["hy_b_out"][j], tm=512, tn=2048)
        h = norm_matmul(x, p["mlp_norm"][i], p["mlp_w1"], i, relu2=True, out_dtype=BF16, tn=2048)
        x = matmul_resid(h, p["mlp_w2"], i, x, tk=4096)
    return x.reshape(B, S, D)


def kernel(x_prompt, x_sample, mix_norm, mlp_norm, attn_w_qkv, attn_q_gain, attn_k_gain, attn_w_out, hy_w_in, hy_b_in, hy_conv_w, hy_conv_b, hy_filt_w1, hy_filt_b1, hy_filt_w2, hy_filt_b2, hy_filt_w3, hy_filt_b3, hy_filt_freq, hy_filt_w4, hy_skip, hy_w_out, hy_b_out, mlp_w1, mlp_w2):
    p = dict(
        mix_norm=mix_norm, mlp_norm=mlp_norm,
        attn_w_qkv=attn_w_qkv.astype(BF16), attn_q_gain=attn_q_gain, attn_k_gain=attn_k_gain,
        attn_w_out=attn_w_out.astype(BF16),
        hy_w_in=hy_w_in.astype(BF16), hy_b_in=hy_b_in, hy_conv_w=hy_conv_w, hy_conv_b=hy_conv_b,
        hy_filt_w1=hy_filt_w1, hy_filt_b1=hy_filt_b1, hy_filt_w2=hy_filt_w2, hy_filt_b2=hy_filt_b2,
        hy_filt_w3=hy_filt_w3, hy_filt_b3=hy_filt_b3, hy_filt_freq=hy_filt_freq, hy_filt_w4=hy_filt_w4,
        hy_skip=hy_skip, hy_w_out=hy_w_out.astype(BF16), hy_b_out=hy_b_out,
        mlp_w1=mlp_w1.astype(BF16), mlp_w2=mlp_w2.astype(BF16),
    )
    return (_trunk(x_prompt, p), _trunk(x_sample, p))
```

```python
import functools
import math

import numpy as np
import jax
import jax.numpy as jnp
from jax import lax
from jax.experimental import pallas as pl
from jax.experimental.pallas import tpu as pltpu

F32 = jnp.float32
BF16 = jnp.bfloat16

HEAD_DIM = 128
GROUP_WINDOWS = (128, 512, 2048)
GROUP_DILATIONS = (1, 4, 16)
ROPE_THETA = 10000.0
FILTER_EMB = 33
FILTER_BANDS = (FILTER_EMB - 1) // 2
DECAY_TARGET = 1e-2
FAST_DECAY_PCT = 0.3
SLOW_DECAY_PCT = 1.5
EPS = 1e-6

LANES = 128
SUBLANES = 8
ATTN_TILE = 1024
ATTN_BATCH = 8
FFT_N2 = 128
FFT_J = FFT_N2 // SUBLANES
FFT_LANE_TILES = 4
MIB = 1024 * 1024


def _params(semantics, vmem_mib):
    return pltpu.CompilerParams(dimension_semantics=semantics, vmem_limit_bytes=vmem_mib * MIB)


def _tile(n, pref, quantum=LANES):
    if n <= pref:
        return n
    t = (pref // quantum) * quantum
    while t > quantum and n % t:
        t -= quantum
    assert n % t == 0, (n, pref)
    return t


def _split(x):
    hi = x.astype(BF16)
    lo = (x - hi.astype(F32)).astype(BF16)
    return hi, lo


def _dot(a, b):
    return jnp.dot(a, b, preferred_element_type=F32)


def _dot3(a, b):
    ah, al = _split(a)
    bh, bl = _split(b)
    return _dot(ah, bh) + (_dot(ah, bl) + _dot(al, bh))


def _dotc(c, b):
    return _dot(c, b.astype(BF16))


def _rms_rows_to(x_ref, g_ref, xn_ref, row_chunk):
    def body(c, carry):
        rows = pl.ds(pl.multiple_of(c * row_chunk, row_chunk), row_chunk)
        x = x_ref[rows, :]
        inv = lax.rsqrt(jnp.mean(x * x, axis=-1, keepdims=True) + EPS)
        xn_ref[rows, :] = (x * inv * g_ref[...]).astype(BF16)
        return carry

    lax.fori_loop(0, x_ref.shape[0] // row_chunk, body, 0)


def _store_grouped(o_ref, v):
    for c in range(o_ref.shape[1]):
        for a in range(o_ref.shape[2]):
            o_ref[:, c, a] = v[a * FFT_N2:(a + 1) * FFT_N2, c * LANES:(c + 1) * LANES].reshape(FFT_J, SUBLANES, LANES)


def _load_grouped(a_ref):
    rows = [jnp.concatenate([a_ref[:, c, s].reshape(FFT_N2, LANES) for c in range(a_ref.shape[1])], axis=1)
            for s in range(a_ref.shape[2])]
    return rows[0] if len(rows) == 1 else jnp.concatenate(rows, axis=0)


def _norm_mm_kernel(x_ref, g_ref, w_ref, *rest, has_bias, relu2, row_chunk):
    if has_bias:
        b_ref, o_ref, xn_ref = rest
    else:
        o_ref, xn_ref = rest

    @pl.when(pl.program_id(1) == 0)
    def _():
        _rms_rows_to(x_ref, g_ref, xn_ref, row_chunk)

    acc = _dot(xn_ref[...], w_ref[...])
    if has_bias:
        acc = acc + b_ref[...]
    if relu2:
        acc = jnp.maximum(acc, 0.0)
        acc = acc * acc
    o_ref[...] = acc.astype(o_ref.dtype)


def norm_matmul(x, g, w, layer, bias=None, *, relu2=False, out_dtype=F32, tm=1024, tn=1024):
    T, K = x.shape
    N = w.shape[2]
    tm = _tile(T, tm)
    tn = _tile(N, tn)
    in_specs = [
        pl.BlockSpec((tm, K), lambda i, j: (i, 0)),
        pl.BlockSpec((1, K), lambda i, j: (0, 0)),
        pl.BlockSpec((None, K, tn), lambda i, j: (layer, 0, j)),
    ]
    args = [x, g.reshape(1, K), w]
    if bias is not None:
        in_specs.append(pl.BlockSpec((1, tn), lambda i, j: (0, j)))
        args.append(bias.reshape(1, N))
    return pl.pallas_call(
        functools.partial(_norm_mm_kernel, has_bias=bias is not None, relu2=relu2, row_chunk=min(tm, 128)),
        grid=(T // tm, N // tn),
        in_specs=in_specs,
        out_specs=pl.BlockSpec((tm, tn), lambda i, j: (i, j)),
        out_shape=jax.ShapeDtypeStruct((T, N), out_dtype),
        scratch_shapes=[pltpu.VMEM((tm, K), BF16)],
        compiler_params=_params(("parallel", "arbitrary"), 56),
        name="norm_matmul",
    )(*args)


def _qkv_kernel(x_ref, g_ref, w_ref, gain_ref, cos_ref, sin_ref, o_ref, xn_ref, acc_a, acc_b, stage_ref, *,
                row_chunk, tiles_per_part, n_j, n_tiles, d):
    n = pl.program_id(0)
    tile = jnp.minimum(n, n_tiles - 1)
    done = jnp.maximum(n - 1, 0)
    tm = xn_ref.shape[0]

    @pl.when(n == 0)
    def _():
        acc_b[...] = jnp.zeros_like(acc_b)

    @pl.when((tile % n_j == 0) & (n < n_tiles))
    def _():
        _rms_rows_to(x_ref, g_ref, xn_ref, row_chunk)

    def step(acc_w, acc_r):
        acc_w[...] = _dot(xn_ref[...], w_ref[...])
        is_v = (done % n_j) // tiles_per_part == 2
        gain = gain_ref[...]
        cos = cos_ref[...]
        sin = sin_ref[...]
        for s in range(acc_r.shape[1] // HEAD_DIM):
            a = acc_r[:, s * HEAD_DIM:(s + 1) * HEAD_DIM]
            inv = jnp.where(is_v, 1.0, lax.rsqrt(jnp.mean(a * a, axis=-1, keepdims=True) + EPS))
            y = a * inv * gain
            y = y * cos + pltpu.roll(y, HEAD_DIM // 2, axis=1) * sin
            if d == 1:
                o_ref[s] = y.astype(o_ref.dtype)
            else:
                stage_ref[s] = y
                n_r = tm // d
                for r in range(d):
                    o_ref[s, r * n_r:(r + 1) * n_r, :] = stage_ref[s, pl.ds(r, n_r, stride=d), :].astype(o_ref.dtype)

    @pl.when(n % 2 == 0)
    def _():
        step(acc_a, acc_b)

    @pl.when(n % 2 == 1)
    def _():
        step(acc_b, acc_a)


def qkv_project(x, g, w, layer, gains, cos, sin, seq_len, group, *, tn=1024):
    T, K = x.shape
    part_width = w.shape[2] // (len(GROUP_DILATIONS) * 3)
    tm = ATTN_TILE
    assert seq_len % tm == 0
    tn = _tile(part_width, tn)
    pos_tiles = seq_len // tm
    tiles_per_part = part_width // tn
    n_j = 3 * tiles_per_part
    n_tiles = (T // tm) * n_j
    mm = lambda n: jnp.minimum(n, n_tiles - 1)
    ep = lambda n: jnp.maximum(n - 1, 0)
    table = lambda n: ((ep(n) % n_j) // tiles_per_part // 2, (ep(n) // n_j) % pos_tiles, 0)
    return pl.pallas_call(
        functools.partial(_qkv_kernel, row_chunk=min(tm, 128), tiles_per_part=tiles_per_part, n_j=n_j,
                          n_tiles=n_tiles, d=GROUP_DILATIONS[group]),
        grid=(n_tiles + 1,),
        in_specs=[
            pl.BlockSpec((tm, K), lambda n: (mm(n) // n_j, 0)),
            pl.BlockSpec((1, K), lambda n: (0, 0)),
            pl.BlockSpec((None, K, tn), lambda n: (layer, 0, group * n_j + mm(n) % n_j)),
            pl.BlockSpec((None, 1, HEAD_DIM), lambda n: (group * 3 + (ep(n) % n_j) // tiles_per_part, 0, 0)),
            pl.BlockSpec((None, tm, HEAD_DIM), table),
            pl.BlockSpec((None, tm, HEAD_DIM), table),
        ],
        out_specs=pl.BlockSpec(
            (tn // HEAD_DIM, None, tm, HEAD_DIM),
            lambda n: (ep(n) % tiles_per_part, ((ep(n) % n_j) // tiles_per_part + 2) % 3, ep(n) // n_j, 0)),
        out_shape=jax.ShapeDtypeStruct((part_width // HEAD_DIM, 3, T, HEAD_DIM), BF16),
        scratch_shapes=[pltpu.VMEM((tm, K), BF16), pltpu.VMEM((tm, tn), F32), pltpu.VMEM((tm, tn), F32),
                        pltpu.VMEM((tn // HEAD_DIM, tm, HEAD_DIM), F32)],
        compiler_params=_params(("arbitrary",), 52),
        name="qkv_project",
    )(x, g.reshape(1, K), w, gains, cos, sin)


def _mm_resid_kernel(a_ref, w_ref, *rest, has_bias, grouped):
    if has_bias:
        b_ref, r_ref, o_ref = rest
    else:
        r_ref, o_ref = rest
    k = pl.program_id(2)
    a = _load_grouped(a_ref) if grouped else a_ref[...]
    part = _dot(a.astype(BF16), w_ref[...])

    @pl.when(k == 0)
    def _():
        first = part + r_ref[...]
        if has_bias:
            first = first + b_ref[...]
        o_ref[...] = first

    @pl.when(k > 0)
    def _():
        o_ref[...] += part


def matmul_resid(a, w, layer, resid, bias=None, *, tm=1024, tn=1024, tk=2048):
    grouped = a.ndim == 6
    T, N = resid.shape
    K = w.shape[1]
    tn = _tile(N, tn)
    tk = _tile(K, tk)
    if grouped:
        seq_len = a.shape[3] * FFT_N2
        tm = _tile(seq_len, tm)
        seq_tiles = seq_len // tm
        a_spec = pl.BlockSpec((None, FFT_J, tk // LANES, tm // FFT_N2, SUBLANES, LANES),
                              lambda i, j, k: (i // seq_tiles, 0, k, i % seq_tiles, 0, 0))
    else:
        tm = _tile(T, tm)
        a_spec = pl.BlockSpec((tm, tk), lambda i, j, k: (i, k))
    in_specs = [
        a_spec,
        pl.BlockSpec((None, tk, tn), lambda i, j, k: (layer, k, j)),
    ]
    args = [a, w]
    if bias is not None:
        in_specs.append(pl.BlockSpec((1, tn), lambda i, j, k: (0, j)))
        args.append(bias.reshape(1, N))
    in_specs.append(pl.BlockSpec((tm, tn), lambda i, j, k: (i, j)))
    args.append(resid)
    return pl.pallas_call(
        functools.partial(_mm_resid_kernel, has_bias=bias is not None, grouped=grouped),
        grid=(T // tm, N // tn, K // tk),
        in_specs=in_specs,
        out_specs=pl.BlockSpec((tm, tn), lambda i, j, k: (i, j)),
        out_shape=jax.ShapeDtypeStruct((T, N), F32),
        compiler_params=_params(("parallel", "parallel", "arbitrary"), 58),
        name="matmul_resid",
    )(*args)


def _attn_kernel(*refs, tq, dils, half, n_tiles):
    G = len(dils)
    cur_refs, prev_refs, next_refs = refs[0:G], refs[G:2 * G], refs[2 * G:3 * G]
    o_ref = refs[3 * G]
    acc_ref, m_ref, l_ref, out_ref, bias_ref = refs[3 * G + 1:]
    i = pl.program_id(2)
    has_prev = i > 0
    has_next = i < n_tiles - 1
    dmax = max(dils)

    for g, d in enumerate(dils):
        n_r = tq // d
        qs = min(128, n_r)
        ks = qs + 2 * half
        n_sub = n_r // qs
        qq = lax.broadcasted_iota(jnp.int32, (qs, ks), 0)
        kk = lax.broadcasted_iota(jnp.int32, (qs, ks), 1)
        band = (kk >= qq) & (kk <= qq + 2 * half)
        prev_ok = (kk >= half) | has_prev
        next_ok = (kk < ks - half) | has_next
        for idx, valid in enumerate((band & prev_ok, band, band & next_ok, band & prev_ok & next_ok)):
            bias_ref[idx, 0:qs, 0:ks] = jnp.where(valid, 0.0, -jnp.inf)

        def window(c, r, u, g=g, n_r=n_r, qs=qs):
            lo, hi = u * qs - half, (u + 1) * qs + half
            pieces = [prev_refs[g][c, r]] if lo < 0 else []
            pieces.append(cur_refs[g][c, r, max(lo, 0):min(hi, n_r), :])
            if hi > n_r:
                pieces.append(next_refs[g][c, r])
            return pieces[0] if len(pieces) == 1 else jnp.concatenate(pieces, axis=0)

        blocks = [(r, u) for r in range(d) for u in range(n_sub)]
        for b0 in range(0, len(blocks), ATTN_BATCH):
            batch = blocks[b0:b0 + ATTN_BATCH]
            edge = lambda u: (3 if n_sub == 1 else 0) if u == 0 else (2 if u == n_sub - 1 else 1)
            q = jnp.stack([cur_refs[g][2, r, u * qs:(u + 1) * qs, :] for r, u in batch])
            kw = jnp.stack([window(0, r, u) for r, u in batch])
            vw = jnp.stack([window(1, r, u) for r, u in batch])
            bias = jnp.stack([bias_ref[edge(u), 0:qs, 0:ks] for _, u in batch])
            s = jnp.einsum("bqd,bkd->bqk", q, kw, preferred_element_type=F32) + bias
            m_blk = jnp.max(s, axis=-1, keepdims=True)
            p = jnp.exp(s - m_blk)
            l_blk = jnp.sum(p, axis=-1, keepdims=True)
            pv = jnp.einsum("bqk,bkd->bqd", p.astype(BF16), vw, preferred_element_type=F32)
            st, pitch = _merge_pitch(dmax // d)
            for b, (r, u) in enumerate(batch):
                base = r * n_r + u * qs
                chunk = qs if st == pitch else st
                for k in range(qs // chunk):
                    src = slice(k * chunk, (k + 1) * chunk)
                    dst = pl.ds(base + k * chunk if st == pitch else (base // st + k) * pitch, chunk)
                    m_ref[g, dst, :] = m_blk[b][src]
                    l_ref[g, dst, :] = l_blk[b][src]
                    acc_ref[g, dst, :] = pv[b][src]

    n_max = tq // dmax
    out_pitch = n_max + SUBLANES
    for r in range(dmax):
        sel = []
        for d in dils:
            st, pitch = _merge_pitch(dmax // d)
            start = (r % d) * (tq // d // st) * pitch + r // d
            sel.append(pl.ds(start, n_max, stride=pitch * (dmax // d) // st) if d < dmax else pl.ds(start, n_max))
        ms = [m_ref[g, sel[g], :] for g in range(G)]
        m = functools.reduce(jnp.maximum, ms)
        ws = [jnp.exp(mg - m) for mg in ms]
        num = sum(ws[g] * acc_ref[g, sel[g], :] for g in range(G))
        den = sum(ws[g] * l_ref[g, sel[g], :] for g in range(G))
        out_ref[r * out_pitch:r * out_pitch + n_max, :] = num / den
    o_ref[...] = jnp.concatenate([out_ref[pl.ds(j, dmax, stride=out_pitch), :] for j in range(n_max)],
                                 axis=0).astype(o_ref.dtype)


def _merge_pitch(stride):
    if stride % SUBLANES:
        return SUBLANES, SUBLANES
    return stride, stride + SUBLANES


def window_attention(qkv, batch, seq_len, n_heads):
    G = len(GROUP_DILATIONS)
    tq = ATTN_TILE
    assert seq_len % tq == 0
    n_tiles = seq_len // tq
    total_tiles = batch * n_tiles
    half = GROUP_WINDOWS[0] // (2 * GROUP_DILATIONS[0])
    for wdw, d in zip(GROUP_WINDOWS, GROUP_DILATIONS):
        assert wdw // (2 * d) == half and half * d <= tq and tq % (d * min(128, tq // d)) == 0
    H = n_heads
    dils = GROUP_DILATIONS

    cur, prev, nxt = [], [], []
    for d in dils:
        cur.append(pl.BlockSpec((None, 3, None, d, tq // d, HEAD_DIM),
                                lambda b, h, i: (h, 0, b * n_tiles + i, 0, 0, 0)))
        prev.append(pl.BlockSpec((None, 2, None, d, half, HEAD_DIM),
                                 lambda b, h, i, d=d: (h, 0, jnp.maximum(b * n_tiles + i - 1, 0), 0,
                                                       tq // d // half - 1, 0)))
        nxt.append(pl.BlockSpec((None, 2, None, d, half, HEAD_DIM),
                                lambda b, h, i: (h, 0, jnp.minimum(b * n_tiles + i + 1, total_tiles - 1), 0, 0, 0)))
    views = [t.reshape(H, 3, total_tiles, d, tq // d, HEAD_DIM) for t, d in zip(qkv, dils)]
    merge_rows = max(tq // st * pitch for st, pitch in (_merge_pitch(max(dils) // d) for d in dils))
    in_specs = cur + prev + nxt
    return pl.pallas_call(
        functools.partial(_attn_kernel, tq=tq, dils=dils, half=half, n_tiles=n_tiles),
        grid=(batch, H, n_tiles),
        in_specs=in_specs,
        out_specs=pl.BlockSpec((tq, HEAD_DIM), lambda b, h, i: (b * n_tiles + i, h)),
        out_shape=jax.ShapeDtypeStruct((batch * seq_len, H * HEAD_DIM), BF16),
        scratch_shapes=[
            pltpu.VMEM((G, merge_rows, HEAD_DIM), F32),
            pltpu.VMEM((G, merge_rows, 1), F32),
            pltpu.VMEM((G, merge_rows, 1), F32),
            pltpu.VMEM((max(dils) * (tq // max(dils) + SUBLANES), HEAD_DIM), F32),
            pltpu.VMEM((4, 128, 128 + 2 * half), F32),
        ],
        compiler_params=_params(("parallel", "parallel", "parallel"), 40),
        name="window_attention",
    )(*(views * 3))


def _sconv_kernel(*refs, n_tiles):
    (x0m, x0p, x0n, x1m, x1p, x1n, vm, vp, vn, w0, w1, w2, b0, b1, b2, z_ref, x0_ref) = refs
    i = pl.program_id(1)

    def conv(m_ref, p_ref, n_ref, w_ref, b_ref):
        x = m_ref[...]
        ts = x.shape[0]
        prev_row = jnp.where(i > 0, p_ref[7:8, :], 0.0)
        next_row = jnp.where(i < n_tiles - 1, n_ref[0:1, :], 0.0)
        row = lax.broadcasted_iota(jnp.int32, x.shape, 0)
        up = jnp.where(row == 0, prev_row, pltpu.roll(x, 1, axis=0))
        dn = jnp.where(row == ts - 1, next_row, pltpu.roll(x, ts - 1, axis=0))
        return up * w_ref[0:1, :] + x * w_ref[1:2, :] + dn * w_ref[2:3, :] + b_ref[...]

    _store_grouped(x0_ref, conv(x0m, x0p, x0n, w0, b0))
    _store_grouped(z_ref, conv(vm, vp, vn, w2, b2) * conv(x1m, x1p, x1n, w1, b1))


def short_conv_gate(u, conv_w, conv_b, *, ts=512, ct=512):
    B, L, D3 = u.shape
    D = D3 // 3
    ts = _tile(L, ts, FFT_N2)
    ct = _tile(D, ct)
    n_tiles = L // ts
    n_ct = D // ct
    rows8 = ts // 8
    last8 = L // 8 - 1
    specs, args = [], []
    for part in range(3):
        off = part * n_ct
        specs += [
            pl.BlockSpec((None, ts, ct), lambda b, i, j, off=off: (b, i, off + j)),
            pl.BlockSpec((None, 8, ct), lambda b, i, j, off=off: (b, jnp.maximum(i * rows8 - 1, 0), off + j)),
            pl.BlockSpec((None, 8, ct), lambda b, i, j, off=off: (b, jnp.minimum((i + 1) * rows8, last8), off + j)),
        ]
        args += [u, u, u]
    for part in range(3):
        specs.append(pl.BlockSpec((3, ct), lambda b, i, j, off=part * n_ct: (0, off + j)))
        args.append(conv_w)
    for part in range(3):
        specs.append(pl.BlockSpec((1, ct), lambda b, i, j, off=part * n_ct: (0, off + j)))
        args.append(conv_b.reshape(1, D3))
    out_spec = pl.BlockSpec((None, FFT_J, ct // LANES, ts // FFT_N2, SUBLANES, LANES),
                            lambda b, i, j: (b, 0, j, i, 0, 0))
    return pl.pallas_call(
        functools.partial(_sconv_kernel, n_tiles=n_tiles),
        grid=(B, n_tiles, n_ct),
        in_specs=specs,
        out_specs=[out_spec, out_spec],
        out_shape=[jax.ShapeDtypeStruct((B, FFT_J, D // LANES, L // FFT_N2, SUBLANES, LANES), F32)] * 2,
        compiler_params=_params(("parallel", "parallel", "parallel"), 32),
        name="short_conv_gate",
    )(*args)


CONV_HALO = 16
CONV_ROWS = 32
INPROJ_CT = 256


def _inproj_kernel(x_ref, xp_ref, xq_ref, g_ref, w0_ref, w1_ref, w2_ref, b_ref, cw_ref, cb_ref, z_ref, x0_ref,
                   xn_ref, acc_ref, *, row_chunk, n_j, n_tiles, seq_tiles):
    n = pl.program_id(0)
    tile = jnp.minimum(n, n_tiles - 1)
    done = jnp.maximum(n - 1, 0)
    tm = x_ref.shape[0]
    h = CONV_HALO

    @pl.when(n == 0)
    def _():
        acc_ref[1] = jnp.zeros(acc_ref.shape[1:], F32)

    @pl.when((tile % n_j == 0) & (n < n_tiles))
    def _():
        _rms_rows_to(xp_ref, g_ref, xn_ref.at[0:h], h)
        _rms_rows_to(x_ref, g_ref, xn_ref.at[h:h + tm], row_chunk)
        _rms_rows_to(xq_ref, g_ref, xn_ref.at[h + tm:2 * h + tm], h)

    def step(acc_w, acc_r):
        pos = (done // n_j) % seq_tiles
        rc = CONV_ROWS
        row = lax.broadcasted_iota(jnp.int32, (rc, 1), 0)
        for t0 in range(0, tm, rc):

            def conv(p):
                b = b_ref[p]
                up = acc_r[p, h + t0 - 1:h + t0 - 1 + rc, :] + b
                dn = acc_r[p, h + t0 + 1:h + t0 + 1 + rc, :] + b
                if t0 == 0:
                    up = jnp.where((row == 0) & (pos == 0), 0.0, up)
                if t0 + rc == tm:
                    dn = jnp.where((row == rc - 1) & (pos == seq_tiles - 1), 0.0, dn)
                cw = cw_ref[p]
                return up * cw[0:1] + (acc_r[p, h + t0:h + t0 + rc, :] + b) * cw[1:2] + dn * cw[2:3] + cb_ref[p]

            x0 = conv(0)
            z = conv(2) * conv(1)
            a, j0 = t0 // FFT_N2, (t0 % FFT_N2) // SUBLANES
            for c in range(z_ref.shape[1]):
                lanes = slice(c * LANES, (c + 1) * LANES)
                x0_ref[j0:j0 + rc // SUBLANES, c, a] = x0[:, lanes].reshape(rc // SUBLANES, SUBLANES, LANES)
                z_ref[j0:j0 + rc // SUBLANES, c, a] = z[:, lanes].reshape(rc // SUBLANES, SUBLANES, LANES)
        xn = xn_ref[...]
        for p, w_ref in enumerate((w0_ref, w1_ref, w2_ref)):
            acc_w[p] = _dot(xn, w_ref[...])

    @pl.when(n % 2 == 0)
    def _():
        step(acc_ref.at[0], acc_ref.at[1])

    @pl.when(n % 2 == 1)
    def _():
        step(acc_ref.at[1], acc_ref.at[0])


def inproj_weight_blocks(w):
    layers, K, d3 = w.shape
    ct = _tile(d3 // 3, INPROJ_CT)
    return w.reshape(layers, K, d3 // ct, ct).transpose(0, 2, 1, 3)


def inproj_conv_gate(x, g, w, layer, b_in, conv_w, conv_b, batch, seq_len, *, tm=1024):
    T, K = x.shape
    ct = w.shape[3]
    D = w.shape[1] * ct // 3
    tm = _tile(seq_len, tm, FFT_N2)
    h = CONV_HALO
    n_j = D // ct
    seq_tiles = seq_len // tm
    n_tiles = (T // tm) * n_j
    halo_per_tile = tm // h
    last_halo = T // h - 1
    mm = lambda n: jnp.minimum(n, n_tiles - 1) // n_j
    ep = lambda n: jnp.maximum(n - 1, 0)
    parts = lambda a: a.reshape(3, 1, D)
    wspec = lambda p: pl.BlockSpec((None, None, K, ct),
                                   lambda n: (layer, p * n_j + jnp.minimum(n, n_tiles - 1) % n_j, 0, 0))
    out_spec = pl.BlockSpec((None, FFT_J, ct // LANES, tm // FFT_N2, SUBLANES, LANES),
                            lambda n: (ep(n) // n_j // seq_tiles, 0, ep(n) % n_j, (ep(n) // n_j) % seq_tiles, 0, 0))
    return pl.pallas_call(
        functools.partial(_inproj_kernel, row_chunk=min(tm, 128), n_j=n_j, n_tiles=n_tiles, seq_tiles=seq_tiles),
        grid=(n_tiles + 1,),
        in_specs=[
            pl.BlockSpec((tm, K), lambda n: (mm(n), 0)),
            pl.BlockSpec((h, K), lambda n: (jnp.maximum(mm(n) * halo_per_tile - 1, 0), 0)),
            pl.BlockSpec((h, K), lambda n: (jnp.minimum((mm(n) + 1) * halo_per_tile, last_halo), 0)),
            pl.BlockSpec((1, K), lambda n: (0, 0)),
            wspec(0), wspec(1), wspec(2),
            pl.BlockSpec((3, 1, ct), lambda n: (0, 0, ep(n) % n_j)),
            pl.BlockSpec((3, 3, ct), lambda n: (0, 0, ep(n) % n_j)),
            pl.BlockSpec((3, 1, ct), lambda n: (0, 0, ep(n) % n_j)),
        ],
        out_specs=[out_spec, out_spec],
        out_shape=[jax.ShapeDtypeStruct((batch, FFT_J, D // LANES, seq_len // FFT_N2, SUBLANES, LANES), F32)] * 2,
        scratch_shapes=[pltpu.VMEM((tm + 2 * h, K), BF16), pltpu.VMEM((2, 3, tm + 2 * h, ct), F32)],
        compiler_params=_params(("arbitrary",), 48),
        name="inproj_conv_gate",
    )(x, x, x, g.reshape(1, K), w, w, w, parts(b_in), conv_w.reshape(3, 3, D).transpose(1, 0, 2), parts(conv_b))


def _filter_kernel(feat_ref, tv_ref, w1, b1, w2, b2, w3, b3, fr_ref, w4, delta_ref, o_ref):
    fr = fr_ref[...]
    h = jnp.sin(fr * (_dot3(feat_ref[...], w1[...]) + b1[...]))
    h = jnp.sin(fr * (_dot3(h, w2[...]) + b2[...]))
    h = jnp.sin(fr * (_dot3(h, w3[...]) + b3[...]))
    out = _dot3(h, w4[...])
    t = tv_ref[:, 0:1]
    valid = tv_ref[:, 1:2]
    _store_grouped(o_ref, out * jnp.exp(-t * delta_ref[...]) * valid)


def conv_kernel_signal(L, D, fw1, fb1, fw2, fb2, fw3, fb3, ffreq, fw4, *, tl=256):
    FH = fw2.shape[0]
    FE = 64
    p = jnp.arange(2 * L, dtype=jnp.int32)
    pos = jnp.where(p < L, p, 2 * L - p).astype(F32)
    t = pos / (L - 1)
    bands = jnp.linspace(1e-4, FILTER_BANDS - 1, FILTER_BANDS, dtype=F32)
    ang = (2.0 * math.pi / L) * pos[:, None] * bands[None, :]
    feat = jnp.concatenate(
        [t[:, None], jnp.cos(ang), -jnp.sin(ang), jnp.zeros((2 * L, FE - FILTER_EMB), F32)], axis=-1)
    tv = jnp.stack([t, (p != L).astype(F32)], axis=-1)
    deltas = np.abs(np.linspace(math.log(DECAY_TARGET) / SLOW_DECAY_PCT, math.log(DECAY_TARGET) / FAST_DECAY_PCT, D,
                                dtype=np.float32)).reshape(1, D)
    w1p = jnp.pad(fw1.astype(F32), ((0, FE - FILTER_EMB), (0, 0)))
    tl = _tile(L, tl, FFT_N2)
    half_tiles = L // tl
    full = lambda shape: pl.BlockSpec(shape, lambda i: (0, 0))
    return pl.pallas_call(
        _filter_kernel,
        grid=(2 * L // tl,),
        in_specs=[
            pl.BlockSpec((tl, FE), lambda i: (i, 0)),
            pl.BlockSpec((tl, 2), lambda i: (i, 0)),
            full((FE, FH)), full((1, FH)), full((FH, FH)), full((1, FH)), full((FH, FH)), full((1, FH)),
            full((1, FH)),
            pl.BlockSpec((FH, D), lambda i: (0, i // half_tiles)),
            full((1, D)),
        ],
        out_specs=pl.BlockSpec((FFT_J, D // LANES, tl // FFT_N2, SUBLANES, LANES), lambda i: (0, 0, i, 0, 0)),
        out_shape=jax.ShapeDtypeStruct((FFT_J, D // LANES, 2 * L // FFT_N2, SUBLANES, LANES), F32),
        compiler_params=_params(("parallel",), 32),
        name="hyena_filter",
    )(feat, tv, w1p, fb1.reshape(1, FH).astype(F32), fw2.astype(F32), fb2.reshape(1, FH).astype(F32),
      fw3.astype(F32), fb3.reshape(1, FH).astype(F32), ffreq.reshape(1, FH).astype(F32), fw4.astype(F32),
      jnp.asarray(deltas))


def _const(m):
    return jnp.asarray(np.asarray(m, np.float64), dtype=F32).astype(BF16)


def _stack(re, im):
    return np.block([[re, -im], [im, re]])


def _dft_tables(n1, n2):
    n, k1 = n1 * n2, n1 // 2
    a1 = 2.0 * np.pi * np.outer(np.arange(n1), np.arange(n1)) / n1
    c1, s1 = np.cos(a1), np.sin(a1)
    a2 = 2.0 * np.pi * np.outer(np.arange(n2), np.arange(n2)) / n2
    c2, s2 = np.cos(a2), np.sin(a2)
    at = 2.0 * np.pi * np.outer(np.arange(n2), np.arange(n1)) / n
    bcast = lambda t: jnp.broadcast_to(jnp.asarray(t, F32)[:, :, None], t.shape + (LANES,))
    return dict(
        f1_real=np.concatenate([c1, -s1], axis=0),
        f1_cplx=_stack(c1[:, :k1], -s1[:, :k1]),
        f2=_stack(c2, -s2),
        g2=_stack(c2, s2),
        g1_cplx=_stack(c1[:k1], s1[:k1]),
        tw1=(bcast(np.cos(at)), bcast(-np.sin(at))),
        tw2=(bcast(np.cos(at.T)), bcast(np.sin(at.T))),
    )


def _fft1_kernel(x_ref, f_ref, twr_ref, twi_ref, ar_ref, ai_ref, *, n1):
    f = f_ref[...]
    parts, cb, rows = x_ref.shape[0], x_ref.shape[1], x_ref.shape[2] // SUBLANES
    for i in range(SUBLANES):
        sel = pl.ds(i, rows, stride=SUBLANES)
        x = jnp.concatenate([jnp.concatenate([x_ref[p, c, sel, :] for c in range(cb)], axis=1)
                             for p in range(parts)], axis=0)
        r = _dotc(f, x)
        re, im = r[:n1], r[n1:]
        twr = jnp.concatenate([twr_ref[i]] * cb, axis=1)
        twi = jnp.concatenate([twi_ref[i]] * cb, axis=1)
        ar = re * twr - im * twi
        ai = re * twi + im * twr
        for c in range(cb):
            ar_ref[c, pl.ds(i, n1, stride=SUBLANES), :] = ar[:, c * LANES:(c + 1) * LANES]
            ai_ref[c, pl.ds(i, n1, stride=SUBLANES), :] = ai[:, c * LANES:(c + 1) * LANES]


def _fft_stage1(x, n1, f1, tw1):
    P, parts, _, C, k8, _ = x.shape
    fc = _const(f1)
    cb = min(C, FFT_LANE_TILES)
    const = lambda shape: pl.BlockSpec(shape, lambda j, p, c: (0,) * len(shape))
    out_spec = pl.BlockSpec((None, None, cb, n1 * SUBLANES, LANES), lambda j, p, c: (p, j, c, 0, 0))
    tw_spec = pl.BlockSpec((SUBLANES, n1, LANES), lambda j, p, c: (j, 0, 0))
    return pl.pallas_call(
        functools.partial(_fft1_kernel, n1=n1),
        grid=(FFT_J, P, C // cb),
        in_specs=[
            pl.BlockSpec((None, parts, None, cb, k8, LANES), lambda j, p, c: (p, 0, j, c, 0, 0)),
            const(f1.shape), tw_spec, tw_spec,
        ],
        out_specs=[out_spec, out_spec],
        out_shape=[jax.ShapeDtypeStruct((P, FFT_J, C, n1 * SUBLANES, LANES), F32)] * 2,
        compiler_params=_params(("parallel", "parallel", "parallel"), 40),
        name="fft_stage1",
    )(x, fc, tw1[0], tw1[1])


def _fft2_kernel(*refs, n2, cc, scale, spectrum_only):
    if spectrum_only:
        ar_ref, ai_ref, f_ref, yr_ref, yi_ref = refs
    else:
        ar_ref, ai_ref, kr_ref, ki_ref, f_ref, g_ref, twr_ref, twi_ref, yr_ref, yi_ref = refs
        twr = jnp.concatenate([twr_ref[...]] * (cc // LANES), axis=1)
        twi = jnp.concatenate([twi_ref[...]] * (cc // LANES), axis=1)
    f = f_ref[...]
    lt = cc // LANES
    rows = lambda ref, c: jnp.concatenate([ref[:, c * lt + t].reshape(n2, LANES) for t in range(lt)], axis=1)
    for c in range(ar_ref.shape[1] // lt):
        cols = slice(c * cc, (c + 1) * cc)
        y = _dotc(f, jnp.concatenate([rows(ar_ref, c), rows(ai_ref, c)], axis=0))
        yr, yi = y[:n2], y[n2:]
        if spectrum_only:
            yr_ref[:, cols] = (yr * scale).astype(yr_ref.dtype)
            yi_ref[:, cols] = (yi * scale).astype(yi_ref.dtype)
        else:
            kr = kr_ref[:, cols].astype(F32)
            ki = ki_ref[:, cols].astype(F32)
            pr = yr * kr - yi * ki
            pi = yr * ki + yi * kr
            q = _dotc(g_ref[...], jnp.concatenate([pr, pi], axis=0))
            qr, qi = q[:n2], q[n2:]
            outr = qr * twr - qi * twi
            outi = qr * twi + qi * twr
            for t in range(lt):
                yr_ref[:, c * lt + t] = outr[:, t * LANES:(t + 1) * LANES].reshape(FFT_J, SUBLANES, LANES)
                yi_ref[:, c * lt + t] = outi[:, t * LANES:(t + 1) * LANES].reshape(FFT_J, SUBLANES, LANES)


def _fft_stage2(ar, ai, n1, n2, D, f2, *, kf=None, g2=None, tw2=None, scale=1.0):
    B, _, C = ar.shape[:3]
    ar = ar.reshape(B, FFT_J, C, n1, SUBLANES, LANES)
    ai = ai.reshape(B, FFT_J, C, n1, SUBLANES, LANES)
    fc = _const(f2)
    cc = _tile(D, 512)
    blk = pl.BlockSpec((None, FFT_J, C, None, SUBLANES, LANES), lambda f, b: (b, 0, 0, f, 0, 0))
    const = lambda shape: pl.BlockSpec(shape, lambda f, b: (0,) * len(shape))
    specs = [blk, blk]
    args = [ar, ai]
    if kf is not None:
        gc = _const(g2)
        kblk = pl.BlockSpec((None, n2, D), lambda f, b: (f, 0, 0))
        tblk = pl.BlockSpec((None, n2, LANES), lambda f, b: (f, 0, 0))
        specs += [kblk, kblk, const((2 * n2, 2 * n2)), const((2 * n2, 2 * n2)), tblk, tblk]
        args += [kf[0], kf[1], fc, gc, tw2[0], tw2[1]]
        out_spec = blk
        out_shape = jax.ShapeDtypeStruct((B, FFT_J, C, n1, SUBLANES, LANES), F32)
    else:
        specs += [const((2 * n2, 2 * n2))]
        args += [fc]
        out_spec = pl.BlockSpec((None, None, n2, D), lambda f, b: (b, f, 0, 0))
        out_shape = jax.ShapeDtypeStruct((B, n1, n2, D), BF16)
    out = pl.pallas_call(
        functools.partial(_fft2_kernel, n2=n2, cc=cc, scale=scale, spectrum_only=kf is None),
        grid=(n1, B),
        in_specs=specs,
        out_specs=[out_spec, out_spec],
        out_shape=[out_shape] * 2,
        compiler_params=_params(("parallel", "parallel"), 40),
        name="fft_stage2",
    )(*args)
    if kf is not None:
        out = [o.reshape(B, FFT_J, C, n1 * SUBLANES, LANES) for o in out]
    return out


def _fft3_kernel(qr_ref, qi_ref, g_ref, z_ref, x0_ref, skip_ref, o_ref, *, n1):
    g = g_ref[...]
    cb, rows = z_ref.shape[1], z_ref.shape[2] // SUBLANES
    for i in range(SUBLANES):
        qsel = pl.ds(i, n1, stride=SUBLANES)
        sel = pl.ds(i, rows, stride=SUBLANES)
        q = jnp.concatenate([jnp.concatenate([ref[c, qsel, :] for c in range(cb)], axis=1)
                             for ref in (qr_ref, qi_ref)], axis=0)
        y = _dotc(g, q)
        for p in range(2):
            for c in range(cb):
                lanes = slice(c * LANES, (c + 1) * LANES)
                conv = y[p * rows:(p + 1) * rows, lanes]
                o_ref[p, c, sel, :] = (conv + z_ref[p, c, sel, :] * skip_ref[:, lanes]) * x0_ref[p, c, sel, :]


def _fft_stage3(qr, qi, z, x0, skip, n1, g1):
    P, _, _, C, k8, _ = z.shape
    gc = _const(g1)
    cb = min(C, FFT_LANE_TILES)
    qblk = pl.BlockSpec((None, None, cb, n1 * SUBLANES, LANES), lambda p, j, c: (p, j, c, 0, 0))
    zblk = pl.BlockSpec((None, 2, None, cb, k8, LANES), lambda p, j, c: (p, 0, j, c, 0, 0))
    const = lambda shape: pl.BlockSpec(shape, lambda p, j, c: (0,) * len(shape))
    return pl.pallas_call(
        functools.partial(_fft3_kernel, n1=n1),
        grid=(P, FFT_J, C // cb),
        in_specs=[qblk, qblk, const(g1.shape), zblk, zblk,
                  pl.BlockSpec((1, cb * LANES), lambda p, j, c: (0, c))],
        out_specs=zblk,
        out_shape=jax.ShapeDtypeStruct(z.shape, F32),
        compiler_params=_params(("parallel", "parallel", "parallel"), 52),
        name="fft_stage3",
    )(qr, qi, gc, z, x0, skip.reshape(1, C * LANES).astype(F32))


def long_conv_gate(z, x0, skip, kern):
    B, _, C, k1 = z.shape[:4]
    D = C * LANES
    n2 = FFT_N2
    n1 = 2 * k1
    assert B % 2 == 0
    t = _dft_tables(n1, n2)
    ka = _fft_stage1(kern.reshape(1, 1, FFT_J, C, n1 * SUBLANES, LANES), n1, t["f1_real"], t["tw1"])
    kf = _fft_stage2(ka[0], ka[1], n1, n2, D, t["f2"], scale=1.0 / (n1 * n2))
    kf = (kf[0].reshape(n1, n2, D), kf[1].reshape(n1, n2, D))
    pairs = lambda v: v.reshape(B // 2, 2, FFT_J, C, k1 * SUBLANES, LANES)
    a = _fft_stage1(pairs(z), n1, t["f1_cplx"], t["tw1"])
    q = _fft_stage2(a[0], a[1], n1, n2, D, t["f2"], kf=kf, g2=t["g2"], tw2=t["tw2"])
    return _fft_stage3(q[0], q[1], pairs(z), pairs(x0), skip, n1, t["g1_cplx"]).reshape(z.shape)


def _rope_tables(S):
    inv = ROPE_THETA ** (-jnp.arange(0, HEAD_DIM, 2, dtype=F32) / HEAD_DIM)
    ang = jnp.arange(S, dtype=F32)[:, None] * inv[None, :]
    cos, sin = jnp.cos(ang), jnp.sin(ang)
    cos2, sin2 = jnp.concatenate([cos, cos], axis=-1), jnp.concatenate([-sin, sin], axis=-1)
    return jnp.stack([cos2, jnp.ones_like(cos2)]), jnp.stack([sin2, jnp.zeros_like(sin2)])


def _trunk(x, p):
    B, S, D = x.shape
    H = D // HEAD_DIM
    G = len(GROUP_DILATIONS)
    T = B * S
    cos, sin = _rope_tables(S)
    x = x.reshape(T, D)
    depth = p["mix_norm"].shape[0]
    for i in range(depth):
        j = i // 2
        if i % 2 == 0:
            scale = HEAD_DIM ** -0.5
            gains = jnp.stack([p["attn_q_gain"][j] * scale, p["attn_k_gain"][j], jnp.ones_like(p["attn_k_gain"][j])],
                              axis=1).reshape(G * 3, 1, HEAD_DIM).astype(F32)
            qkv = [qkv_project(x, p["mix_norm"][i], p["attn_w_qkv"], j, gains, cos, sin, S, g) for g in range(G)]
            o = window_attention(qkv, B, S, H)
            x = matmul_resid(o, p["attn_w_out"], j, x, tm=512, tn=2048)
        else:
            z, x0 = inproj_conv_gate(x, p["mix_norm"][i], p["hy_w_in"], j, p["hy_b_in"][j], p["hy_conv_w"][j],
                                     p["hy_conv_b"][j], B, S)
            kern = conv_kernel_signal(S, D, p["hy_filt_w1"][j], p["hy_filt_b1"][j], p["hy_filt_w2"][j],
                                      p["hy_filt_b2"][j], p["hy_filt_w3"][j], p["hy_filt_b3"][j],
                                      p["hy_filt_freq"][j], p["hy_filt_w4"][j])
            y = long_conv_gate(z, x0, p["hy_skip"][j], kern)
            x = matmul_resid(y, p["hy_w_out"], j, x, p["hy_b_out"][j], tm=512, tn=2048)
        h = norm_matmul(x, p["mlp_norm"][i], p["mlp_w1"], i, relu2=True, out_dtype=BF16, tn=2048)
        x = matmul_resid(h, p["mlp_w2"], i, x, tk=4096)
    return x.reshape(B, S, D)


def kernel(x_prompt, x_sample, mix_norm, mlp_norm, attn_w_qkv, attn_q_gain, attn_k_gain, attn_w_out, hy_w_in, hy_b_in, hy_conv_w, hy_conv_b, hy_filt_w1, hy_filt_b1, hy_filt_w2, hy_filt_b2, hy_filt_w3, hy_filt_b3, hy_filt_freq, hy_filt_w4, hy_skip, hy_w_out, hy_b_out, mlp_w1, mlp_w2):
    p = dict(
        mix_norm=mix_norm, mlp_norm=mlp_norm,
        attn_w_qkv=attn_w_qkv.astype(BF16), attn_q_gain=attn_q_gain, attn_k_gain=attn_k_gain,
        attn_w_out=attn_w_out.astype(BF16),
        hy_w_in=inproj_weight_blocks(hy_w_in.astype(BF16)), hy_b_in=hy_b_in, hy_conv_w=hy_conv_w, hy_conv_b=hy_conv_b,
        hy_filt_w1=hy_filt_w1, hy_filt_b1=hy_filt_b1, hy_filt_w2=hy_filt_w2, hy_filt_b2=hy_filt_b2,
        hy_filt_w3=hy_filt_w3, hy_filt_b3=hy_filt_b3, hy_filt_freq=hy_filt_freq, hy_filt_w4=hy_filt_w4,
        hy_skip=hy_skip, hy_w_out=hy_w_out.astype(BF16), hy_b_out=hy_b_out,
        mlp_w1=mlp_w1.astype(BF16), mlp_w2=mlp_w2.astype(BF16),
    )
    return (_trunk(x_prompt, p), _trunk(x_sample, p))
```

```python
import functools
import math

import numpy as np
import jax
import jax.numpy as jnp
from jax import lax
from jax.experimental import pallas as pl
from jax.experimental.pallas import tpu as pltpu

F32 = jnp.float32
BF16 = jnp.bfloat16

HEAD_DIM = 128
GROUP_WINDOWS = (128, 512, 2048)
GROUP_DILATIONS = (1, 4, 16)
ROPE_THETA = 10000.0
FILTER_EMB = 33
FILTER_BANDS = (FILTER_EMB - 1) // 2
DECAY_TARGET = 1e-2
FAST_DECAY_PCT = 0.3
SLOW_DECAY_PCT = 1.5
EPS = 1e-6

LANES = 128
SUBLANES = 8
ATTN_TILE = 1024
ATTN_BLOCK = 128
ATTN_BATCH = 8
FFT_N2 = 128
FFT_J = FFT_N2 // SUBLANES
FFT_LANE_TILES = 4
MIB = 1024 * 1024


def _params(semantics, vmem_mib):
    return pltpu.CompilerParams(dimension_semantics=semantics, vmem_limit_bytes=vmem_mib * MIB)


def _tile(n, pref, quantum=LANES):
    if n <= pref:
        return n
    t = (pref // quantum) * quantum
    while t > quantum and n % t:
        t -= quantum
    assert n % t == 0, (n, pref)
    return t


def _split(x):
    hi = x.astype(BF16)
    lo = (x - hi.astype(F32)).astype(BF16)
    return hi, lo


def _dot(a, b):
    return jnp.dot(a, b, preferred_element_type=F32)


def _dot3(a, b):
    ah, al = _split(a)
    bh, bl = _split(b)
    return _dot(ah, bh) + (_dot(ah, bl) + _dot(al, bh))


def _dotc(c, b):
    return _dot(c, b.astype(BF16))


def _rms_rows_to(x_ref, g_ref, xn_ref, row_chunk):
    def body(c, carry):
        rows = pl.ds(pl.multiple_of(c * row_chunk, row_chunk), row_chunk)
        x = x_ref[rows, :]
        inv = lax.rsqrt(jnp.mean(x * x, axis=-1, keepdims=True) + EPS)
        xn_ref[rows, :] = (x * inv * g_ref[...]).astype(BF16)
        return carry

    lax.fori_loop(0, x_ref.shape[0] // row_chunk, body, 0)


def _store_grouped(o_ref, v):
    for c in range(o_ref.shape[1]):
        for a in range(o_ref.shape[2]):
            o_ref[:, c, a] = v[a * FFT_N2:(a + 1) * FFT_N2, c * LANES:(c + 1) * LANES].reshape(FFT_J, SUBLANES, LANES)


def _load_grouped(a_ref):
    rows = [jnp.concatenate([a_ref[:, c, s].reshape(FFT_N2, LANES) for c in range(a_ref.shape[1])], axis=1)
            for s in range(a_ref.shape[2])]
    return rows[0] if len(rows) == 1 else jnp.concatenate(rows, axis=0)


def _norm_mm_kernel(x_ref, g_ref, w_ref, *rest, has_bias, relu2, row_chunk):
    if has_bias:
        b_ref, o_ref, xn_ref = rest
    else:
        o_ref, xn_ref = rest

    @pl.when(pl.program_id(1) == 0)
    def _():
        _rms_rows_to(x_ref, g_ref, xn_ref, row_chunk)

    acc = _dot(xn_ref[...], w_ref[...])
    if has_bias:
        acc = acc + b_ref[...]
    if relu2:
        acc = jnp.maximum(acc, 0.0)
        acc = acc * acc
    o_ref[...] = acc.astype(o_ref.dtype)


def norm_matmul(x, g, w, layer, bias=None, *, relu2=False, out_dtype=F32, tm=1024, tn=1024):
    T, K = x.shape
    N = w.shape[2]
    tm = _tile(T, tm)
    tn = _tile(N, tn)
    in_specs = [
        pl.BlockSpec((tm, K), lambda i, j: (i, 0)),
        pl.BlockSpec((1, K), lambda i, j: (0, 0)),
        pl.BlockSpec((None, K, tn), lambda i, j: (layer, 0, j)),
    ]
    args = [x, g.reshape(1, K), w]
    if bias is not None:
        in_specs.append(pl.BlockSpec((1, tn), lambda i, j: (0, j)))
        args.append(bias.reshape(1, N))
    return pl.pallas_call(
        functools.partial(_norm_mm_kernel, has_bias=bias is not None, relu2=relu2, row_chunk=min(tm, 128)),
        grid=(T // tm, N // tn),
        in_specs=in_specs,
        out_specs=pl.BlockSpec((tm, tn), lambda i, j: (i, j)),
        out_shape=jax.ShapeDtypeStruct((T, N), out_dtype),
        scratch_shapes=[pltpu.VMEM((tm, K), BF16)],
        compiler_params=_params(("parallel", "arbitrary"), 56),
        name="norm_matmul",
    )(*args)


def _qkv_kernel(x_ref, g_ref, w_ref, gain_ref, cos_ref, sin_ref, o_ref, xn_ref, acc_a, acc_b, stage_ref, *,
                row_chunk, tiles_per_part, n_j, n_tiles, d):
    n = pl.program_id(0)
    tile = jnp.minimum(n, n_tiles - 1)
    done = jnp.maximum(n - 1, 0)
    tm = xn_ref.shape[0]

    @pl.when(n == 0)
    def _():
        acc_b[...] = jnp.zeros_like(acc_b)

    @pl.when((tile % n_j == 0) & (n < n_tiles))
    def _():
        _rms_rows_to(x_ref, g_ref, xn_ref, row_chunk)

    def step(acc_w, acc_r):
        acc_w[...] = _dot(xn_ref[...], w_ref[...])
        is_v = (done % n_j) // tiles_per_part == 2
        gain = gain_ref[...]
        cos = cos_ref[...]
        sin = sin_ref[...]
        for s in range(acc_r.shape[1] // HEAD_DIM):
            a = acc_r[:, s * HEAD_DIM:(s + 1) * HEAD_DIM]
            inv = jnp.where(is_v, 1.0, lax.rsqrt(jnp.mean(a * a, axis=-1, keepdims=True) + EPS))
            y = a * inv * gain
            y = y * cos + pltpu.roll(y, HEAD_DIM // 2, axis=1) * sin
            if d == 1:
                o_ref[s] = y.astype(o_ref.dtype)
            else:
                stage_ref[s] = y
                n_r = tm // d
                for r in range(d):
                    o_ref[s, r * n_r:(r + 1) * n_r, :] = stage_ref[s, pl.ds(r, n_r, stride=d), :].astype(o_ref.dtype)

    @pl.when(n % 2 == 0)
    def _():
        step(acc_a, acc_b)

    @pl.when(n % 2 == 1)
    def _():
        step(acc_b, acc_a)


def qkv_project(x, g, w, layer, gains, cos, sin, seq_len, group, *, tn=1024):
    T, K = x.shape
    part_width = w.shape[2] // (len(GROUP_DILATIONS) * 3)
    tm = ATTN_TILE
    assert seq_len % tm == 0
    tn = _tile(part_width, tn)
    pos_tiles = seq_len // tm
    tiles_per_part = part_width // tn
    n_j = 3 * tiles_per_part
    n_tiles = (T // tm) * n_j
    mm = lambda n: jnp.minimum(n, n_tiles - 1)
    ep = lambda n: jnp.maximum(n - 1, 0)
    table = lambda n: ((ep(n) % n_j) // tiles_per_part // 2, (ep(n) // n_j) % pos_tiles, 0)
    return pl.pallas_call(
        functools.partial(_qkv_kernel, row_chunk=min(tm, 128), tiles_per_part=tiles_per_part, n_j=n_j,
                          n_tiles=n_tiles, d=GROUP_DILATIONS[group]),
        grid=(n_tiles + 1,),
        in_specs=[
            pl.BlockSpec((tm, K), lambda n: (mm(n) // n_j, 0)),
            pl.BlockSpec((1, K), lambda n: (0, 0)),
            pl.BlockSpec((None, K, tn), lambda n: (layer, 0, group * n_j + mm(n) % n_j)),
            pl.BlockSpec((None, 1, HEAD_DIM), lambda n: (group * 3 + (ep(n) % n_j) // tiles_per_part, 0, 0)),
            pl.BlockSpec((None, tm, HEAD_DIM), table),
            pl.BlockSpec((None, tm, HEAD_DIM), table),
        ],
        out_specs=pl.BlockSpec(
            (tn // HEAD_DIM, None, tm, HEAD_DIM),
            lambda n: (ep(n) % tiles_per_part, ((ep(n) % n_j) // tiles_per_part + 2) % 3, ep(n) // n_j, 0)),
        out_shape=jax.ShapeDtypeStruct((part_width // HEAD_DIM, 3, T, HEAD_DIM), BF16),
        scratch_shapes=[pltpu.VMEM((tm, K), BF16), pltpu.VMEM((tm, tn), F32), pltpu.VMEM((tm, tn), F32),
                        pltpu.VMEM((tn // HEAD_DIM, tm, HEAD_DIM), F32)],
        compiler_params=_params(("arbitrary",), 52),
        name="qkv_project",
    )(x, g.reshape(1, K), w, gains, cos, sin)


def _mm_resid_kernel(a_ref, w_ref, *rest, has_bias, grouped):
    if has_bias:
        b_ref, r_ref, o_ref = rest
    else:
        r_ref, o_ref = rest
    k = pl.program_id(2)
    a = _load_grouped(a_ref) if grouped else a_ref[...]
    part = _dot(a.astype(BF16), w_ref[...])

    @pl.when(k == 0)
    def _():
        first = part + r_ref[...]
        if has_bias:
            first = first + b_ref[...]
        o_ref[...] = first

    @pl.when(k > 0)
    def _():
        o_ref[...] += part


def matmul_resid(a, w, layer, resid, bias=None, *, tm=1024, tn=1024, tk=2048):
    grouped = a.ndim == 6
    T, N = resid.shape
    K = w.shape[1]
    tn = _tile(N, tn)
    tk = _tile(K, tk)
    if grouped:
        seq_len = a.shape[3] * FFT_N2
        tm = _tile(seq_len, tm)
        seq_tiles = seq_len // tm
        a_spec = pl.BlockSpec((None, FFT_J, tk // LANES, tm // FFT_N2, SUBLANES, LANES),
                              lambda i, j, k: (i // seq_tiles, 0, k, i % seq_tiles, 0, 0))
    else:
        tm = _tile(T, tm)
        a_spec = pl.BlockSpec((tm, tk), lambda i, j, k: (i, k))
    in_specs = [
        a_spec,
        pl.BlockSpec((None, tk, tn), lambda i, j, k: (layer, k, j)),
    ]
    args = [a, w]
    if bias is not None:
        in_specs.append(pl.BlockSpec((1, tn), lambda i, j, k: (0, j)))
        args.append(bias.reshape(1, N))
    in_specs.append(pl.BlockSpec((tm, tn), lambda i, j, k: (i, j)))
    args.append(resid)
    return pl.pallas_call(
        functools.partial(_mm_resid_kernel, has_bias=bias is not None, grouped=grouped),
        grid=(T // tm, N // tn, K // tk),
        in_specs=in_specs,
        out_specs=pl.BlockSpec((tm, tn), lambda i, j, k: (i, j)),
        out_shape=jax.ShapeDtypeStruct((T, N), F32),
        compiler_params=_params(("parallel", "parallel", "arbitrary"), 58),
        name="matmul_resid",
    )(*args)


def _attn_kernel(*refs, tq, dils, half, n_tiles):
    G = len(dils)
    cur_refs, prev_refs, next_refs = refs[0:G], refs[G:2 * G], refs[2 * G:3 * G]
    o_ref = refs[3 * G]
    acc_ref, m_ref, l_ref, out_ref, bias_ref = refs[3 * G + 1:]
    i = pl.program_id(2)
    has_prev = i > 0
    has_next = i < n_tiles - 1
    dmax = max(dils)

    for g, d in enumerate(dils):
        n_r = tq // d
        qs = min(ATTN_BLOCK, n_r)
        ks = qs + 2 * half
        n_sub = n_r // qs
        qq = lax.broadcasted_iota(jnp.int32, (qs, ks), 0)
        kk = lax.broadcasted_iota(jnp.int32, (qs, ks), 1)
        band = (kk >= qq) & (kk <= qq + 2 * half)
        prev_ok = (kk >= half) | has_prev
        next_ok = (kk < ks - half) | has_next
        for idx, valid in enumerate((band & prev_ok, band, band & next_ok, band & prev_ok & next_ok)):
            bias_ref[idx, 0:qs, 0:ks] = jnp.where(valid, 0.0, -jnp.inf)

        def window(c, r, u, g=g, n_r=n_r, qs=qs):
            lo, hi = u * qs - half, (u + 1) * qs + half
            pieces = [prev_refs[g][c, r]] if lo < 0 else []
            pieces.append(cur_refs[g][c, r, max(lo, 0):min(hi, n_r), :])
            if hi > n_r:
                pieces.append(next_refs[g][c, r])
            return pieces[0] if len(pieces) == 1 else jnp.concatenate(pieces, axis=0)

        blocks = [(r, u) for r in range(d) for u in range(n_sub)]
        for b0 in range(0, len(blocks), ATTN_BATCH):
            batch = blocks[b0:b0 + ATTN_BATCH]
            edge = lambda u: (3 if n_sub == 1 else 0) if u == 0 else (2 if u == n_sub - 1 else 1)
            q = jnp.stack([cur_refs[g][2, r, u * qs:(u + 1) * qs, :] for r, u in batch])
            kw = jnp.stack([window(0, r, u) for r, u in batch])
            vw = jnp.stack([window(1, r, u) for r, u in batch])
            bias = jnp.stack([bias_ref[edge(u), 0:qs, 0:ks] for _, u in batch])
            s = jnp.einsum("bqd,bkd->bqk", q, kw, preferred_element_type=F32) + bias
            m_blk = jnp.max(s, axis=-1, keepdims=True)
            p = jnp.exp(s - m_blk)
            l_blk = jnp.sum(p, axis=-1, keepdims=True)
            pv = jnp.einsum("bqk,bkd->bqd", p.astype(BF16), vw, preferred_element_type=F32)
            st, pitch = _merge_pitch(dmax // d)
            for b, (r, u) in enumerate(batch):
                base = r * n_r + u * qs
                chunk = qs if st == pitch else st
                for k in range(qs // chunk):
                    src = slice(k * chunk, (k + 1) * chunk)
                    dst = pl.ds(base + k * chunk if st == pitch else (base // st + k) * pitch, chunk)
                    m_ref[g, dst, :] = m_blk[b][src]
                    l_ref[g, dst, :] = l_blk[b][src]
                    acc_ref[g, dst, :] = pv[b][src]

    n_max = tq // dmax
    out_pitch = n_max + SUBLANES
    for r in range(dmax):
        sel = []
        for d in dils:
            st, pitch = _merge_pitch(dmax // d)
            start = (r % d) * (tq // d // st) * pitch + r // d
            sel.append(pl.ds(start, n_max, stride=pitch * (dmax // d) // st) if d < dmax else pl.ds(start, n_max))
        ms = [m_ref[g, sel[g], :] for g in range(G)]
        m = functools.reduce(jnp.maximum, ms)
        ws = [jnp.exp(mg - m) for mg in ms]
        num = sum(ws[g] * acc_ref[g, sel[g], :] for g in range(G))
        den = sum(ws[g] * l_ref[g, sel[g], :] for g in range(G))
        out_ref[r * out_pitch:r * out_pitch + n_max, :] = num / den
    o_ref[...] = jnp.concatenate([out_ref[pl.ds(j, dmax, stride=out_pitch), :] for j in range(n_max)],
                                 axis=0).astype(o_ref.dtype)


def _merge_pitch(stride):
    if stride % SUBLANES:
        return SUBLANES, SUBLANES
    return stride, stride + SUBLANES


def window_attention(qkv, batch, seq_len, n_heads):
    G = len(GROUP_DILATIONS)
    tq = ATTN_TILE
    assert seq_len % tq == 0
    n_tiles = seq_len // tq
    total_tiles = batch * n_tiles
    half = GROUP_WINDOWS[0] // (2 * GROUP_DILATIONS[0])
    for wdw, d in zip(GROUP_WINDOWS, GROUP_DILATIONS):
        assert wdw // (2 * d) == half and half * d <= tq and tq % (d * min(ATTN_BLOCK, tq // d)) == 0
    H = n_heads
    dils = GROUP_DILATIONS

    cur, prev, nxt = [], [], []
    for d in dils:
        cur.append(pl.BlockSpec((None, 3, None, d, tq // d, HEAD_DIM),
                                lambda b, h, i: (h, 0, b * n_tiles + i, 0, 0, 0)))
        prev.append(pl.BlockSpec((None, 2, None, d, half, HEAD_DIM),
                                 lambda b, h, i, d=d: (h, 0, jnp.maximum(b * n_tiles + i - 1, 0), 0,
                                                       tq // d // half - 1, 0)))
        nxt.append(pl.BlockSpec((None, 2, None, d, half, HEAD_DIM),
                                lambda b, h, i: (h, 0, jnp.minimum(b * n_tiles + i + 1, total_tiles - 1), 0, 0, 0)))
    views = [t.reshape(H, 3, total_tiles, d, tq // d, HEAD_DIM) for t, d in zip(qkv, dils)]
    merge_rows = max(tq // st * pitch for st, pitch in (_merge_pitch(max(dils) // d) for d in dils))
    in_specs = cur + prev + nxt
    return pl.pallas_call(
        functools.partial(_attn_kernel, tq=tq, dils=dils, half=half, n_tiles=n_tiles),
        grid=(batch, H, n_tiles),
        in_specs=in_specs,
        out_specs=pl.BlockSpec((tq, HEAD_DIM), lambda b, h, i: (b * n_tiles + i, h)),
        out_shape=jax.ShapeDtypeStruct((batch * seq_len, H * HEAD_DIM), BF16),
        scratch_shapes=[
            pltpu.VMEM((G, merge_rows, HEAD_DIM), F32),
            pltpu.VMEM((G, merge_rows, 1), F32),
            pltpu.VMEM((G, merge_rows, 1), F32),
            pltpu.VMEM((max(dils) * (tq // max(dils) + SUBLANES), HEAD_DIM), F32),
            pltpu.VMEM((4, ATTN_BLOCK, ATTN_BLOCK + 2 * half), F32),
        ],
        compiler_params=_params(("parallel", "parallel", "parallel"), 40),
        name="window_attention",
    )(*(views * 3))


def _sconv_kernel(*refs, n_tiles):
    (x0m, x0p, x0n, x1m, x1p, x1n, vm, vp, vn, w0, w1, w2, b0, b1, b2, z_ref, x0_ref) = refs
    i = pl.program_id(1)

    def conv(m_ref, p_ref, n_ref, w_ref, b_ref):
        x = m_ref[...]
        ts = x.shape[0]
        prev_row = jnp.where(i > 0, p_ref[7:8, :], 0.0)
        next_row = jnp.where(i < n_tiles - 1, n_ref[0:1, :], 0.0)
        row = lax.broadcasted_iota(jnp.int32, x.shape, 0)
        up = jnp.where(row == 0, prev_row, pltpu.roll(x, 1, axis=0))
        dn = jnp.where(row == ts - 1, next_row, pltpu.roll(x, ts - 1, axis=0))
        return up * w_ref[0:1, :] + x * w_ref[1:2, :] + dn * w_ref[2:3, :] + b_ref[...]

    _store_grouped(x0_ref, conv(x0m, x0p, x0n, w0, b0))
    _store_grouped(z_ref, conv(vm, vp, vn, w2, b2) * conv(x1m, x1p, x1n, w1, b1))


def short_conv_gate(u, conv_w, conv_b, *, ts=512, ct=512):
    B, L, D3 = u.shape
    D = D3 // 3
    ts = _tile(L, ts, FFT_N2)
    ct = _tile(D, ct)
    n_tiles = L // ts
    n_ct = D // ct
    rows8 = ts // 8
    last8 = L // 8 - 1
    specs, args = [], []
    for part in range(3):
        off = part * n_ct
        specs += [
            pl.BlockSpec((None, ts, ct), lambda b, i, j, off=off: (b, i, off + j)),
            pl.BlockSpec((None, 8, ct), lambda b, i, j, off=off: (b, jnp.maximum(i * rows8 - 1, 0), off + j)),
            pl.BlockSpec((None, 8, ct), lambda b, i, j, off=off: (b, jnp.minimum((i + 1) * rows8, last8), off + j)),
        ]
        args += [u, u, u]
    for part in range(3):
        specs.append(pl.BlockSpec((3, ct), lambda b, i, j, off=part * n_ct: (0, off + j)))
        args.append(conv_w)
    for part in range(3):
        specs.append(pl.BlockSpec((1, ct), lambda b, i, j, off=part * n_ct: (0, off + j)))
        args.append(conv_b.reshape(1, D3))
    out_spec = pl.BlockSpec((None, FFT_J, ct // LANES, ts // FFT_N2, SUBLANES, LANES),
                            lambda b, i, j: (b, 0, j, i, 0, 0))
    return pl.pallas_call(
        functools.partial(_sconv_kernel, n_tiles=n_tiles),
        grid=(B, n_tiles, n_ct),
        in_specs=specs,
        out_specs=[out_spec, out_spec],
        out_shape=[jax.ShapeDtypeStruct((B, FFT_J, D // LANES, L // FFT_N2, SUBLANES, LANES), F32)] * 2,
        compiler_params=_params(("parallel", "parallel", "parallel"), 32),
        name="short_conv_gate",
    )(*args)


def _filter_kernel(feat_ref, tv_ref, w1, b1, w2, b2, w3, b3, fr_ref, w4, delta_ref, o_ref):
    fr = fr_ref[...]
    h = jnp.sin(fr * (_dot3(feat_ref[...], w1[...]) + b1[...]))
    h = jnp.sin(fr * (_dot3(h, w2[...]) + b2[...]))
    h = jnp.sin(fr * (_dot3(h, w3[...]) + b3[...]))
    out = _dot3(h, w4[...])
    t = tv_ref[:, 0:1]
    valid = tv_ref[:, 1:2]
    _store_grouped(o_ref, out * jnp.exp(-t * delta_ref[...]) * valid)


def conv_kernel_signal(L, D, fw1, fb1, fw2, fb2, fw3, fb3, ffreq, fw4, *, tl=256):
    FH = fw2.shape[0]
    FE = 64
    p = jnp.arange(2 * L, dtype=jnp.int32)
    pos = jnp.where(p < L, p, 2 * L - p).astype(F32)
    t = pos / (L - 1)
    bands = jnp.linspace(1e-4, FILTER_BANDS - 1, FILTER_BANDS, dtype=F32)
    ang = (2.0 * math.pi / L) * pos[:, None] * bands[None, :]
    feat = jnp.concatenate(
        [t[:, None], jnp.cos(ang), -jnp.sin(ang), jnp.zeros((2 * L, FE - FILTER_EMB), F32)], axis=-1)
    tv = jnp.stack([t, (p != L).astype(F32)], axis=-1)
    deltas = np.abs(np.linspace(math.log(DECAY_TARGET) / SLOW_DECAY_PCT, math.log(DECAY_TARGET) / FAST_DECAY_PCT, D,
                                dtype=np.float32)).reshape(1, D)
    w1p = jnp.pad(fw1.astype(F32), ((0, FE - FILTER_EMB), (0, 0)))
    tl = _tile(L, tl, FFT_N2)
    half_tiles = L // tl
    full = lambda shape: pl.BlockSpec(shape, lambda i: (0, 0))
    return pl.pallas_call(
        _filter_kernel,
        grid=(2 * L // tl,),
        in_specs=[
            pl.BlockSpec((tl, FE), lambda i: (i, 0)),
            pl.BlockSpec((tl, 2), lambda i: (i, 0)),
            full((FE, FH)), full((1, FH)), full((FH, FH)), full((1, FH)), full((FH, FH)), full((1, FH)),
            full((1, FH)),
            pl.BlockSpec((FH, D), lambda i: (0, i // half_tiles)),
            full((1, D)),
        ],
        out_specs=pl.BlockSpec((FFT_J, D // LANES, tl // FFT_N2, SUBLANES, LANES), lambda i: (0, 0, i, 0, 0)),
        out_shape=jax.ShapeDtypeStruct((FFT_J, D // LANES, 2 * L // FFT_N2, SUBLANES, LANES), F32),
        compiler_params=_params(("parallel",), 32),
        name="hyena_filter",
    )(feat, tv, w1p, fb1.reshape(1, FH).astype(F32), fw2.astype(F32), fb2.reshape(1, FH).astype(F32),
      fw3.astype(F32), fb3.reshape(1, FH).astype(F32), ffreq.reshape(1, FH).astype(F32), fw4.astype(F32),
      jnp.asarray(deltas))


def _const(m):
    return jnp.asarray(np.asarray(m, np.float64), dtype=F32).astype(BF16)


def _stack(re, im):
    return np.block([[re, -im], [im, re]])


def _dft_tables(n1, n2):
    n, k1 = n1 * n2, n1 // 2
    a1 = 2.0 * np.pi * np.outer(np.arange(n1), np.arange(n1)) / n1
    c1, s1 = np.cos(a1), np.sin(a1)
    a2 = 2.0 * np.pi * np.outer(np.arange(n2), np.arange(n2)) / n2
    c2, s2 = np.cos(a2), np.sin(a2)
    at = 2.0 * np.pi * np.outer(np.arange(n2), np.arange(n1)) / n
    bcast = lambda t: jnp.broadcast_to(jnp.asarray(t, F32)[:, :, None], t.shape + (LANES,))
    return dict(
        f1_real=np.concatenate([c1, -s1], axis=0),
        f1_cplx=_stack(c1[:, :k1], -s1[:, :k1]),
        f2=_stack(c2, -s2),
        g2=_stack(c2, s2),
        g1_cplx=_stack(c1[:k1], s1[:k1]),
        tw1=(bcast(np.cos(at)), bcast(-np.sin(at))),
        tw2=(bcast(np.cos(at.T)), bcast(np.sin(at.T))),
    )


def _fft1_kernel(x_ref, f_ref, twr_ref, twi_ref, ar_ref, ai_ref, *, n1):
    f = f_ref[...]
    parts, cb, rows = x_ref.shape[0], x_ref.shape[1], x_ref.shape[2] // SUBLANES
    for i in range(SUBLANES):
        sel = pl.ds(i, rows, stride=SUBLANES)
        x = jnp.concatenate([jnp.concatenate([x_ref[p, c, sel, :] for c in range(cb)], axis=1)
                             for p in range(parts)], axis=0)
        r = _dotc(f, x)
        re, im = r[:n1], r[n1:]
        twr = jnp.concatenate([twr_ref[i]] * cb, axis=1)
        twi = jnp.concatenate([twi_ref[i]] * cb, axis=1)
        ar = re * twr - im * twi
        ai = re * twi + im * twr
        for c in range(cb):
            ar_ref[c, pl.ds(i, n1, stride=SUBLANES), :] = ar[:, c * LANES:(c + 1) * LANES]
            ai_ref[c, pl.ds(i, n1, stride=SUBLANES), :] = ai[:, c * LANES:(c + 1) * LANES]


def _fft_stage1(x, n1, f1, tw1):
    P, parts, _, C, k8, _ = x.shape
    fc = _const(f1)
    cb = min(C, FFT_LANE_TILES)
    const = lambda shape: pl.BlockSpec(shape, lambda j, p, c: (0,) * len(shape))
    out_spec = pl.BlockSpec((None, None, cb, n1 * SUBLANES, LANES), lambda j, p, c: (p, j, c, 0, 0))
    tw_spec = pl.BlockSpec((SUBLANES, n1, LANES), lambda j, p, c: (j, 0, 0))
    return pl.pallas_call(
        functools.partial(_fft1_kernel, n1=n1),
        grid=(FFT_J, P, C // cb),
        in_specs=[
            pl.BlockSpec((None, parts, None, cb, k8, LANES), lambda j, p, c: (p, 0, j, c, 0, 0)),
            const(f1.shape), tw_spec, tw_spec,
        ],
        out_specs=[out_spec, out_spec],
        out_shape=[jax.ShapeDtypeStruct((P, FFT_J, C, n1 * SUBLANES, LANES), F32)] * 2,
        compiler_params=_params(("parallel", "parallel", "parallel"), 40),
        name="fft_stage1",
    )(x, fc, tw1[0], tw1[1])


def _fft2_kernel(*refs, n2, cc, scale, spectrum_only):
    if spectrum_only:
        ar_ref, ai_ref, f_ref, yr_ref, yi_ref = refs
    else:
        ar_ref, ai_ref, kr_ref, ki_ref, f_ref, g_ref, twr_ref, twi_ref, yr_ref, yi_ref = refs
        twr = jnp.concatenate([twr_ref[...]] * (cc // LANES), axis=1)
        twi = jnp.concatenate([twi_ref[...]] * (cc // LANES), axis=1)
    f = f_ref[...]
    lt = cc // LANES
    rows = lambda ref, c: jnp.concatenate([ref[:, c * lt + t].reshape(n2, LANES) for t in range(lt)], axis=1)
    for c in range(ar_ref.shape[1] // lt):
        cols = slice(c * cc, (c + 1) * cc)
        y = _dotc(f, jnp.concatenate([rows(ar_ref, c), rows(ai_ref, c)], axis=0))
        yr, yi = y[:n2], y[n2:]
        if spectrum_only:
            yr_ref[:, cols] = (yr * scale).astype(yr_ref.dtype)
            yi_ref[:, cols] = (yi * scale).astype(yi_ref.dtype)
        else:
            kr = kr_ref[:, cols].astype(F32)
            ki = ki_ref[:, cols].astype(F32)
            pr = yr * kr - yi * ki
            pi = yr * ki + yi * kr
            q = _dotc(g_ref[...], jnp.concatenate([pr, pi], axis=0))
            qr, qi = q[:n2], q[n2:]
            outr = qr * twr - qi * twi
            outi = qr * twi + qi * twr
            for t in range(lt):
                yr_ref[:, c * lt + t] = outr[:, t * LANES:(t + 1) * LANES].reshape(FFT_J, SUBLANES, LANES)
                yi_ref[:, c * lt + t] = outi[:, t * LANES:(t + 1) * LANES].reshape(FFT_J, SUBLANES, LANES)


def _fft_stage2(ar, ai, n1, n2, D, f2, *, kf=None, g2=None, tw2=None, scale=1.0):
    B, _, C = ar.shape[:3]
    ar = ar.reshape(B, FFT_J, C, n1, SUBLANES, LANES)
    ai = ai.reshape(B, FFT_J, C, n1, SUBLANES, LANES)
    fc = _const(f2)
    cc = _tile(D, 512)
    blk = pl.BlockSpec((None, FFT_J, C, None, SUBLANES, LANES), lambda f, b: (b, 0, 0, f, 0, 0))
    const = lambda shape: pl.BlockSpec(shape, lambda f, b: (0,) * len(shape))
    specs = [blk, blk]
    args = [ar, ai]
    if kf is not None:
        gc = _const(g2)
        kblk = pl.BlockSpec((None, n2, D), lambda f, b: (f, 0, 0))
        tblk = pl.BlockSpec((None, n2, LANES), lambda f, b: (f, 0, 0))
        specs += [kblk, kblk, const((2 * n2, 2 * n2)), const((2 * n2, 2 * n2)), tblk, tblk]
        args += [kf[0], kf[1], fc, gc, tw2[0], tw2[1]]
        out_spec = blk
        out_shape = jax.ShapeDtypeStruct((B, FFT_J, C, n1, SUBLANES, LANES), F32)
    else:
        specs += [const((2 * n2, 2 * n2))]
        args += [fc]
        out_spec = pl.BlockSpec((None, None, n2, D), lambda f, b: (b, f, 0, 0))
        out_shape = jax.ShapeDtypeStruct((B, n1, n2, D), BF16)
    out = pl.pallas_call(
        functools.partial(_fft2_kernel, n2=n2, cc=cc, scale=scale, spectrum_only=kf is None),
        grid=(n1, B),
        in_specs=specs,
        out_specs=[out_spec, out_spec],
        out_shape=[out_shape] * 2,
        compiler_params=_params(("parallel", "parallel"), 40),
        name="fft_stage2",
    )(*args)
    if kf is not None:
        out = [o.reshape(B, FFT_J, C, n1 * SUBLANES, LANES) for o in out]
    return out


def _fft3_kernel(qr_ref, qi_ref, g_ref, z_ref, x0_ref, skip_ref, o_ref, *, n1):
    g = g_ref[...]
    cb, rows = z_ref.shape[1], z_ref.shape[2] // SUBLANES
    for i in range(SUBLANES):
        qsel = pl.ds(i, n1, stride=SUBLANES)
        sel = pl.ds(i, rows, stride=SUBLANES)
        q = jnp.concatenate([jnp.concatenate([ref[c, qsel, :] for c in range(cb)], axis=1)
                             for ref in (qr_ref, qi_ref)], axis=0)
        y = _dotc(g, q)
        for p in range(2):
            for c in range(cb):
                lanes = slice(c * LANES, (c + 1) * LANES)
                conv = y[p * rows:(p + 1) * rows, lanes]
                o_ref[p, c, sel, :] = (conv + z_ref[p, c, sel, :] * skip_ref[:, lanes]) * x0_ref[p, c, sel, :]


def _fft_stage3(qr, qi, z, x0, skip, n1, g1):
    P, _, _, C, k8, _ = z.shape
    gc = _const(g1)
    cb = min(C, FFT_LANE_TILES // 2)
    qblk = pl.BlockSpec((None, None, cb, n1 * SUBLANES, LANES), lambda p, j, c: (p, j, c, 0, 0))
    zblk = pl.BlockSpec((None, 2, None, cb, k8, LANES), lambda p, j, c: (p, 0, j, c, 0, 0))
    const = lambda shape: pl.BlockSpec(shape, lambda p, j, c: (0,) * len(shape))
    return pl.pallas_call(
        functools.partial(_fft3_kernel, n1=n1),
        grid=(P, FFT_J, C // cb),
        in_specs=[qblk, qblk, const(g1.shape), zblk, zblk,
                  pl.BlockSpec((1, cb * LANES), lambda p, j, c: (0, c))],
        out_specs=zblk,
        out_shape=jax.ShapeDtypeStruct(z.shape, F32),
        compiler_params=_params(("parallel", "parallel", "parallel"), 52),
        name="fft_stage3",
    )(qr, qi, gc, z, x0, skip.reshape(1, C * LANES).astype(F32))


def long_conv_gate(z, x0, skip, kern):
    B, _, C, k1 = z.shape[:4]
    D = C * LANES
    n2 = FFT_N2
    n1 = 2 * k1
    assert B % 2 == 0
    t = _dft_tables(n1, n2)
    ka = _fft_stage1(kern.reshape(1, 1, FFT_J, C, n1 * SUBLANES, LANES), n1, t["f1_real"], t["tw1"])
    kf = _fft_stage2(ka[0], ka[1], n1, n2, D, t["f2"], scale=1.0 / (n1 * n2))
    kf = (kf[0].reshape(n1, n2, D), kf[1].reshape(n1, n2, D))
    pairs = lambda v: v.reshape(B // 2, 2, FFT_J, C, k1 * SUBLANES, LANES)
    a = _fft_stage1(pairs(z), n1, t["f1_cplx"], t["tw1"])
    q = _fft_stage2(a[0], a[1], n1, n2, D, t["f2"], kf=kf, g2=t["g2"], tw2=t["tw2"])
    return _fft_stage3(q[0], q[1], pairs(z), pairs(x0), skip, n1, t["g1_cplx"]).reshape(z.shape)


def _rope_tables(S):
    inv = ROPE_THETA ** (-jnp.arange(0, HEAD_DIM, 2, dtype=F32) / HEAD_DIM)
    ang = jnp.arange(S, dtype=F32)[:, None] * inv[None, :]
    cos, sin = jnp.cos(ang), jnp.sin(ang)
    cos2, sin2 = jnp.concatenate([cos, cos], axis=-1), jnp.concatenate([-sin, sin], axis=-1)
    return jnp.stack([cos2, jnp.ones_like(cos2)]), jnp.stack([sin2, jnp.zeros_like(sin2)])


def _trunk(x, p):
    B, S, D = x.shape
    H = D // HEAD_DIM
    G = len(GROUP_DILATIONS)
    T = B * S
    cos, sin = _rope_tables(S)
    x = x.reshape(T, D)
    depth = p["mix_norm"].shape[0]
    for i in range(depth):
        j = i // 2
        if i % 2 == 0:
            scale = HEAD_DIM ** -0.5
            gains = jnp.stack([p["attn_q_gain"][j] * scale, p["attn_k_gain"][j], jnp.ones_like(p["attn_k_gain"][j])],
                              axis=1).reshape(G * 3, 1, HEAD_DIM).astype(F32)
            qkv = [qkv_project(x, p["mix_norm"][i], p["attn_w_qkv"], j, gains, cos, sin, S, g) for g in range(G)]
            o = window_attention(qkv, B, S, H)
            x = matmul_resid(o, p["attn_w_out"], j, x, tm=512, tn=2048)
        else:
            u = norm_matmul(x, p["mix_norm"][i], p["hy_w_in"], j, p["hy_b_in"][j])
            z, x0 = short_conv_gate(u.reshape(B, S, 3 * D), p["hy_conv_w"][j], p["hy_conv_b"][j])
            kern = conv_kernel_signal(S, D, p["hy_filt_w1"][j], p["hy_filt_b1"][j], p["hy_filt_w2"][j],
                                      p["hy_filt_b2"][j], p["hy_filt_w3"][j], p["hy_filt_b3"][j],
                                      p["hy_filt_freq"][j], p["hy_filt_w4"][j])
            y = long_conv_gate(z, x0, p["hy_skip"][j], kern)
            x = matmul_resid(y, p["hy_w_out"], j, x, p["hy_b_out"][j], tm=512, tn=2048)
        h = norm_matmul(x, p["mlp_norm"][i], p["mlp_w1"], i, relu2=True, out_dtype=BF16, tn=2048)
        x = matmul_resid(h, p["mlp_w2"], i, x, tk=4096)
    return x.reshape(B, S, D)


def kernel(x_prompt, x_sample, mix_norm, mlp_norm, attn_w_qkv, attn_q_gain, attn_k_gain, attn_w_out, hy_w_in, hy_b_in, hy_conv_w, hy_conv_b, hy_filt_w1, hy_filt_b1, hy_filt_w2, hy_filt_b2, hy_filt_w3, hy_filt_b3, hy_filt_freq, hy_filt_w4, hy_skip, hy_w_out, hy_b_out, mlp_w1, mlp_w2):
    p = dict(
        mix_norm=mix_norm, mlp_norm=mlp_norm,
        attn_w_qkv=attn_w_qkv.astype(BF16), attn_q_gain=attn_q_gain, attn_k_gain=attn_k_gain,
        attn_w_out=attn_w_out.astype(BF16),
        hy_w_in=hy_w_in.astype(BF16), hy_b_in=hy_b_in, hy_conv_w=hy_conv_w, hy_conv_b=hy_conv_b,
        hy_filt_w1=hy_filt_w1, hy_filt_b1=hy_filt_b1, hy_filt_w2=hy_filt_w2, hy_filt_b2=hy_filt_b2,
        hy_filt_w3=hy_filt_w3, hy_filt_b3=hy_filt_b3, hy_filt_freq=hy_filt_freq, hy_filt_w4=hy_filt_w4,
        hy_skip=hy_skip, hy_w_out=hy_w_out.astype(BF16), hy_b_out=hy_b_out,
        mlp_w1=mlp_w1.astype(BF16), mlp_w2=mlp_w2.astype(BF16),
    )
    return (_trunk(x_prompt, p), _trunk(x_sample, p))
```

```python
import functools
import math

import numpy as np
import jax
import jax.numpy as jnp
from jax import lax
from jax.experimental import pallas as pl
from jax.experimental.pallas import tpu as pltpu

F32 = jnp.float32
BF16 = jnp.bfloat16

HEAD_DIM = 128
GROUP_WINDOWS = (128, 512, 2048)
GROUP_DILATIONS = (1, 4, 16)
ROPE_THETA = 10000.0
FILTER_EMB = 33
FILTER_BANDS = (FILTER_EMB - 1) // 2
DECAY_TARGET = 1e-2
FAST_DECAY_PCT = 0.3
SLOW_DECAY_PCT = 1.5
EPS = 1e-6

LANES = 128
SUBLANES = 8
ATTN_TILE = 1024
ATTN_HEADS = 2
ATTN_BATCH = 8
FFT_N2 = 128
FFT_J = FFT_N2 // SUBLANES
FFT_LANE_TILES = 4
MIB = 1024 * 1024


def _params(semantics, vmem_mib):
    return pltpu.CompilerParams(dimension_semantics=semantics, vmem_limit_bytes=vmem_mib * MIB)


def _tile(n, pref, quantum=LANES):
    if n <= pref:
        return n
    t = (pref // quantum) * quantum
    while t > quantum and n % t:
        t -= quantum
    assert n % t == 0, (n, pref)
    return t


def _split(x):
    hi = x.astype(BF16)
    lo = (x - hi.astype(F32)).astype(BF16)
    return hi, lo


def _dot(a, b):
    return jnp.dot(a, b, preferred_element_type=F32)


def _dot3(a, b):
    ah, al = _split(a)
    bh, bl = _split(b)
    return _dot(ah, bh) + (_dot(ah, bl) + _dot(al, bh))


def _dotc(c, b):
    return _dot(c, b.astype(BF16))


def _rms_rows_to(x_ref, g_ref, xn_ref, row_chunk):
    def body(c, carry):
        rows = pl.ds(pl.multiple_of(c * row_chunk, row_chunk), row_chunk)
        x = x_ref[rows, :]
        inv = lax.rsqrt(jnp.mean(x * x, axis=-1, keepdims=True) + EPS)
        xn_ref[rows, :] = (x * inv * g_ref[...]).astype(BF16)
        return carry

    lax.fori_loop(0, x_ref.shape[0] // row_chunk, body, 0)


def _store_grouped(o_ref, v):
    for c in range(o_ref.shape[1]):
        for a in range(o_ref.shape[2]):
            o_ref[:, c, a] = v[a * FFT_N2:(a + 1) * FFT_N2, c * LANES:(c + 1) * LANES].reshape(FFT_J, SUBLANES, LANES)


def _load_grouped(a_ref):
    rows = [jnp.concatenate([a_ref[:, c, s].reshape(FFT_N2, LANES) for c in range(a_ref.shape[1])], axis=1)
            for s in range(a_ref.shape[2])]
    return rows[0] if len(rows) == 1 else jnp.concatenate(rows, axis=0)


def _norm_mm_kernel(x_ref, g_ref, w_ref, *rest, has_bias, relu2, row_chunk):
    if has_bias:
        b_ref, o_ref, xn_ref = rest
    else:
        o_ref, xn_ref = rest

    @pl.when(pl.program_id(1) == 0)
    def _():
        _rms_rows_to(x_ref, g_ref, xn_ref, row_chunk)

    acc = _dot(xn_ref[...], w_ref[...])
    if has_bias:
        acc = acc + b_ref[...]
    if relu2:
        acc = jnp.maximum(acc, 0.0)
        acc = acc * acc
    o_ref[...] = acc.astype(o_ref.dtype)


def norm_matmul(x, g, w, layer, bias=None, *, relu2=False, out_dtype=F32, tm=1024, tn=1024):
    T, K = x.shape
    N = w.shape[2]
    tm = _tile(T, tm)
    tn = _tile(N, tn)
    in_specs = [
        pl.BlockSpec((tm, K), lambda i, j: (i, 0)),
        pl.BlockSpec((1, K), lambda i, j: (0, 0)),
        pl.BlockSpec((None, K, tn), lambda i, j: (layer, 0, j)),
    ]
    args = [x, g.reshape(1, K), w]
    if bias is not None:
        in_specs.append(pl.BlockSpec((1, tn), lambda i, j: (0, j)))
        args.append(bias.reshape(1, N))
    return pl.pallas_call(
        functools.partial(_norm_mm_kernel, has_bias=bias is not None, relu2=relu2, row_chunk=min(tm, 128)),
        grid=(T // tm, N // tn),
        in_specs=in_specs,
        out_specs=pl.BlockSpec((tm, tn), lambda i, j: (i, j)),
        out_shape=jax.ShapeDtypeStruct((T, N), out_dtype),
        scratch_shapes=[pltpu.VMEM((tm, K), BF16)],
        compiler_params=_params(("parallel", "arbitrary"), 56),
        name="norm_matmul",
    )(*args)


def _qkv_kernel(x_ref, g_ref, w_ref, gain_ref, cos_ref, sin_ref, o_ref, xn_ref, acc_a, acc_b, stage_ref, *,
                row_chunk, tiles_per_part, n_j, n_tiles, d):
    n = pl.program_id(0)
    tile = jnp.minimum(n, n_tiles - 1)
    done = jnp.maximum(n - 1, 0)
    tm = xn_ref.shape[0]

    @pl.when(n == 0)
    def _():
        acc_b[...] = jnp.zeros_like(acc_b)

    @pl.when((tile % n_j == 0) & (n < n_tiles))
    def _():
        _rms_rows_to(x_ref, g_ref, xn_ref, row_chunk)

    def step(acc_w, acc_r):
        acc_w[...] = _dot(xn_ref[...], w_ref[...])
        is_v = (done % n_j) // tiles_per_part == 2
        gain = gain_ref[...]
        cos = cos_ref[...]
        sin = sin_ref[...]
        for s in range(acc_r.shape[1] // HEAD_DIM):
            a = acc_r[:, s * HEAD_DIM:(s + 1) * HEAD_DIM]
            inv = jnp.where(is_v, 1.0, lax.rsqrt(jnp.mean(a * a, axis=-1, keepdims=True) + EPS))
            y = a * inv * gain
            y = y * cos + pltpu.roll(y, HEAD_DIM // 2, axis=1) * sin
            if d == 1:
                o_ref[s] = y.astype(o_ref.dtype)
            else:
                stage_ref[s] = y
                n_r = tm // d
                for r in range(d):
                    o_ref[s, r * n_r:(r + 1) * n_r, :] = stage_ref[s, pl.ds(r, n_r, stride=d), :].astype(o_ref.dtype)

    @pl.when(n % 2 == 0)
    def _():
        step(acc_a, acc_b)

    @pl.when(n % 2 == 1)
    def _():
        step(acc_b, acc_a)


def qkv_project(x, g, w, layer, gains, cos, sin, seq_len, group, *, tn=1024):
    T, K = x.shape
    part_width = w.shape[2] // (len(GROUP_DILATIONS) * 3)
    tm = ATTN_TILE
    assert seq_len % tm == 0
    tn = _tile(part_width, tn)
    pos_tiles = seq_len // tm
    tiles_per_part = part_width // tn
    n_j = 3 * tiles_per_part
    n_tiles = (T // tm) * n_j
    mm = lambda n: jnp.minimum(n, n_tiles - 1)
    ep = lambda n: jnp.maximum(n - 1, 0)
    table = lambda n: ((ep(n) % n_j) // tiles_per_part // 2, (ep(n) // n_j) % pos_tiles, 0)
    return pl.pallas_call(
        functools.partial(_qkv_kernel, row_chunk=min(tm, 128), tiles_per_part=tiles_per_part, n_j=n_j,
                          n_tiles=n_tiles, d=GROUP_DILATIONS[group]),
        grid=(n_tiles + 1,),
        in_specs=[
            pl.BlockSpec((tm, K), lambda n: (mm(n) // n_j, 0)),
            pl.BlockSpec((1, K), lambda n: (0, 0)),
            pl.BlockSpec((None, K, tn), lambda n: (layer, 0, group * n_j + mm(n) % n_j)),
            pl.BlockSpec((None, 1, HEAD_DIM), lambda n: (group * 3 + (ep(n) % n_j) // tiles_per_part, 0, 0)),
            pl.BlockSpec((None, tm, HEAD_DIM), table),
            pl.BlockSpec((None, tm, HEAD_DIM), table),
        ],
        out_specs=pl.BlockSpec(
            (tn // HEAD_DIM, None, tm, HEAD_DIM),
            lambda n: (ep(n) % tiles_per_part, ((ep(n) % n_j) // tiles_per_part + 2) % 3, ep(n) // n_j, 0)),
        out_shape=jax.ShapeDtypeStruct((part_width // HEAD_DIM, 3, T, HEAD_DIM), BF16),
        scratch_shapes=[pltpu.VMEM((tm, K), BF16), pltpu.VMEM((tm, tn), F32), pltpu.VMEM((tm, tn), F32),
                        pltpu.VMEM((tn // HEAD_DIM, tm, HEAD_DIM), F32)],
        compiler_params=_params(("arbitrary",), 52),
        name="qkv_project",
    )(x, g.reshape(1, K), w, gains, cos, sin)


def _mm_resid_kernel(a_ref, w_ref, *rest, has_bias, grouped):
    if has_bias:
        b_ref, r_ref, o_ref = rest
    else:
        r_ref, o_ref = rest
    k = pl.program_id(2)
    a = _load_grouped(a_ref) if grouped else a_ref[...]
    part = _dot(a.astype(BF16), w_ref[...])

    @pl.when(k == 0)
    def _():
        first = part + r_ref[...]
        if has_bias:
            first = first + b_ref[...]
        o_ref[...] = first

    @pl.when(k > 0)
    def _():
        o_ref[...] += part


def matmul_resid(a, w, layer, resid, bias=None, *, tm=1024, tn=1024, tk=2048):
    grouped = a.ndim == 6
    T, N = resid.shape
    K = w.shape[1]
    tn = _tile(N, tn)
    tk = _tile(K, tk)
    if grouped:
        seq_len = a.shape[3] * FFT_N2
        tm = _tile(seq_len, tm)
        seq_tiles = seq_len // tm
        a_spec = pl.BlockSpec((None, FFT_J, tk // LANES, tm // FFT_N2, SUBLANES, LANES),
                              lambda i, j, k: (i // seq_tiles, 0, k, i % seq_tiles, 0, 0))
    else:
        tm = _tile(T, tm)
        a_spec = pl.BlockSpec((tm, tk), lambda i, j, k: (i, k))
    in_specs = [
        a_spec,
        pl.BlockSpec((None, tk, tn), lambda i, j, k: (layer, k, j)),
    ]
    args = [a, w]
    if bias is not None:
        in_specs.append(pl.BlockSpec((1, tn), lambda i, j, k: (0, j)))
        args.append(bias.reshape(1, N))
    in_specs.append(pl.BlockSpec((tm, tn), lambda i, j, k: (i, j)))
    args.append(resid)
    return pl.pallas_call(
        functools.partial(_mm_resid_kernel, has_bias=bias is not None, grouped=grouped),
        grid=(T // tm, N // tn, K // tk),
        in_specs=in_specs,
        out_specs=pl.BlockSpec((tm, tn), lambda i, j, k: (i, j)),
        out_shape=jax.ShapeDtypeStruct((T, N), F32),
        compiler_params=_params(("parallel", "parallel", "arbitrary"), 58),
        name="matmul_resid",
    )(*args)


def _attn_kernel(*refs, tq, dils, half, n_tiles):
    n_in = 3 * len(dils)
    o_ref = refs[n_in]
    for hh in range(refs[0].shape[0]):
        _attn_head([r.at[hh] for r in refs[:n_in]], o_ref.at[:, hh * HEAD_DIM:(hh + 1) * HEAD_DIM], refs[n_in + 1:],
                   tq=tq, dils=dils, half=half, n_tiles=n_tiles)


def _attn_head(refs, o_ref, scratch, *, tq, dils, half, n_tiles):
    G = len(dils)
    cur_refs, prev_refs, next_refs = refs[0:G], refs[G:2 * G], refs[2 * G:3 * G]
    acc_ref, m_ref, l_ref, out_ref, bias_ref = scratch
    i = pl.program_id(2)
    has_prev = i > 0
    has_next = i < n_tiles - 1
    dmax = max(dils)

    for g, d in enumerate(dils):
        n_r = tq // d
        qs = min(128, n_r)
        ks = qs + 2 * half
        n_sub = n_r // qs
        qq = lax.broadcasted_iota(jnp.int32, (qs, ks), 0)
        kk = lax.broadcasted_iota(jnp.int32, (qs, ks), 1)
        band = (kk >= qq) & (kk <= qq + 2 * half)
        prev_ok = (kk >= half) | has_prev
        next_ok = (kk < ks - half) | has_next
        for idx, valid in enumerate((band & prev_ok, band, band & next_ok, band & prev_ok & next_ok)):
            bias_ref[idx, 0:qs, 0:ks] = jnp.where(valid, 0.0, -jnp.inf)

        def window(c, r, u, g=g, n_r=n_r, qs=qs):
            lo, hi = u * qs - half, (u + 1) * qs + half
            pieces = [prev_refs[g][c, r]] if lo < 0 else []
            pieces.append(cur_refs[g][c, r, max(lo, 0):min(hi, n_r), :])
            if hi > n_r:
                pieces.append(next_refs[g][c, r])
            return pieces[0] if len(pieces) == 1 else jnp.concatenate(pieces, axis=0)

        blocks = [(r, u) for r in range(d) for u in range(n_sub)]
        for b0 in range(0, len(blocks), ATTN_BATCH):
            batch = blocks[b0:b0 + ATTN_BATCH]
            edge = lambda u: (3 if n_sub == 1 else 0) if u == 0 else (2 if u == n_sub - 1 else 1)
            q = jnp.stack([cur_refs[g][2, r, u * qs:(u + 1) * qs, :] for r, u in batch])
            kw = jnp.stack([window(0, r, u) for r, u in batch])
            vw = jnp.stack([window(1, r, u) for r, u in batch])
            bias = jnp.stack([bias_ref[edge(u), 0:qs, 0:ks] for _, u in batch])
            s = jnp.einsum("bqd,bkd->bqk", q, kw, preferred_element_type=F32) + bias
            m_blk = jnp.max(s, axis=-1, keepdims=True)
            p = jnp.exp(s - m_blk)
            l_blk = jnp.sum(p, axis=-1, keepdims=True)
            pv = jnp.einsum("bqk,bkd->bqd", p.astype(BF16), vw, preferred_element_type=F32)
            st, pitch = _merge_pitch(dmax // d)
            for b, (r, u) in enumerate(batch):
                base = r * n_r + u * qs
                chunk = qs if st == pitch else st
                for k in range(qs // chunk):
                    src = slice(k * chunk, (k + 1) * chunk)
                    dst = pl.ds(base + k * chunk if st == pitch else (base // st + k) * pitch, chunk)
                    m_ref[g, dst, :] = m_blk[b][src]
                    l_ref[g, dst, :] = l_blk[b][src]
                    acc_ref[g, dst, :] = pv[b][src]

    n_max = tq // dmax
    out_pitch = n_max + SUBLANES
    for r in range(dmax):
        sel = []
        for d in dils:
            st, pitch = _merge_pitch(dmax // d)
            start = (r % d) * (tq // d // st) * pitch + r // d
            sel.append(pl.ds(start, n_max, stride=pitch * (dmax // d) // st) if d < dmax else pl.ds(start, n_max))
        ms = [m_ref[g, sel[g], :] for g in range(G)]
        m = functools.reduce(jnp.maximum, ms)
        ws = [jnp.exp(mg - m) for mg in ms]
        num = sum(ws[g] * acc_ref[g, sel[g], :] for g in range(G))
        den = sum(ws[g] * l_ref[g, sel[g], :] for g in range(G))
        out_ref[r * out_pitch:r * out_pitch + n_max, :] = num / den
    o_ref[...] = jnp.concatenate([out_ref[pl.ds(j, dmax, stride=out_pitch), :] for j in range(n_max)],
                                 axis=0).astype(o_ref.dtype)


def _merge_pitch(stride):
    if stride % SUBLANES:
        return SUBLANES, SUBLANES
    return stride, stride + SUBLANES


def window_attention(qkv, batch, seq_len, n_heads):
    G = len(GROUP_DILATIONS)
    tq = ATTN_TILE
    assert seq_len % tq == 0
    n_tiles = seq_len // tq
    total_tiles = batch * n_tiles
    half = GROUP_WINDOWS[0] // (2 * GROUP_DILATIONS[0])
    for wdw, d in zip(GROUP_WINDOWS, GROUP_DILATIONS):
        assert wdw // (2 * d) == half and half * d <= tq and tq % (d * min(128, tq // d)) == 0
    H = n_heads
    dils = GROUP_DILATIONS

    cur, prev, nxt = [], [], []
    hb = ATTN_HEADS if H % ATTN_HEADS == 0 else 1
    for d in dils:
        cur.append(pl.BlockSpec((hb, 3, None, d, tq // d, HEAD_DIM),
                                lambda b, h, i: (h, 0, b * n_tiles + i, 0, 0, 0)))
        prev.append(pl.BlockSpec((hb, 2, None, d, half, HEAD_DIM),
                                 lambda b, h, i, d=d: (h, 0, jnp.maximum(b * n_tiles + i - 1, 0), 0,
                                                       tq // d // half - 1, 0)))
        nxt.append(pl.BlockSpec((hb, 2, None, d, half, HEAD_DIM),
                                lambda b, h, i: (h, 0, jnp.minimum(b * n_tiles + i + 1, total_tiles - 1), 0, 0, 0)))
    views = [t.reshape(H, 3, total_tiles, d, tq // d, HEAD_DIM) for t, d in zip(qkv, dils)]
    merge_rows = max(tq // st * pitch for st, pitch in (_merge_pitch(max(dils) // d) for d in dils))
    in_specs = cur + prev + nxt
    return pl.pallas_call(
        functools.partial(_attn_kernel, tq=tq, dils=dils, half=half, n_tiles=n_tiles),
        grid=(batch, H // hb, n_tiles),
        in_specs=in_specs,
        out_specs=pl.BlockSpec((tq, hb * HEAD_DIM), lambda b, h, i: (b * n_tiles + i, h)),
        out_shape=jax.ShapeDtypeStruct((batch * seq_len, H * HEAD_DIM), BF16),
        scratch_shapes=[
            pltpu.VMEM((G, merge_rows, HEAD_DIM), F32),
            pltpu.VMEM((G, merge_rows, 1), F32),
            pltpu.VMEM((G, merge_rows, 1), F32),
            pltpu.VMEM((max(dils) * (tq // max(dils) + SUBLANES), HEAD_DIM), F32),
            pltpu.VMEM((4, 128, 128 + 2 * half), F32),
        ],
        compiler_params=_params(("parallel", "parallel", "parallel"), 40),
        name="window_attention",
    )(*(views * 3))


def _sconv_kernel(*refs, n_tiles):
    (x0m, x0p, x0n, x1m, x1p, x1n, vm, vp, vn, w0, w1, w2, b0, b1, b2, z_ref, x0_ref) = refs
    i = pl.program_id(1)

    def conv(m_ref, p_ref, n_ref, w_ref, b_ref):
        x = m_ref[...]
        ts = x.shape[0]
        prev_row = jnp.where(i > 0, p_ref[7:8, :], 0.0)
        next_row = jnp.where(i < n_tiles - 1, n_ref[0:1, :], 0.0)
        row = lax.broadcasted_iota(jnp.int32, x.shape, 0)
        up = jnp.where(row == 0, prev_row, pltpu.roll(x, 1, axis=0))
        dn = jnp.where(row == ts - 1, next_row, pltpu.roll(x, ts - 1, axis=0))
        return up * w_ref[0:1, :] + x * w_ref[1:2, :] + dn * w_ref[2:3, :] + b_ref[...]

    _store_grouped(x0_ref, conv(x0m, x0p, x0n, w0, b0))
    _store_grouped(z_ref, conv(vm, vp, vn, w2, b2) * conv(x1m, x1p, x1n, w1, b1))


def short_conv_gate(u, conv_w, conv_b, *, ts=512, ct=512):
    B, L, D3 = u.shape
    D = D3 // 3
    ts = _tile(L, ts, FFT_N2)
    ct = _tile(D, ct)
    n_tiles = L // ts
    n_ct = D // ct
    rows8 = ts // 8
    last8 = L // 8 - 1
    specs, args = [], []
    for part in range(3):
        off = part * n_ct
        specs += [
            pl.BlockSpec((None, ts, ct), lambda b, i, j, off=off: (b, i, off + j)),
            pl.BlockSpec((None, 8, ct), lambda b, i, j, off=off: (b, jnp.maximum(i * rows8 - 1, 0), off + j)),
            pl.BlockSpec((None, 8, ct), lambda b, i, j, off=off: (b, jnp.minimum((i + 1) * rows8, last8), off + j)),
        ]
        args += [u, u, u]
    for part in range(3):
        specs.append(pl.BlockSpec((3, ct), lambda b, i, j, off=part * n_ct: (0, off + j)))
        args.append(conv_w)
    for part in range(3):
        specs.append(pl.BlockSpec((1, ct), lambda b, i, j, off=part * n_ct: (0, off + j)))
        args.append(conv_b.reshape(1, D3))
    out_spec = pl.BlockSpec((None, FFT_J, ct // LANES, ts // FFT_N2, SUBLANES, LANES),
                            lambda b, i, j: (b, 0, j, i, 0, 0))
    return pl.pallas_call(
        functools.partial(_sconv_kernel, n_tiles=n_tiles),
        grid=(B, n_tiles, n_ct),
        in_specs=specs,
        out_specs=[out_spec, out_spec],
        out_shape=[jax.ShapeDtypeStruct((B, FFT_J, D // LANES, L // FFT_N2, SUBLANES, LANES), F32)] * 2,
        compiler_params=_params(("parallel", "parallel", "parallel"), 32),
        name="short_conv_gate",
    )(*args)


def _filter_kernel(feat_ref, tv_ref, w1, b1, w2, b2, w3, b3, fr_ref, w4, delta_ref, o_ref):
    fr = fr_ref[...]
    h = jnp.sin(fr * (_dot3(feat_ref[...], w1[...]) + b1[...]))
    h = jnp.sin(fr * (_dot3(h, w2[...]) + b2[...]))
    h = jnp.sin(fr * (_dot3(h, w3[...]) + b3[...]))
    out = _dot3(h, w4[...])
    t = tv_ref[:, 0:1]
    valid = tv_ref[:, 1:2]
    _store_grouped(o_ref, out * jnp.exp(-t * delta_ref[...]) * valid)


def conv_kernel_signal(L, D, fw1, fb1, fw2, fb2, fw3, fb3, ffreq, fw4, *, tl=256):
    FH = fw2.shape[0]
    FE = 64
    p = jnp.arange(2 * L, dtype=jnp.int32)
    pos = jnp.where(p < L, p, 2 * L - p).astype(F32)
    t = pos / (L - 1)
    bands = jnp.linspace(1e-4, FILTER_BANDS - 1, FILTER_BANDS, dtype=F32)
    ang = (2.0 * math.pi / L) * pos[:, None] * bands[None, :]
    feat = jnp.concatenate(
        [t[:, None], jnp.cos(ang), -jnp.sin(ang), jnp.zeros((2 * L, FE - FILTER_EMB), F32)], axis=-1)
    tv = jnp.stack([t, (p != L).astype(F32)], axis=-1)
    deltas = np.abs(np.linspace(math.log(DECAY_TARGET) / SLOW_DECAY_PCT, math.log(DECAY_TARGET) / FAST_DECAY_PCT, D,
                                dtype=np.float32)).reshape(1, D)
    w1p = jnp.pad(fw1.astype(F32), ((0, FE - FILTER_EMB), (0, 0)))
    tl = _tile(L, tl, FFT_N2)
    half_tiles = L // tl
    full = lambda shape: pl.BlockSpec(shape, lambda i: (0, 0))
    return pl.pallas_call(
        _filter_kernel,
        grid=(2 * L // tl,),
        in_specs=[
            pl.BlockSpec((tl, FE), lambda i: (i, 0)),
            pl.BlockSpec((tl, 2), lambda i: (i, 0)),
            full((FE, FH)), full((1, FH)), full((FH, FH)), full((1, FH)), full((FH, FH)), full((1, FH)),
            full((1, FH)),
            pl.BlockSpec((FH, D), lambda i: (0, i // half_tiles)),
            full((1, D)),
        ],
        out_specs=pl.BlockSpec((FFT_J, D // LANES, tl // FFT_N2, SUBLANES, LANES), lambda i: (0, 0, i, 0, 0)),
        out_shape=jax.ShapeDtypeStruct((FFT_J, D // LANES, 2 * L // FFT_N2, SUBLANES, LANES), F32),
        compiler_params=_params(("parallel",), 32),
        name="hyena_filter",
    )(feat, tv, w1p, fb1.reshape(1, FH).astype(F32), fw2.astype(F32), fb2.reshape(1, FH).astype(F32),
      fw3.astype(F32), fb3.reshape(1, FH).astype(F32), ffreq.reshape(1, FH).astype(F32), fw4.astype(F32),
      jnp.asarray(deltas))


def _const(m):
    return jnp.asarray(np.asarray(m, np.float64), dtype=F32).astype(BF16)


def _stack(re, im):
    return np.block([[re, -im], [im, re]])


def _dft_tables(n1, n2):
    n, k1 = n1 * n2, n1 // 2
    a1 = 2.0 * np.pi * np.outer(np.arange(n1), np.arange(n1)) / n1
    c1, s1 = np.cos(a1), np.sin(a1)
    a2 = 2.0 * np.pi * np.outer(np.arange(n2), np.arange(n2)) / n2
    c2, s2 = np.cos(a2), np.sin(a2)
    at = 2.0 * np.pi * np.outer(np.arange(n2), np.arange(n1)) / n
    bcast = lambda t: jnp.broadcast_to(jnp.asarray(t, F32)[:, :, None], t.shape + (LANES,))
    return dict(
        f1_real=np.concatenate([c1, -s1], axis=0),
        f1_cplx=_stack(c1[:, :k1], -s1[:, :k1]),
        f2=_stack(c2, -s2),
        g2=_stack(c2, s2),
        g1_cplx=_stack(c1[:k1], s1[:k1]),
        tw1=(bcast(np.cos(at)), bcast(-np.sin(at))),
        tw2=(bcast(np.cos(at.T)), bcast(np.sin(at.T))),
    )


def _fft1_kernel(x_ref, f_ref, twr_ref, twi_ref, ar_ref, ai_ref, *, n1):
    f = f_ref[...]
    parts, cb, rows = x_ref.shape[0], x_ref.shape[1], x_ref.shape[2] // SUBLANES
    for i in range(SUBLANES):
        sel = pl.ds(i, rows, stride=SUBLANES)
        x = jnp.concatenate([jnp.concatenate([x_ref[p, c, sel, :] for c in range(cb)], axis=1)
                             for p in range(parts)], axis=0)
        r = _dotc(f, x)
        re, im = r[:n1], r[n1:]
        twr = jnp.concatenate([twr_ref[i]] * cb, axis=1)
        twi = jnp.concatenate([twi_ref[i]] * cb, axis=1)
        ar = re * twr - im * twi
        ai = re * twi + im * twr
        for c in range(cb):
            ar_ref[c, pl.ds(i, n1, stride=SUBLANES), :] = ar[:, c * LANES:(c + 1) * LANES]
            ai_ref[c, pl.ds(i, n1, stride=SUBLANES), :] = ai[:, c * LANES:(c + 1) * LANES]


def _fft_stage1(x, n1, f1, tw1):
    P, parts, _, C, k8, _ = x.shape
    fc = _const(f1)
    cb = min(C, FFT_LANE_TILES)
    const = lambda shape: pl.BlockSpec(shape, lambda j, p, c: (0,) * len(shape))
    out_spec = pl.BlockSpec((None, None, cb, n1 * SUBLANES, LANES), lambda j, p, c: (p, j, c, 0, 0))
    tw_spec = pl.BlockSpec((SUBLANES, n1, LANES), lambda j, p, c: (j, 0, 0))
    return pl.pallas_call(
        functools.partial(_fft1_kernel, n1=n1),
        grid=(FFT_J, P, C // cb),
        in_specs=[
            pl.BlockSpec((None, parts, None, cb, k8, LANES), lambda j, p, c: (p, 0, j, c, 0, 0)),
            const(f1.shape), tw_spec, tw_spec,
        ],
        out_specs=[out_spec, out_spec],
        out_shape=[jax.ShapeDtypeStruct((P, FFT_J, C, n1 * SUBLANES, LANES), F32)] * 2,
        compiler_params=_params(("parallel", "parallel", "parallel"), 40),
        name="fft_stage1",
    )(x, fc, tw1[0], tw1[1])


def _fft2_kernel(*refs, n2, cc, scale, spectrum_only):
    if spectrum_only:
        ar_ref, ai_ref, f_ref, yr_ref, yi_ref = refs
    else:
        ar_ref, ai_ref, kr_ref, ki_ref, f_ref, g_ref, twr_ref, twi_ref, yr_ref, yi_ref = refs
        twr = jnp.concatenate([twr_ref[...]] * (cc // LANES), axis=1)
        twi = jnp.concatenate([twi_ref[...]] * (cc // LANES), axis=1)
    f = f_ref[...]
    lt = cc // LANES
    rows = lambda ref, c: jnp.concatenate([ref[:, c * lt + t].reshape(n2, LANES) for t in range(lt)], axis=1)
    for c in range(ar_ref.shape[1] // lt):
        cols = slice(c * cc, (c + 1) * cc)
        y = _dotc(f, jnp.concatenate([rows(ar_ref, c), rows(ai_ref, c)], axis=0))
        yr, yi = y[:n2], y[n2:]
        if spectrum_only:
            yr_ref[:, cols] = (yr * scale).astype(yr_ref.dtype)
            yi_ref[:, cols] = (yi * scale).astype(yi_ref.dtype)
        else:
            kr = kr_ref[:, cols].astype(F32)
            ki = ki_ref[:, cols].astype(F32)
            pr = yr * kr - yi * ki
            pi = yr * ki + yi * kr
            q = _dotc(g_ref[...], jnp.concatenate([pr, pi], axis=0))
            qr, qi = q[:n2], q[n2:]
            outr = qr * twr - qi * twi
            outi = qr * twi + qi * twr
            for t in range(lt):
                yr_ref[:, c * lt + t] = outr[:, t * LANES:(t + 1) * LANES].reshape(FFT_J, SUBLANES, LANES)
                yi_ref[:, c * lt + t] = outi[:, t * LANES:(t + 1) * LANES].reshape(FFT_J, SUBLANES, LANES)


def _fft_stage2(ar, ai, n1, n2, D, f2, *, kf=None, g2=None, tw2=None, scale=1.0):
    B, _, C = ar.shape[:3]
    ar = ar.reshape(B, FFT_J, C, n1, SUBLANES, LANES)
    ai = ai.reshape(B, FFT_J, C, n1, SUBLANES, LANES)
    fc = _const(f2)
    cc = _tile(D, 512)
    blk = pl.BlockSpec((None, FFT_J, C, None, SUBLANES, LANES), lambda f, b: (b, 0, 0, f, 0, 0))
    const = lambda shape: pl.BlockSpec(shape, lambda f, b: (0,) * len(shape))
    specs = [blk, blk]
    args = [ar, ai]
    if kf is not None:
        gc = _const(g2)
        kblk = pl.BlockSpec((None, n2, D), lambda f, b: (f, 0, 0))
        tblk = pl.BlockSpec((None, n2, LANES), lambda f, b: (f, 0, 0))
        specs += [kblk, kblk, const((2 * n2, 2 * n2)), const((2 * n2, 2 * n2)), tblk, tblk]
        args += [kf[0], kf[1], fc, gc, tw2[0], tw2[1]]
        out_spec = blk
        out_shape = jax.ShapeDtypeStruct((B, FFT_J, C, n1, SUBLANES, LANES), F32)
    else:
        specs += [const((2 * n2, 2 * n2))]
        args += [fc]
        out_spec = pl.BlockSpec((None, None, n2, D), lambda f, b: (b, f, 0, 0))
        out_shape = jax.ShapeDtypeStruct((B, n1, n2, D), BF16)
    out = pl.pallas_call(
        functools.partial(_fft2_kernel, n2=n2, cc=cc, scale=scale, spectrum_only=kf is None),
        grid=(n1, B),
        in_specs=specs,
        out_specs=[out_spec, out_spec],
        out_shape=[out_shape] * 2,
        compiler_params=_params(("parallel", "parallel"), 40),
        name="fft_stage2",
    )(*args)
    if kf is not None:
        out = [o.reshape(B, FFT_J, C, n1 * SUBLANES, LANES) for o in out]
    return out


def _fft3_kernel(qr_ref, qi_ref, g_ref, z_ref, x0_ref, skip_ref, o_ref, *, n1):
    g = g_ref[...]
    cb, rows = z_ref.shape[1], z_ref.shape[2] // SUBLANES
    for i in range(SUBLANES):
        qsel = pl.ds(i, n1, stride=SUBLANES)
        sel = pl.ds(i, rows, stride=SUBLANES)
        q = jnp.concatenate([jnp.concatenate([ref[c, qsel, :] for c in range(cb)], axis=1)
                             for ref in (qr_ref, qi_ref)], axis=0)
        y = _dotc(g, q)
        for p in range(2):
            for c in range(cb):
                lanes = slice(c * LANES, (c + 1) * LANES)
                conv = y[p * rows:(p + 1) * rows, lanes]
                o_ref[p, c, sel, :] = (conv + z_ref[p, c, sel, :] * skip_ref[:, lanes]) * x0_ref[p, c, sel, :]


def _fft_stage3(qr, qi, z, x0, skip, n1, g1):
    P, _, _, C, k8, _ = z.shape
    gc = _const(g1)
    cb = min(C, FFT_LANE_TILES)
    qblk = pl.BlockSpec((None, None, cb, n1 * SUBLANES, LANES), lambda p, j, c: (p, j, c, 0, 0))
    zblk = pl.BlockSpec((None, 2, None, cb, k8, LANES), lambda p, j, c: (p, 0, j, c, 0, 0))
    const = lambda shape: pl.BlockSpec(shape, lambda p, j, c: (0,) * len(shape))
    return pl.pallas_call(
        functools.partial(_fft3_kernel, n1=n1),
        grid=(P, FFT_J, C // cb),
        in_specs=[qblk, qblk, const(g1.shape), zblk, zblk,
                  pl.BlockSpec((1, cb * LANES), lambda p, j, c: (0, c))],
        out_specs=zblk,
        out_shape=jax.ShapeDtypeStruct(z.shape, F32),
        compiler_params=_params(("parallel", "parallel", "parallel"), 52),
        name="fft_stage3",
    )(qr, qi, gc, z, x0, skip.reshape(1, C * LANES).astype(F32))


def long_conv_gate(z, x0, skip, kern):
    B, _, C, k1 = z.shape[:4]
    D = C * LANES
    n2 = FFT_N2
    n1 = 2 * k1
    assert B % 2 == 0
    t = _dft_tables(n1, n2)
    ka = _fft_stage1(kern.reshape(1, 1, FFT_J, C, n1 * SUBLANES, LANES), n1, t["f1_real"], t["tw1"])
    kf = _fft_stage2(ka[0], ka[1], n1, n2, D, t["f2"], scale=1.0 / (n1 * n2))
    kf = (kf[0].reshape(n1, n2, D), kf[1].reshape(n1, n2, D))
    pairs = lambda v: v.reshape(B // 2, 2, FFT_J, C, k1 * SUBLANES, LANES)
    a = _fft_stage1(pairs(z), n1, t["f1_cplx"], t["tw1"])
    q = _fft_stage2(a[0], a[1], n1, n2, D, t["f2"], kf=kf, g2=t["g2"], tw2=t["tw2"])
    return _fft_stage3(q[0], q[1], pairs(z), pairs(x0), skip, n1, t["g1_cplx"]).reshape(z.shape)


def _rope_tables(S):
    inv = ROPE_THETA ** (-jnp.arange(0, HEAD_DIM, 2, dtype=F32) / HEAD_DIM)
    ang = jnp.arange(S, dtype=F32)[:, None] * inv[None, :]
    cos, sin = jnp.cos(ang), jnp.sin(ang)
    cos2, sin2 = jnp.concatenate([cos, cos], axis=-1), jnp.concatenate([-sin, sin], axis=-1)
    return jnp.stack([cos2, jnp.ones_like(cos2)]), jnp.stack([sin2, jnp.zeros_like(sin2)])


def _trunk(x, p):
    B, S, D = x.shape
    H = D // HEAD_DIM
    G = len(GROUP_DILATIONS)
    T = B * S
    cos, sin = _rope_tables(S)
    x = x.reshape(T, D)
    depth = p["mix_norm"].shape[0]
    for i in range(depth):
        j = i // 2
        if i % 2 == 0:
            scale = HEAD_DIM ** -0.5
            gains = jnp.stack([p["attn_q_gain"][j] * scale, p["attn_k_gain"][j], jnp.ones_like(p["attn_k_gain"][j])],
                              axis=1).reshape(G * 3, 1, HEAD_DIM).astype(F32)
            qkv = [qkv_project(x, p["mix_norm"][i], p["attn_w_qkv"], j, gains, cos, sin, S, g) for g in range(G)]
            o = window_attention(qkv, B, S, H)
            x = matmul_resid(o, p["attn_w_out"], j, x, tm=512, tn=2048)
        else:
            u = norm_matmul(x, p["mix_norm"][i], p["hy_w_in"], j, p["hy_b_in"][j])
            z, x0 = short_conv_gate(u.reshape(B, S, 3 * D), p["hy_conv_w"][j], p["hy_conv_b"][j])
            kern = conv_kernel_signal(S, D, p["hy_filt_w1"][j], p["hy_filt_b1"][j], p["hy_filt_w2"][j],
                                      p["hy_filt_b2"][j], p["hy_filt_w3"][j], p["hy_filt_b3"][j],
                                      p["hy_filt_freq"][j], p["hy_filt_w4"][j])
            y = long_conv_gate(z, x0, p["hy_skip"][j], kern)
            x = matmul_resid(y, p["hy_w_out"], j, x, p["hy_b_out"][j], tm=512, tn=2048)
        h = norm_matmul(x, p["mlp_norm"][i], p["mlp_w1"], i, relu2=True, out_dtype=BF16, tn=2048)
        x = matmul_resid(h, p["mlp_w2"], i, x, tk=4096)
    return x.reshape(B, S, D)


def kernel(x_prompt, x_sample, mix_norm, mlp_norm, attn_w_qkv, attn_q_gain, attn_k_gain, attn_w_out, hy_w_in, hy_b_in, hy_conv_w, hy_conv_b, hy_filt_w1, hy_filt_b1, hy_filt_w2, hy_filt_b2, hy_filt_w3, hy_filt_b3, hy_filt_freq, hy_filt_w4, hy_skip, hy_w_out, hy_b_out, mlp_w1, mlp_w2):
    p = dict(
        mix_norm=mix_norm, mlp_norm=mlp_norm,
        attn_w_qkv=attn_w_qkv.astype(BF16), attn_q_gain=attn_q_gain, attn_k_gain=attn_k_gain,
        attn_w_out=attn_w_out.astype(BF16),
        hy_w_in=hy_w_in.astype(BF16), hy_b_in=hy_b_in, hy_conv_w=hy_conv_w, hy_conv_b=hy_conv_b,
        hy_filt_w1=hy_filt_w1, hy_filt_b1=hy_filt_b1, hy_filt_w2=hy_filt_w2, hy_filt_b2=hy_filt_b2,
        hy_filt_w3=hy_filt_w3, hy_filt_b3=hy_filt_b3, hy_filt_freq=hy_filt_freq, hy_filt_w4=hy_filt_w4,
        hy_skip=hy_skip, hy_w_out=hy_w_out.astype(BF16), hy_b_out=hy_b_out,
        mlp_w1=mlp_w1.astype(BF16), mlp_w2=mlp_w2.astype(BF16),
    )
    return (_trunk(x_prompt, p), _trunk(x_sample, p))
```

```python
import functools
import math

import numpy as np
import jax
import jax.numpy as jnp
from jax import lax
from jax.experimental import pallas as pl
from jax.experimental.pallas import tpu as pltpu

F32 = jnp.float32
BF16 = jnp.bfloat16

HEAD_DIM = 128
GROUP_WINDOWS = (128, 512, 2048)
GROUP_DILATIONS = (1, 4, 16)
ROPE_THETA = 10000.0
FILTER_EMB = 33
FILTER_BANDS = (FILTER_EMB - 1) // 2
DECAY_TARGET = 1e-2
FAST_DECAY_PCT = 0.3
SLOW_DECAY_PCT = 1.5
EPS = 1e-6

LANES = 128
SUBLANES = 8
ATTN_TILE = 1024
ATTN_HEADS = 4
ATTN_BATCH = 8
FFT_N2 = 128
FFT_J = FFT_N2 // SUBLANES
FFT_LANE_TILES = 4
MIB = 1024 * 1024


def _params(semantics, vmem_mib):
    return pltpu.CompilerParams(dimension_semantics=semantics, vmem_limit_bytes=vmem_mib * MIB)


def _tile(n, pref, quantum=LANES):
    if n <= pref:
        return n
    t = (pref // quantum) * quantum
    while t > quantum and n % t:
        t -= quantum
    assert n % t == 0, (n, pref)
    return t


def _split(x):
    hi = x.astype(BF16)
    lo = (x - hi.astype(F32)).astype(BF16)
    return hi, lo


def _dot(a, b):
    return jnp.dot(a, b, preferred_element_type=F32)


def _dot3(a, b):
    ah, al = _split(a)
    bh, bl = _split(b)
    return _dot(ah, bh) + (_dot(ah, bl) + _dot(al, bh))


def _dotc(c, b):
    return _dot(c, b.astype(BF16))


def _rms_rows_to(x_ref, g_ref, xn_ref, row_chunk):
    def body(c, carry):
        rows = pl.ds(pl.multiple_of(c * row_chunk, row_chunk), row_chunk)
        x = x_ref[rows, :]
        inv = lax.rsqrt(jnp.mean(x * x, axis=-1, keepdims=True) + EPS)
        xn_ref[rows, :] = (x * inv * g_ref[...]).astype(BF16)
        return carry

    lax.fori_loop(0, x_ref.shape[0] // row_chunk, body, 0)


def _store_grouped(o_ref, v):
    for c in range(o_ref.shape[1]):
        for a in range(o_ref.shape[2]):
            o_ref[:, c, a] = v[a * FFT_N2:(a + 1) * FFT_N2, c * LANES:(c + 1) * LANES].reshape(FFT_J, SUBLANES, LANES)


def _load_grouped(a_ref):
    rows = [jnp.concatenate([a_ref[:, c, s].reshape(FFT_N2, LANES) for c in range(a_ref.shape[1])], axis=1)
            for s in range(a_ref.shape[2])]
    return rows[0] if len(rows) == 1 else jnp.concatenate(rows, axis=0)


def _norm_mm_kernel(x_ref, g_ref, w_ref, *rest, has_bias, relu2, row_chunk):
    if has_bias:
        b_ref, o_ref, xn_ref = rest
    else:
        o_ref, xn_ref = rest

    @pl.when(pl.program_id(1) == 0)
    def _():
        _rms_rows_to(x_ref, g_ref, xn_ref, row_chunk)

    acc = _dot(xn_ref[...], w_ref[...])
    if has_bias:
        acc = acc + b_ref[...]
    if relu2:
        acc = jnp.maximum(acc, 0.0)
        acc = acc * acc
    o_ref[...] = acc.astype(o_ref.dtype)


def norm_matmul(x, g, w, layer, bias=None, *, relu2=False, out_dtype=F32, tm=1024, tn=1024):
    T, K = x.shape
    N = w.shape[2]
    tm = _tile(T, tm)
    tn = _tile(N, tn)
    in_specs = [
        pl.BlockSpec((tm, K), lambda i, j: (i, 0)),
        pl.BlockSpec((1, K), lambda i, j: (0, 0)),
        pl.BlockSpec((None, K, tn), lambda i, j: (layer, 0, j)),
    ]
    args = [x, g.reshape(1, K), w]
    if bias is not None:
        in_specs.append(pl.BlockSpec((1, tn), lambda i, j: (0, j)))
        args.append(bias.reshape(1, N))
    return pl.pallas_call(
        functools.partial(_norm_mm_kernel, has_bias=bias is not None, relu2=relu2, row_chunk=min(tm, 128)),
        grid=(T // tm, N // tn),
        in_specs=in_specs,
        out_specs=pl.BlockSpec((tm, tn), lambda i, j: (i, j)),
        out_shape=jax.ShapeDtypeStruct((T, N), out_dtype),
        scratch_shapes=[pltpu.VMEM((tm, K), BF16)],
        compiler_params=_params(("parallel", "arbitrary"), 56),
        name="norm_matmul",
    )(*args)


def _qkv_kernel(x_ref, g_ref, w_ref, gain_ref, cos_ref, sin_ref, o_ref, xn_ref, acc_a, acc_b, stage_ref, *,
                row_chunk, tiles_per_part, n_j, n_tiles, d):
    n = pl.program_id(0)
    tile = jnp.minimum(n, n_tiles - 1)
    done = jnp.maximum(n - 1, 0)
    tm = xn_ref.shape[0]

    @pl.when(n == 0)
    def _():
        acc_b[...] = jnp.zeros_like(acc_b)

    @pl.when((tile % n_j == 0) & (n < n_tiles))
    def _():
        _rms_rows_to(x_ref, g_ref, xn_ref, row_chunk)

    def step(acc_w, acc_r):
        acc_w[...] = _dot(xn_ref[...], w_ref[...])
        is_v = (done % n_j) // tiles_per_part == 2
        gain = gain_ref[...]
        cos = cos_ref[...]
        sin = sin_ref[...]
        for s in range(acc_r.shape[1] // HEAD_DIM):
            a = acc_r[:, s * HEAD_DIM:(s + 1) * HEAD_DIM]
            inv = jnp.where(is_v, 1.0, lax.rsqrt(jnp.mean(a * a, axis=-1, keepdims=True) + EPS))
            y = a * inv * gain
            y = y * cos + pltpu.roll(y, HEAD_DIM // 2, axis=1) * sin
            if d == 1:
                o_ref[s] = y.astype(o_ref.dtype)
            else:
                stage_ref[s] = y
                n_r = tm // d
                for r in range(d):
                    o_ref[s, r * n_r:(r + 1) * n_r, :] = stage_ref[s, pl.ds(r, n_r, stride=d), :].astype(o_ref.dtype)

    @pl.when(n % 2 == 0)
    def _():
        step(acc_a, acc_b)

    @pl.when(n % 2 == 1)
    def _():
        step(acc_b, acc_a)


def qkv_project(x, g, w, layer, gains, cos, sin, seq_len, group, *, tn=1024):
    T, K = x.shape
    part_width = w.shape[2] // (len(GROUP_DILATIONS) * 3)
    tm = ATTN_TILE
    assert seq_len % tm == 0
    tn = _tile(part_width, tn)
    pos_tiles = seq_len // tm
    tiles_per_part = part_width // tn
    n_j = 3 * tiles_per_part
    n_tiles = (T // tm) * n_j
    mm = lambda n: jnp.minimum(n, n_tiles - 1)
    ep = lambda n: jnp.maximum(n - 1, 0)
    table = lambda n: ((ep(n) % n_j) // tiles_per_part // 2, (ep(n) // n_j) % pos_tiles, 0)
    return pl.pallas_call(
        functools.partial(_qkv_kernel, row_chunk=min(tm, 128), tiles_per_part=tiles_per_part, n_j=n_j,
                          n_tiles=n_tiles, d=GROUP_DILATIONS[group]),
        grid=(n_tiles + 1,),
        in_specs=[
            pl.BlockSpec((tm, K), lambda n: (mm(n) // n_j, 0)),
            pl.BlockSpec((1, K), lambda n: (0, 0)),
            pl.BlockSpec((None, K, tn), lambda n: (layer, 0, group * n_j + mm(n) % n_j)),
            pl.BlockSpec((None, 1, HEAD_DIM), lambda n: (group * 3 + (ep(n) % n_j) // tiles_per_part, 0, 0)),
            pl.BlockSpec((None, tm, HEAD_DIM), table),
            pl.BlockSpec((None, tm, HEAD_DIM), table),
        ],
        out_specs=pl.BlockSpec(
            (tn // HEAD_DIM, None, tm, HEAD_DIM),
            lambda n: (ep(n) % tiles_per_part, ((ep(n) % n_j) // tiles_per_part + 2) % 3, ep(n) // n_j, 0)),
        out_shape=jax.ShapeDtypeStruct((part_width // HEAD_DIM, 3, T, HEAD_DIM), BF16),
        scratch_shapes=[pltpu.VMEM((tm, K), BF16), pltpu.VMEM((tm, tn), F32), pltpu.VMEM((tm, tn), F32),
                        pltpu.VMEM((tn // HEAD_DIM, tm, HEAD_DIM), F32)],
        compiler_params=_params(("arbitrary",), 52),
        name="qkv_project",
    )(x, g.reshape(1, K), w, gains, cos, sin)


def _mm_resid_kernel(a_ref, w_ref, *rest, has_bias, grouped):
    if has_bias:
        b_ref, r_ref, o_ref = rest
    else:
        r_ref, o_ref = rest
    k = pl.program_id(2)
    a = _load_grouped(a_ref) if grouped else a_ref[...]
    part = _dot(a.astype(BF16), w_ref[...])

    @pl.when(k == 0)
    def _():
        first = part + r_ref[...]
        if has_bias:
            first = first + b_ref[...]
        o_ref[...] = first

    @pl.when(k > 0)
    def _():
        o_ref[...] += part


def matmul_resid(a, w, layer, resid, bias=None, *, tm=1024, tn=1024, tk=2048):
    grouped = a.ndim == 6
    T, N = resid.shape
    K = w.shape[1]
    tn = _tile(N, tn)
    tk = _tile(K, tk)
    if grouped:
        seq_len = a.shape[3] * FFT_N2
        tm = _tile(seq_len, tm)
        seq_tiles = seq_len // tm
        a_spec = pl.BlockSpec((None, FFT_J, tk // LANES, tm // FFT_N2, SUBLANES, LANES),
                              lambda i, j, k: (i // seq_tiles, 0, k, i % seq_tiles, 0, 0))
    else:
        tm = _tile(T, tm)
        a_spec = pl.BlockSpec((tm, tk), lambda i, j, k: (i, k))
    in_specs = [
        a_spec,
        pl.BlockSpec((None, tk, tn), lambda i, j, k: (layer, k, j)),
    ]
    args = [a, w]
    if bias is not None:
        in_specs.append(pl.BlockSpec((1, tn), lambda i, j, k: (0, j)))
        args.append(bias.reshape(1, N))
    in_specs.append(pl.BlockSpec((tm, tn), lambda i, j, k: (i, j)))
    args.append(resid)
    return pl.pallas_call(
        functools.partial(_mm_resid_kernel, has_bias=bias is not None, grouped=grouped),
        grid=(T // tm, N // tn, K // tk),
        in_specs=in_specs,
        out_specs=pl.BlockSpec((tm, tn), lambda i, j, k: (i, j)),
        out_shape=jax.ShapeDtypeStruct((T, N), F32),
        compiler_params=_params(("parallel", "parallel", "arbitrary"), 58),
        name="matmul_resid",
    )(*args)


def _attn_kernel(*refs, tq, dils, half, n_tiles):
    n_in = 3 * len(dils)
    o_ref = refs[n_in]
    for hh in range(refs[0].shape[0]):
        _attn_head([r.at[hh] for r in refs[:n_in]], o_ref.at[:, hh * HEAD_DIM:(hh + 1) * HEAD_DIM], refs[n_in + 1:],
                   tq=tq, dils=dils, half=half, n_tiles=n_tiles)


def _attn_head(refs, o_ref, scratch, *, tq, dils, half, n_tiles):
    G = len(dils)
    cur_refs, prev_refs, next_refs = refs[0:G], refs[G:2 * G], refs[2 * G:3 * G]
    acc_ref, m_ref, l_ref, out_ref, bias_ref = scratch
    i = pl.program_id(2)
    has_prev = i > 0
    has_next = i < n_tiles - 1
    dmax = max(dils)

    for g, d in enumerate(dils):
        n_r = tq // d
        qs = min(128, n_r)
        ks = qs + 2 * half
        n_sub = n_r // qs
        qq = lax.broadcasted_iota(jnp.int32, (qs, ks), 0)
        kk = lax.broadcasted_iota(jnp.int32, (qs, ks), 1)
        band = (kk >= qq) & (kk <= qq + 2 * half)
        prev_ok = (kk >= half) | has_prev
        next_ok = (kk < ks - half) | has_next
        for idx, valid in enumerate((band & prev_ok, band, band & next_ok, band & prev_ok & next_ok)):
            bias_ref[idx, 0:qs, 0:ks] = jnp.where(valid, 0.0, -jnp.inf)

        def window(c, r, u, g=g, n_r=n_r, qs=qs):
            lo, hi = u * qs - half, (u + 1) * qs + half
            pieces = [prev_refs[g][c, r]] if lo < 0 else []
            pieces.append(cur_refs[g][c, r, max(lo, 0):min(hi, n_r), :])
            if hi > n_r:
                pieces.append(next_refs[g][c, r])
            return pieces[0] if len(pieces) == 1 else jnp.concatenate(pieces, axis=0)

        blocks = [(r, u) for r in range(d) for u in range(n_sub)]
        for b0 in range(0, len(blocks), ATTN_BATCH):
            batch = blocks[b0:b0 + ATTN_BATCH]
            edge = lambda u: (3 if n_sub == 1 else 0) if u == 0 else (2 if u == n_sub - 1 else 1)
            q = jnp.stack([cur_refs[g][2, r, u * qs:(u + 1) * qs, :] for r, u in batch])
            kw = jnp.stack([window(0, r, u) for r, u in batch])
            vw = jnp.stack([window(1, r, u) for r, u in batch])
            bias = jnp.stack([bias_ref[edge(u), 0:qs, 0:ks] for _, u in batch])
            s = jnp.einsum("bqd,bkd->bqk", q, kw, preferred_element_type=F32) + bias
            m_blk = jnp.max(s, axis=-1, keepdims=True)
            p = jnp.exp(s - m_blk)
            l_blk = jnp.sum(p, axis=-1, keepdims=True)
            pv = jnp.einsum("bqk,bkd->bqd", p.astype(BF16), vw, preferred_element_type=F32)
            st, pitch = _merge_pitch(dmax // d)
            for b, (r, u) in enumerate(batch):
                base = r * n_r + u * qs
                chunk = qs if st == pitch else st
                for k in range(qs // chunk):
                    src = slice(k * chunk, (k + 1) * chunk)
                    dst = pl.ds(base + k * chunk if st == pitch else (base // st + k) * pitch, chunk)
                    m_ref[g, dst, :] = m_blk[b][src]
                    l_ref[g, dst, :] = l_blk[b][src]
                    acc_ref[g, dst, :] = pv[b][src]

    n_max = tq // dmax
    out_pitch = n_max + SUBLANES
    for r in range(dmax):
        sel = []
        for d in dils:
            st, pitch = _merge_pitch(dmax // d)
            start = (r % d) * (tq // d // st) * pitch + r // d
            sel.append(pl.ds(start, n_max, stride=pitch * (dmax // d) // st) if d < dmax else pl.ds(start, n_max))
        ms = [m_ref[g, sel[g], :] for g in range(G)]
        m = functools.reduce(jnp.maximum, ms)
        ws = [jnp.exp(mg - m) for mg in ms]
        num = sum(ws[g] * acc_ref[g, sel[g], :] for g in range(G))
        den = sum(ws[g] * l_ref[g, sel[g], :] for g in range(G))
        out_ref[r * out_pitch:r * out_pitch + n_max, :] = num / den
    o_ref[...] = jnp.concatenate([out_ref[pl.ds(j, dmax, stride=out_pitch), :] for j in range(n_max)],
                                 axis=0).astype(o_ref.dtype)


def _merge_pitch(stride):
    if stride % SUBLANES:
        return SUBLANES, SUBLANES
    return stride, stride + SUBLANES


def window_attention(qkv, batch, seq_len, n_heads):
    G = len(GROUP_DILATIONS)
    tq = ATTN_TILE
    assert seq_len % tq == 0
    n_tiles = seq_len // tq
    total_tiles = batch * n_tiles
    half = GROUP_WINDOWS[0] // (2 * GROUP_DILATIONS[0])
    for wdw, d in zip(GROUP_WINDOWS, GROUP_DILATIONS):
        assert wdw // (2 * d) == half and half * d <= tq and tq % (d * min(128, tq // d)) == 0
    H = n_heads
    dils = GROUP_DILATIONS

    cur, prev, nxt = [], [], []
    hb = ATTN_HEADS if H % ATTN_HEADS == 0 else 1
    for d in dils:
        cur.append(pl.BlockSpec((hb, 3, None, d, tq // d, HEAD_DIM),
                                lambda b, h, i: (h, 0, b * n_tiles + i, 0, 0, 0)))
        prev.append(pl.BlockSpec((hb, 2, None, d, half, HEAD_DIM),
                                 lambda b, h, i, d=d: (h, 0, jnp.maximum(b * n_tiles + i - 1, 0), 0,
                                                       tq // d // half - 1, 0)))
        nxt.append(pl.BlockSpec((hb, 2, None, d, half, HEAD_DIM),
                                lambda b, h, i: (h, 0, jnp.minimum(b * n_tiles + i + 1, total_tiles - 1), 0, 0, 0)))
    views = [t.reshape(H, 3, total_tiles, d, tq // d, HEAD_DIM) for t, d in zip(qkv, dils)]
    merge_rows = max(tq // st * pitch for st, pitch in (_merge_pitch(max(dils) // d) for d in dils))
    in_specs = cur + prev + nxt
    return pl.pallas_call(
        functools.partial(_attn_kernel, tq=tq, dils=dils, half=half, n_tiles=n_tiles),
        grid=(batch, H // hb, n_tiles),
        in_specs=in_specs,
        out_specs=pl.BlockSpec((tq, hb * HEAD_DIM), lambda b, h, i: (b * n_tiles + i, h)),
        out_shape=jax.ShapeDtypeStruct((batch * seq_len, H * HEAD_DIM), BF16),
        scratch_shapes=[
            pltpu.VMEM((G, merge_rows, HEAD_DIM), F32),
            pltpu.VMEM((G, merge_rows, 1), F32),
            pltpu.VMEM((G, merge_rows, 1), F32),
            pltpu.VMEM((max(dils) * (tq // max(dils) + SUBLANES), HEAD_DIM), F32),
            pltpu.VMEM((4, 128, 128 + 2 * half), F32),
        ],
        compiler_params=_params(("parallel", "parallel", "parallel"), 40),
        name="window_attention",
    )(*(views * 3))


def _sconv_kernel(*refs, n_tiles):
    (x0m, x0p, x0n, x1m, x1p, x1n, vm, vp, vn, w0, w1, w2, b0, b1, b2, z_ref, x0_ref) = refs
    i = pl.program_id(1)

    def conv(m_ref, p_ref, n_ref, w_ref, b_ref):
        x = m_ref[...]
        ts = x.shape[0]
        prev_row = jnp.where(i > 0, p_ref[7:8, :], 0.0)
        next_row = jnp.where(i < n_tiles - 1, n_ref[0:1, :], 0.0)
        row = lax.broadcasted_iota(jnp.int32, x.shape, 0)
        up = jnp.where(row == 0, prev_row, pltpu.roll(x, 1, axis=0))
        dn = jnp.where(row == ts - 1, next_row, pltpu.roll(x, ts - 1, axis=0))
        return up * w_ref[0:1, :] + x * w_ref[1:2, :] + dn * w_ref[2:3, :] + b_ref[...]

    _store_grouped(x0_ref, conv(x0m, x0p, x0n, w0, b0))
    _store_grouped(z_ref, conv(vm, vp, vn, w2, b2) * conv(x1m, x1p, x1n, w1, b1))


def short_conv_gate(u, conv_w, conv_b, *, ts=512, ct=512):
    B, L, D3 = u.shape
    D = D3 // 3
    ts = _tile(L, ts, FFT_N2)
    ct = _tile(D, ct)
    n_tiles = L // ts
    n_ct = D // ct
    rows8 = ts // 8
    last8 = L // 8 - 1
    specs, args = [], []
    for part in range(3):
        off = part * n_ct
        specs += [
            pl.BlockSpec((None, ts, ct), lambda b, i, j, off=off: (b, i, off + j)),
            pl.BlockSpec((None, 8, ct), lambda b, i, j, off=off: (b, jnp.maximum(i * rows8 - 1, 0), off + j)),
            pl.BlockSpec((None, 8, ct), lambda b, i, j, off=off: (b, jnp.minimum((i + 1) * rows8, last8), off + j)),
        ]
        args += [u, u, u]
    for part in range(3):
        specs.append(pl.BlockSpec((3, ct), lambda b, i, j, off=part * n_ct: (0, off + j)))
        args.append(conv_w)
    for part in range(3):
        specs.append(pl.BlockSpec((1, ct), lambda b, i, j, off=part * n_ct: (0, off + j)))
        args.append(conv_b.reshape(1, D3))
    out_spec = pl.BlockSpec((None, FFT_J, ct // LANES, ts // FFT_N2, SUBLANES, LANES),
                            lambda b, i, j: (b, 0, j, i, 0, 0))
    return pl.pallas_call(
        functools.partial(_sconv_kernel, n_tiles=n_tiles),
        grid=(B, n_tiles, n_ct),
        in_specs=specs,
        out_specs=[out_spec, out_spec],
        out_shape=[jax.ShapeDtypeStruct((B, FFT_J, D // LANES, L // FFT_N2, SUBLANES, LANES), F32)] * 2,
        compiler_params=_params(("parallel", "parallel", "parallel"), 32),
        name="short_conv_gate",
    )(*args)


def _filter_kernel(feat_ref, tv_ref, w1, b1, w2, b2, w3, b3, fr_ref, w4, delta_ref, o_ref):
    fr = fr_ref[...]
    h = jnp.sin(fr * (_dot3(feat_ref[...], w1[...]) + b1[...]))
    h = jnp.sin(fr * (_dot3(h, w2[...]) + b2[...]))
    h = jnp.sin(fr * (_dot3(h, w3[...]) + b3[...]))
    out = _dot3(h, w4[...])
    t = tv_ref[:, 0:1]
    valid = tv_ref[:, 1:2]
    _store_grouped(o_ref, out * jnp.exp(-t * delta_ref[...]) * valid)


def conv_kernel_signal(L, D, fw1, fb1, fw2, fb2, fw3, fb3, ffreq, fw4, *, tl=256):
    FH = fw2.shape[0]
    FE = 64
    p = jnp.arange(2 * L, dtype=jnp.int32)
    pos = jnp.where(p < L, p, 2 * L - p).astype(F32)
    t = pos / (L - 1)
    bands = jnp.linspace(1e-4, FILTER_BANDS - 1, FILTER_BANDS, dtype=F32)
    ang = (2.0 * math.pi / L) * pos[:, None] * bands[None, :]
    feat = jnp.concatenate(
        [t[:, None], jnp.cos(ang), -jnp.sin(ang), jnp.zeros((2 * L, FE - FILTER_EMB), F32)], axis=-1)
    tv = jnp.stack([t, (p != L).astype(F32)], axis=-1)
    deltas = np.abs(np.linspace(math.log(DECAY_TARGET) / SLOW_DECAY_PCT, math.log(DECAY_TARGET) / FAST_DECAY_PCT, D,
                                dtype=np.float32)).reshape(1, D)
    w1p = jnp.pad(fw1.astype(F32), ((0, FE - FILTER_EMB), (0, 0)))
    tl = _tile(L, tl, FFT_N2)
    half_tiles = L // tl
    full = lambda shape: pl.BlockSpec(shape, lambda i: (0, 0))
    return pl.pallas_call(
        _filter_kernel,
        grid=(2 * L // tl,),
        in_specs=[
            pl.BlockSpec((tl, FE), lambda i: (i, 0)),
            pl.BlockSpec((tl, 2), lambda i: (i, 0)),
            full((FE, FH)), full((1, FH)), full((FH, FH)), full((1, FH)), full((FH, FH)), full((1, FH)),
            full((1, FH)),
            pl.BlockSpec((FH, D), lambda i: (0, i // half_tiles)),
            full((1, D)),
        ],
        out_specs=pl.BlockSpec((FFT_J, D // LANES, tl // FFT_N2, SUBLANES, LANES), lambda i: (0, 0, i, 0, 0)),
        out_shape=jax.ShapeDtypeStruct((FFT_J, D // LANES, 2 * L // FFT_N2, SUBLANES, LANES), F32),
        compiler_params=_params(("parallel",), 32),
        name="hyena_filter",
    )(feat, tv, w1p, fb1.reshape(1, FH).astype(F32), fw2.astype(F32), fb2.reshape(1, FH).astype(F32),
      fw3.astype(F32), fb3.reshape(1, FH).astype(F32), ffreq.reshape(1, FH).astype(F32), fw4.astype(F32),
      jnp.asarray(deltas))


def _const(m):
    return jnp.asarray(np.asarray(m, np.float64), dtype=F32).astype(BF16)


def _stack(re, im):
    return np.block([[re, -im], [im, re]])


def _dft_tables(n1, n2):
    n, k1 = n1 * n2, n1 // 2
    a1 = 2.0 * np.pi * np.outer(np.arange(n1), np.arange(n1)) / n1
    c1, s1 = np.cos(a1), np.sin(a1)
    a2 = 2.0 * np.pi * np.outer(np.arange(n2), np.arange(n2)) / n2
    c2, s2 = np.cos(a2), np.sin(a2)
    at = 2.0 * np.pi * np.outer(np.arange(n2), np.arange(n1)) / n
    bcast = lambda t: jnp.broadcast_to(jnp.asarray(t, F32)[:, :, None], t.shape + (LANES,))
    return dict(
        f1_real=np.concatenate([c1, -s1], axis=0),
        f1_cplx=_stack(c1[:, :k1], -s1[:, :k1]),
        f2=_stack(c2, -s2),
        g2=_stack(c2, s2),
        g1_cplx=_stack(c1[:k1], s1[:k1]),
        tw1=(bcast(np.cos(at)), bcast(-np.sin(at))),
        tw2=(bcast(np.cos(at.T)), bcast(np.sin(at.T))),
    )


def _fft1_kernel(x_ref, f_ref, twr_ref, twi_ref, ar_ref, ai_ref, *, n1):
    f = f_ref[...]
    parts, cb, rows = x_ref.shape[0], x_ref.shape[1], x_ref.shape[2] // SUBLANES
    for i in range(SUBLANES):
        sel = pl.ds(i, rows, stride=SUBLANES)
        x = jnp.concatenate([jnp.concatenate([x_ref[p, c, sel, :] for c in range(cb)], axis=1)
                             for p in range(parts)], axis=0)
        r = _dotc(f, x)
        re, im = r[:n1], r[n1:]
        twr = jnp.concatenate([twr_ref[i]] * cb, axis=1)
        twi = jnp.concatenate([twi_ref[i]] * cb, axis=1)
        ar = re * twr - im * twi
        ai = re * twi + im * twr
        for c in range(cb):
            ar_ref[c, pl.ds(i, n1, stride=SUBLANES), :] = ar[:, c * LANES:(c + 1) * LANES]
            ai_ref[c, pl.ds(i, n1, stride=SUBLANES), :] = ai[:, c * LANES:(c + 1) * LANES]


def _fft_stage1(x, n1, f1, tw1):
    P, parts, _, C, k8, _ = x.shape
    fc = _const(f1)
    cb = min(C, FFT_LANE_TILES)
    const = lambda shape: pl.BlockSpec(shape, lambda j, p, c: (0,) * len(shape))
    out_spec = pl.BlockSpec((None, None, cb, n1 * SUBLANES, LANES), lambda j, p, c: (p, j, c, 0, 0))
    tw_spec = pl.BlockSpec((SUBLANES, n1, LANES), lambda j, p, c: (j, 0, 0))
    return pl.pallas_call(
        functools.partial(_fft1_kernel, n1=n1),
        grid=(FFT_J, P, C // cb),
        in_specs=[
            pl.BlockSpec((None, parts, None, cb, k8, LANES), lambda j, p, c: (p, 0, j, c, 0, 0)),
            const(f1.shape), tw_spec, tw_spec,
        ],
        out_specs=[out_spec, out_spec],
        out_shape=[jax.ShapeDtypeStruct((P, FFT_J, C, n1 * SUBLANES, LANES), F32)] * 2,
        compiler_params=_params(("parallel", "parallel", "parallel"), 40),
        name="fft_stage1",
    )(x, fc, tw1[0], tw1[1])


def _fft2_kernel(*refs, n2, cc, scale, spectrum_only):
    if spectrum_only:
        ar_ref, ai_ref, f_ref, yr_ref, yi_ref = refs
    else:
        ar_ref, ai_ref, kr_ref, ki_ref, f_ref, g_ref, twr_ref, twi_ref, yr_ref, yi_ref = refs
        twr = jnp.concatenate([twr_ref[...]] * (cc // LANES), axis=1)
        twi = jnp.concatenate([twi_ref[...]] * (cc // LANES), axis=1)
    f = f_ref[...]
    lt = cc // LANES
    rows = lambda ref, c: jnp.concatenate([ref[:, c * lt + t].reshape(n2, LANES) for t in range(lt)], axis=1)
    for c in range(ar_ref.shape[1] // lt):
        cols = slice(c * cc, (c + 1) * cc)
        y = _dotc(f, jnp.concatenate([rows(ar_ref, c), rows(ai_ref, c)], axis=0))
        yr, yi = y[:n2], y[n2:]
        if spectrum_only:
            yr_ref[:, cols] = (yr * scale).astype(yr_ref.dtype)
            yi_ref[:, cols] = (yi * scale).astype(yi_ref.dtype)
        else:
            kr = kr_ref[:, cols].astype(F32)
            ki = ki_ref[:, cols].astype(F32)
            pr = yr * kr - yi * ki
            pi = yr * ki + yi * kr
            q = _dotc(g_ref[...], jnp.concatenate([pr, pi], axis=0))
            qr, qi = q[:n2], q[n2:]
            outr = qr * twr - qi * twi
            outi = qr * twi + qi * twr
            for t in range(lt):
                yr_ref[:, c * lt + t] = outr[:, t * LANES:(t + 1) * LANES].reshape(FFT_J, SUBLANES, LANES)
                yi_ref[:, c * lt + t] = outi[:, t * LANES:(t + 1) * LANES].reshape(FFT_J, SUBLANES, LANES)


def _fft_stage2(ar, ai, n1, n2, D, f2, *, kf=None, g2=None, tw2=None, scale=1.0):
    B, _, C = ar.shape[:3]
    ar = ar.reshape(B, FFT_J, C, n1, SUBLANES, LANES)
    ai = ai.reshape(B, FFT_J, C, n1, SUBLANES, LANES)
    fc = _const(f2)
    cc = _tile(D, 512)
    blk = pl.BlockSpec((None, FFT_J, C, None, SUBLANES, LANES), lambda f, b: (b, 0, 0, f, 0, 0))
    const = lambda shape: pl.BlockSpec(shape, lambda f, b: (0,) * len(shape))
    specs = [blk, blk]
    args = [ar, ai]
    if kf is not None:
        gc = _const(g2)
        kblk = pl.BlockSpec((None, n2, D), lambda f, b: (f, 0, 0))
        tblk = pl.BlockSpec((None, n2, LANES), lambda f, b: (f, 0, 0))
        specs += [kblk, kblk, const((2 * n2, 2 * n2)), const((2 * n2, 2 * n2)), tblk, tblk]
        args += [kf[0], kf[1], fc, gc, tw2[0], tw2[1]]
        out_spec = blk
        out_shape = jax.ShapeDtypeStruct((B, FFT_J, C, n1, SUBLANES, LANES), F32)
    else:
        specs += [const((2 * n2, 2 * n2))]
        args += [fc]
        out_spec = pl.BlockSpec((None, None, n2, D), lambda f, b: (b, f, 0, 0))
        out_shape = jax.ShapeDtypeStruct((B, n1, n2, D), BF16)
    out = pl.pallas_call(
        functools.partial(_fft2_kernel, n2=n2, cc=cc, scale=scale, spectrum_only=kf is None),
        grid=(n1, B),
        in_specs=specs,
        out_specs=[out_spec, out_spec],
        out_shape=[out_shape] * 2,
        compiler_params=_params(("parallel", "parallel"), 40),
        name="fft_stage2",
    )(*args)
    if kf is not None:
        out = [o.reshape(B, FFT_J, C, n1 * SUBLANES, LANES) for o in out]
    return out


def _fft3_kernel(qr_ref, qi_ref, g_ref, z_ref, x0_ref, skip_ref, o_ref, *, n1):
    g = g_ref[...]
    cb, rows = z_ref.shape[1], z_ref.shape[2] // SUBLANES
    for i in range(SUBLANES):
        qsel = pl.ds(i, n1, stride=SUBLANES)
        sel = pl.ds(i, rows, stride=SUBLANES)
        q = jnp.concatenate([jnp.concatenate([ref[c, qsel, :] for c in range(cb)], axis=1)
                             for ref in (qr_ref, qi_ref)], axis=0)
        y = _dotc(g, q)
        for p in range(2):
            for c in range(cb):
                lanes = slice(c * LANES, (c + 1) * LANES)
                conv = y[p * rows:(p + 1) * rows, lanes]
                o_ref[p, c, sel, :] = (conv + z_ref[p, c, sel, :] * skip_ref[:, lanes]) * x0_ref[p, c, sel, :]


def _fft_stage3(qr, qi, z, x0, skip, n1, g1):
    P, _, _, C, k8, _ = z.shape
    gc = _const(g1)
    cb = min(C, FFT_LANE_TILES)
    qblk = pl.BlockSpec((None, None, cb, n1 * SUBLANES, LANES), lambda p, j, c: (p, j, c, 0, 0))
    zblk = pl.BlockSpec((None, 2, None, cb, k8, LANES), lambda p, j, c: (p, 0, j, c, 0, 0))
    const = lambda shape: pl.BlockSpec(shape, lambda p, j, c: (0,) * len(shape))
    return pl.pallas_call(
        functools.partial(_fft3_kernel, n1=n1),
        grid=(P, FFT_J, C // cb),
        in_specs=[qblk, qblk, const(g1.shape), zblk, zblk,
                  pl.BlockSpec((1, cb * LANES), lambda p, j, c: (0, c))],
        out_specs=zblk,
        out_shape=jax.ShapeDtypeStruct(z.shape, F32),
        compiler_params=_params(("parallel", "parallel", "parallel"), 52),
        name="fft_stage3",
    )(qr, qi, gc, z, x0, skip.reshape(1, C * LANES).astype(F32))


def long_conv_gate(z, x0, skip, kern):
    B, _, C, k1 = z.shape[:4]
    D = C * LANES
    n2 = FFT_N2
    n1 = 2 * k1
    assert B % 2 == 0
    t = _dft_tables(n1, n2)
    ka = _fft_stage1(kern.reshape(1, 1, FFT_J, C, n1 * SUBLANES, LANES), n1, t["f1_real"], t["tw1"])
    kf = _fft_stage2(ka[0], ka[1], n1, n2, D, t["f2"], scale=1.0 / (n1 * n2))
    kf = (kf[0].reshape(n1, n2, D), kf[1].reshape(n1, n2, D))
    pairs = lambda v: v.reshape(B // 2, 2, FFT_J, C, k1 * SUBLANES, LANES)
    a = _fft_stage1(pairs(z), n1, t["f1_cplx"], t["tw1"])
    q = _fft_stage2(a[0], a[1], n1, n2, D, t["f2"], kf=kf, g2=t["g2"], tw2=t["tw2"])
    return _fft_stage3(q[0], q[1], pairs(z), pairs(x0), skip, n1, t["g1_cplx"]).reshape(z.shape)


def _rope_tables(S):
    inv = ROPE_THETA ** (-jnp.arange(0, HEAD_DIM, 2, dtype=F32) / HEAD_DIM)
    ang = jnp.arange(S, dtype=F32)[:, None] * inv[None, :]
    cos, sin = jnp.cos(ang), jnp.sin(ang)
    cos2, sin2 = jnp.concatenate([cos, cos], axis=-1), jnp.concatenate([-sin, sin], axis=-1)
    return jnp.stack([cos2, jnp.ones_like(cos2)]), jnp.stack([sin2, jnp.zeros_like(sin2)])


def _trunk(x, p):
    B, S, D = x.shape
    H = D // HEAD_DIM
    G = len(GROUP_DILATIONS)
    T = B * S
    cos, sin = _rope_tables(S)
    x = x.reshape(T, D)
    depth = p["mix_norm"].shape[0]
    for i in range(depth):
        j = i // 2
        if i % 2 == 0:
            scale = HEAD_DIM ** -0.5
            gains = jnp.stack([p["attn_q_gain"][j] * scale, p["attn_k_gain"][j], jnp.ones_like(p["attn_k_gain"][j])],
                              axis=1).reshape(G * 3, 1, HEAD_DIM).astype(F32)
            qkv = [qkv_project(x, p["mix_norm"][i], p["attn_w_qkv"], j, gains, cos, sin, S, g) for g in range(G)]
            o = window_attention(qkv, B, S, H)
            x = matmul_resid(o, p["attn_w_out"], j, x, tm=512, tn=2048)
        else:
            u = norm_matmul(x, p["mix_norm"][i], p["hy_w_in"], j, p["hy_b_in"][j])
            z, x0 = short_conv_gate(u.reshape(B, S, 3 * D), p["hy_conv_w"][j], p["hy_conv_b"][j])
            kern = conv_kernel_signal(S, D, p["hy_filt_w1"][j], p["hy_filt_b1"][j], p["hy_filt_w2"][j],
                                      p["hy_filt_b2"][j], p["hy_filt_w3"][j], p["hy_filt_b3"][j],
                                      p["hy_filt_freq"][j], p["hy_filt_w4"][j])
            y = long_conv_gate(z, x0, p["hy_skip"][j], kern)
            x = matmul_resid(y, p["hy_w_out"], j, x, p["hy_b_out"][j], tm=512, tn=2048)
        h = norm_matmul(x, p["mlp_norm"][i], p["mlp_w1"], i, relu2=True, out_dtype=BF16, tn=2048)
        x = matmul_resid(h, p["mlp_w2"], i, x, tk=4096)
    return x.reshape(B, S, D)


def kernel(x_prompt, x_sample, mix_norm, mlp_norm, attn_w_qkv, attn_q_gain, attn_k_gain, attn_w_out, hy_w_in, hy_b_in, hy_conv_w, hy_conv_b, hy_filt_w1, hy_filt_b1, hy_filt_w2, hy_filt_b2, hy_filt_w3, hy_filt_b3, hy_filt_freq, hy_filt_w4, hy_skip, hy_w_out, hy_b_out, mlp_w1, mlp_w2):
    p = dict(
        mix_norm=mix_norm, mlp_norm=mlp_norm,
        attn_w_qkv=attn_w_qkv.astype(BF16), attn_q_gain=attn_q_gain, attn_k_gain=attn_k_gain,
        attn_w_out=attn_w_out.astype(BF16),
        hy_w_in=hy_w_in.astype(BF16), hy_b_in=hy_b_in, hy_conv_w=hy_conv_w, hy_conv_b=hy_conv_b,
        hy_filt_w1=hy_filt_w1, hy_filt_b1=hy_filt_b1, hy_filt_w2=hy_filt_w2, hy_filt_b2=hy_filt_b2,
        hy_filt_w3=hy_filt_w3, hy_filt_b3=hy_filt_b3, hy_filt_freq=hy_filt_freq, hy_filt_w4=hy_filt_w4,
        hy_skip=hy_skip, hy_w_out=hy_w_out.astype(BF16), hy_b_out=hy_b_out,
        mlp_w1=mlp_w1.astype(BF16), mlp_w2=mlp_w2.astype(BF16),
    )
    return (_trunk(x_prompt, p), _trunk(x_sample, p))
```
